```python
import jax, jax.numpy as jnp
from jax import lax
import numpy as np

D_MODEL = 1024
BATCH = 32
SEQ = 256
DEPTH = 4
DEC_BATCH = 2
DEC_SEQ = 2048
PAST_LEN = 512

GRID_W = 64
HEAD_DIM = 64
N_HEADS = 8
N_KV_HEADS = 2
ATTN_WIDTH = N_HEADS * HEAD_DIM
KV_WIDTH = N_KV_HEADS * HEAD_DIM
POOL_WIDTH = D_MODEL - ATTN_WIDTH
POOL_WINDOWS = (2, 4, 8, 16)
N_POOL_GROUPS = 4
POOL_GROUP_DIM = POOL_WIDTH // N_POOL_GROUPS
IN_WIDTH = ATTN_WIDTH + 2 * KV_WIDTH + POOL_WIDTH
D_FF = 2816
CONV_WIDTH = 3
Q_BLOCK = 128
ROPE_THETA = 10000.0
EPS = 1e-6
N_MOD = 6

kernel_name = 'hybrid_attn_pool_diffusion_trunk'


def rms_norm(x, g):
    xf = x.astype(jnp.float32)
    y = xf * lax.rsqrt(jnp.mean(xf * xf, axis=-1, keepdims=True) + EPS)
    return (y * g.astype(jnp.float32)).astype(x.dtype)


def modulation(cond, w_mod, b_mod):
    m = jax.nn.silu(cond) @ w_mod + b_mod
    return jnp.split(m[:, None, :], N_MOD, axis=-1)


def axial_rope_tables(n_tokens):
    t = jnp.arange(n_tokens)
    row = (t // GRID_W).astype(jnp.float32)
    col = (t % GRID_W).astype(jnp.float32)
    n_freq = HEAD_DIM // 4
    inv = ROPE_THETA ** (-jnp.arange(n_freq, dtype=jnp.float32) / n_freq)
    ang = jnp.concatenate([row[:, None] * inv, col[:, None] * inv], axis=-1)
    return jnp.cos(ang), jnp.sin(ang)


def apply_axial_rope(x, cos, sin):
    b, l, h, d = x.shape
    xf = x.astype(jnp.float32).reshape(b, l, h, 2, 2, d // 4)
    x1, x2 = xf[..., 0, :], xf[..., 1, :]
    c = cos.reshape(l, 1, 2, d // 4)
    s = sin.reshape(l, 1, 2, d // 4)
    out = jnp.stack([x1 * c - x2 * s, x1 * s + x2 * c], axis=-2)
    return out.reshape(b, l, h, d).astype(x.dtype)


def block_attention(q, k, v):
    b, l, h, d = q.shape
    n_blk = l // Q_BLOCK
    grp = h // N_KV_HEADS
    qb = q.reshape(b, n_blk, Q_BLOCK, N_KV_HEADS, grp, d).transpose(1, 0, 2, 3, 4, 5)
    kf = k.astype(jnp.float32)
    vf = v.astype(jnp.float32)
    scale = d ** -0.5

    def one_block(qi):
        s = jnp.einsum('bqkgd,bskd->bkgqs', qi.astype(jnp.float32), kf) * scale
        p = jax.nn.softmax(s, axis=-1)
        return jnp.einsum('bkgqs,bskd->bqkgd', p, vf).astype(q.dtype)

    ob = lax.map(one_block, qb)
    return ob.transpose(1, 0, 2, 3, 4, 5).reshape(b, l, h * d)


def pool_mixer(u, w_pool, pool_scale):
    b, l, _ = u.shape
    ug = u.astype(jnp.float32).reshape(b, l, N_POOL_GROUPS, POOL_GROUP_DIM)
    csum = jnp.pad(jnp.cumsum(ug, axis=1), ((0, 0), (1, 0), (0, 0), (0, 0)))
    t = jnp.arange(l)
    outs = []
    for gi, w in enumerate(POOL_WINDOWS):
        lo = jnp.clip(t - w // 2, 0, l)
        hi = jnp.clip(t + (w - w // 2), 0, l)
        total = csum[:, hi, gi] - csum[:, lo, gi]
        cnt = (hi - lo).astype(jnp.float32)[None, :, None]
        outs.append(total / cnt - ug[:, :, gi])
    pooled = jnp.stack(outs, axis=2)
    mixed = jnp.einsum('blgc,gcd->blgd', pooled, w_pool.astype(jnp.float32))
    return (mixed.reshape(b, l, POOL_WIDTH) * pool_scale).astype(u.dtype)


def conv_ffn(h, w_up, conv_w, conv_b, w_down):
    z = h @ w_up
    zp = jnp.pad(z, ((0, 0), (1, 1), (0, 0)))
    z = zp[:, :-2] * conv_w[0] + zp[:, 1:-1] * conv_w[1] + zp[:, 2:] * conv_w[2] + conv_b
    a, g = jnp.split(z, 2, axis=-1)
    return (jax.nn.silu(a) * g) @ w_down


def mixer_projections(h, w_in, q_norm_g, k_norm_g):
    b, l, _ = h.shape
    proj = h @ w_in
    q, k, v, u = jnp.split(proj, [ATTN_WIDTH, ATTN_WIDTH + KV_WIDTH, ATTN_WIDTH + 2 * KV_WIDTH], axis=-1)
    q = rms_norm(q.reshape(b, l, N_HEADS, HEAD_DIM), q_norm_g)
    k = rms_norm(k.reshape(b, l, N_KV_HEADS, HEAD_DIM), k_norm_g)
    v = v.reshape(b, l, N_KV_HEADS, HEAD_DIM)
    return q, k, v, u


def trunk_layer(x, cond, ctx_k, ctx_v, p):
    sh1, sc1, gt1, sh2, sc2, gt2 = modulation(cond, p['w_mod'], p['b_mod'])
    h = rms_norm(x, p['norm1_g']) * (1.0 + sc1) + sh1
    q, k, v, u = mixer_projections(h, p['w_in'], p['q_norm_g'], p['k_norm_g'])
    if ctx_k is None:
        attn = block_attention(q, k, v)
        new_k, new_v = k, v
    else:
        cos, sin = axial_rope_tables(x.shape[1])
        q = apply_axial_rope(q, cos, sin)
        k = apply_axial_rope(k, cos, sin)
        keys = jnp.concatenate([ctx_k.astype(k.dtype), k], axis=1)
        vals = jnp.concatenate([ctx_v.astype(v.dtype), v], axis=1)
        attn = block_attention(q, keys, vals)
        new_k, new_v = None, None
    pooled = pool_mixer(u, p['w_pool'], p['pool_scale'])
    mix = jnp.concatenate([attn, pooled.astype(attn.dtype)], axis=-1) @ p['w_out']
    x = x + gt1 * mix
    h2 = rms_norm(x, p['norm2_g']) * (1.0 + sc2) + sh2
    x = x + gt2 * conv_ffn(h2, p['w_up'], p['conv_w'], p['conv_b'], p['w_down'])
    return x, new_k, new_v


def setup_inputs(seed: int = 0) -> dict:
    key = jax.random.key(seed)
    ks = jax.random.split(key, 24)
    nrm = jax.random.normal
    f32 = jnp.float32
    d = D_MODEL
    return {
        'x_prompt': nrm(ks[0], (BATCH, SEQ, d), f32),
        'x_sample': nrm(ks[1], (DEC_BATCH, DEC_SEQ, d), f32),
        'cache_k': nrm(ks[2], (DEC_BATCH, DEPTH, PAST_LEN, N_KV_HEADS, HEAD_DIM), f32),
        'cache_v': nrm(ks[3], (DEC_BATCH, DEPTH, PAST_LEN, N_KV_HEADS, HEAD_DIM), f32),
        'c': nrm(ks[4], (DEC_BATCH, d), f32),
        'c_ctx': nrm(ks[5], (d,), f32),
        'w_mod': nrm(ks[6], (DEPTH, d, N_MOD * d), f32) * (0.5 * d ** -0.5),
        'b_mod': nrm(ks[7], (DEPTH, N_MOD * d), f32) * 0.01,
        'norm1_g': 1.0 + 0.05 * nrm(ks[8], (DEPTH, d), f32),
        'w_in': nrm(ks[9], (DEPTH, d, IN_WIDTH), f32) * d ** -0.5,
        'q_norm_g': 1.0 + 0.05 * nrm(ks[10], (DEPTH, HEAD_DIM), f32),
        'k_norm_g': 1.0 + 0.05 * nrm(ks[11], (DEPTH, HEAD_DIM), f32),
        'w_pool': nrm(ks[12], (DEPTH, N_POOL_GROUPS, POOL_GROUP_DIM, POOL_GROUP_DIM), f32) * POOL_GROUP_DIM ** -0.5,
        'pool_scale': 1.0 + 0.05 * nrm(ks[13], (DEPTH, POOL_WIDTH), f32),
        'w_out': nrm(ks[14], (DEPTH, d, d), f32) * d ** -0.5,
        'norm2_g': 1.0 + 0.05 * nrm(ks[15], (DEPTH, d), f32),
        'w_up': nrm(ks[16], (DEPTH, d, 2 * D_FF), f32) * d ** -0.5,
        'conv_w': nrm(ks[17], (DEPTH, CONV_WIDTH, 2 * D_FF), f32) * CONV_WIDTH ** -0.5,
        'conv_b': nrm(ks[18], (DEPTH, 2 * D_FF), f32) * 0.01,
        'w_down': nrm(ks[19], (DEPTH, D_FF, d), f32) * D_FF ** -0.5,
        'final_norm_g': 1.0 + 0.05 * nrm(ks[20], (d,), f32),
    }


def reference(x_prompt, x_sample, cache_k, cache_v, c, c_ctx, w_mod, b_mod, norm1_g, w_in,
              q_norm_g, k_norm_g, w_pool, pool_scale, w_out, norm2_g, w_up, conv_w, conv_b,
              w_down, final_norm_g):
    cond_ctx = c_ctx[None, :]
    xp = x_prompt
    xs = x_sample
    ks_out = []
    vs_out = []
    for i in range(DEPTH):
        p = {'w_mod': w_mod[i], 'b_mod': b_mod[i], 'norm1_g': norm1_g[i], 'w_in': w_in[i],
             'q_norm_g': q_norm_g[i], 'k_norm_g': k_norm_g[i], 'w_pool': w_pool[i],
             'pool_scale': pool_scale[i], 'w_out': w_out[i], 'norm2_g': norm2_g[i],
             'w_up': w_up[i], 'conv_w': conv_w[i], 'conv_b': conv_b[i], 'w_down': w_down[i]}
        xp, k_i, v_i = trunk_layer(xp, cond_ctx, None, None, p)
        ks_out.append(k_i)
        vs_out.append(v_i)
        xs, _, _ = trunk_layer(xs, c, cache_k[:, i], cache_v[:, i], p)
    y_prompt = rms_norm(xp, final_norm_g)
    y_sample = rms_norm(xs, final_norm_g)
    new_cache_k = jnp.stack(ks_out, axis=1)
    new_cache_v = jnp.stack(vs_out, axis=1)
    return (y_prompt, y_sample, new_cache_k, new_cache_v)
```

```python
import functools

import jax
import jax.numpy as jnp
from jax import lax
from jax.experimental import pallas as pl
from jax.experimental.pallas import tpu as pltpu

D_MODEL = 1024
DEPTH = 4
GRID_W = 64
HEAD_DIM = 64
N_HEADS = 8
N_KV_HEADS = 2
ATTN_WIDTH = N_HEADS * HEAD_DIM
KV_WIDTH = N_KV_HEADS * HEAD_DIM
POOL_WIDTH = D_MODEL - ATTN_WIDTH
POOL_WINDOWS = (2, 4, 8, 16)
POOL_GROUP_DIM = 128
IN_WIDTH = ATTN_WIDTH + 2 * KV_WIDTH + POOL_WIDTH
D_FF = 2816
ROPE_THETA = 10000.0
EPS = 1e-6
N_MOD = 6

LANES = 128
F32_SUBLANES = 8
BF16_SUBLANES = 16
VMEM_LIMIT = 56 * 1024 * 1024

POOL_HALO = 8
FFN_CHUNK = 256
CTX_TILE = 512
LAT_TILE = 512
LAT_Q_TILE = 128

BF16 = jnp.bfloat16
F32 = jnp.float32


def _dot(a, b):
    return jnp.dot(a, b, preferred_element_type=F32)


def _dot_nt(a, b):
    return lax.dot_general(a, b, (((1,), (1,)), ((), ())), preferred_element_type=F32)


def _rms_mod(x, g, scale1p, shift):
    ms = jnp.mean(x * x, axis=-1, keepdims=True)
    return (x * lax.rsqrt(ms + EPS) * g) * scale1p + shift


def _head_norm(t, blockdiag, g_tiled):
    ssq = _dot((t * t).astype(BF16), blockdiag)
    return t * lax.rsqrt(ssq * (1.0 / HEAD_DIM) + EPS) * g_tiled


def _rope(t, cos_t, sin_t):
    lane = lax.broadcasted_iota(jnp.int32, (t.shape[0], LANES), 1)
    first = (lane % 32) < 16
    outs = []
    for ci in range(t.shape[1] // LANES):
        tc = t[:, ci * LANES:(ci + 1) * LANES]
        partner = jnp.where(first, pltpu.roll(tc, LANES - 16, 1), pltpu.roll(tc, 16, 1))
        outs.append(tc * cos_t + partner * sin_t)
    return outs[0] if len(outs) == 1 else jnp.concatenate(outs, axis=1)


def _dup_halves(t):
    lane = lax.broadcasted_iota(jnp.int32, t.shape, 1)
    lo = lane < HEAD_DIM
    sw = pltpu.roll(t, HEAD_DIM, 1)
    return jnp.where(lo, t, sw).astype(BF16), jnp.where(lo, sw, t).astype(BF16)


def _attend_group(q_a, q_b, kdup, vdup):
    tq = q_a.shape[0]
    lane = lax.broadcasted_iota(jnp.int32, (1, LANES), 1)
    m_lo = (lane < HEAD_DIM).astype(BF16)
    m_hi = (lane >= HEAD_DIM).astype(BF16)
    lhs = jnp.concatenate([q_a * m_lo, q_a * m_hi, q_b * m_lo, q_b * m_hi], axis=0)
    s = _dot_nt(lhs, kdup)
    p = jnp.exp(s - jnp.max(s, axis=-1, keepdims=True))
    inv_l = 1.0 / jnp.sum(p, axis=-1, keepdims=True)
    o = _dot(p.astype(BF16), vdup) * inv_l
    lo = lax.broadcasted_iota(jnp.int32, (tq, LANES), 1) < HEAD_DIM
    out_a = jnp.where(lo, o[0:tq], o[tq:2 * tq])
    out_b = jnp.where(lo, o[2 * tq:3 * tq], o[3 * tq:4 * tq])
    return out_a, out_b


def _pool_mix(u, t_loc, seq_len, wpool_ref, lo_row, n_rows):
    m = u.shape[0]
    outs = []
    for gi, w in enumerate(POOL_WINDOWS):
        ug = u[:, gi * LANES:(gi + 1) * LANES]
        half = w // 2
        past, future, d = ug, ug, 1
        while d < half:
            past = past + jnp.where(t_loc >= d, pltpu.roll(past, d, 0), 0.0)
            future = future + jnp.where(t_loc + d < seq_len, pltpu.roll(future, m - d, 0), 0.0)
            d *= 2
        total = jnp.where(t_loc >= 1, pltpu.roll(past, 1, 0), 0.0) + future
        cnt = jnp.minimum(t_loc + half, seq_len) - jnp.maximum(t_loc - half, 0)
        pooled = total / cnt.astype(F32) - ug
        pooled = pooled[lo_row:lo_row + n_rows]
        outs.append(_dot(pooled.astype(BF16), wpool_ref[gi]))
    return jnp.concatenate(outs, axis=1)


def _mix_residual_norm2(x, attn, mixed, wout_ref, mod_ref, n2g_ref, x1_out, h2_out):
    gt1, sh2, sc2 = mod_ref[2:3, :], mod_ref[3:4, :], mod_ref[4:5, :]
    mix = (_dot(attn.astype(BF16), wout_ref[0:ATTN_WIDTH, :])
           + _dot(mixed.astype(BF16), wout_ref[ATTN_WIDTH:D_MODEL, :]))
    x1 = x + gt1 * mix
    x1_out[...] = x1
    h2_out[...] = _rms_mod(x1, n2g_ref[...], 1.0 + sc2, sh2).astype(BF16)


def _mod_kernel(cond_ref, w_ref, b_ref, out_ref):
    s = jax.nn.silu(cond_ref[...]).astype(BF16)
    out_ref[...] = _dot(s, w_ref[...].astype(BF16)) + b_ref[...]


def _modulation(cond8, w_mod, b_mod):
    tn = 1536
    n = N_MOD * D_MODEL
    return pl.pallas_call(
        _mod_kernel,
        out_shape=jax.ShapeDtypeStruct((DEPTH, 8, n), F32),
        grid=(DEPTH, n // tn),
        in_specs=[
            pl.BlockSpec((8, D_MODEL), lambda l, j: (0, 0)),
            pl.BlockSpec((None, D_MODEL, tn), lambda l, j: (l, 0, j)),
            pl.BlockSpec((None, 1, tn), lambda l, j: (l, 0, j)),
        ],
        out_specs=pl.BlockSpec((None, 8, tn), lambda l, j: (l, 0, j)),
        compiler_params=pltpu.CompilerParams(
            dimension_semantics=("arbitrary", "arbitrary"), vmem_limit_bytes=VMEM_LIMIT),
        name="modulation",
    )(cond8, w_mod, b_mod.reshape(DEPTH, 1, n))


def _ctx_front_kernel(x_ref, mod_ref, n1g_ref, win_ref, bd_ref, qg_ref, kg_ref, wpool_ref,
                      pscale_ref, wout_ref, n2g_ref, k_out, v_out, x1_out, h2_out, *, seq_len):
    x = x_ref[...]
    tm = x.shape[0]
    sh1, sc1 = mod_ref[0:1, :], mod_ref[1:2, :]
    h = _rms_mod(x, n1g_ref[...], 1.0 + sc1, sh1).astype(BF16)
    proj = _dot(h, win_ref[...])
    bd = bd_ref[...]
    q = (_head_norm(proj[:, 0:ATTN_WIDTH], bd, qg_ref[...]) * (HEAD_DIM ** -0.5)).astype(BF16)
    k = _head_norm(proj[:, ATTN_WIDTH:ATTN_WIDTH + KV_WIDTH], bd[0:KV_WIDTH, 0:KV_WIDTH], kg_ref[...])
    v = proj[:, ATTN_WIDTH + KV_WIDTH:ATTN_WIDTH + 2 * KV_WIDTH]
    u = proj[:, ATTN_WIDTH + 2 * KV_WIDTH:IN_WIDTH]
    k_out[...] = k
    v_out[...] = v

    seq_outs = []
    for s in range(tm // seq_len):
        r0 = s * seq_len
        kdup = _dup_halves(k[r0:r0 + seq_len])
        vdup = _dup_halves(v[r0:r0 + seq_len])
        blocks = []
        for g in range(N_KV_HEADS):
            q_a = q[r0:r0 + seq_len, (2 * g) * LANES:(2 * g + 1) * LANES]
            q_b = q[r0:r0 + seq_len, (2 * g + 1) * LANES:(2 * g + 2) * LANES]
            blocks.extend(_attend_group(q_a, q_b, kdup[g], vdup[g]))
        seq_outs.append(jnp.concatenate(blocks, axis=1))
    attn = jnp.concatenate(seq_outs, axis=0)

    t_loc = lax.broadcasted_iota(jnp.int32, (tm, LANES), 0) % seq_len
    mixed = _pool_mix(u, t_loc, seq_len, wpool_ref, 0, tm) * pscale_ref[...]
    _mix_residual_norm2(x, attn, mixed, wout_ref, mod_ref, n2g_ref, x1_out, h2_out)


def _ctx_front(layer, xp, seq_len, mods, norm1_g, w_in, bd, qg, kg, w_pool, pool_scale, w_out, norm2_g):
    n_tok = xp.shape[0]
    tm = CTX_TILE
    const2 = lambda i: (0, 0)
    tile = lambda i: (i, 0)
    return pl.pallas_call(
        functools.partial(_ctx_front_kernel, seq_len=seq_len),
        out_shape=(
            jax.ShapeDtypeStruct((n_tok, KV_WIDTH), F32),
            jax.ShapeDtypeStruct((n_tok, KV_WIDTH), F32),
            jax.ShapeDtypeStruct((n_tok, D_MODEL), F32),
            jax.ShapeDtypeStruct((n_tok, D_MODEL), BF16),
        ),
        grid=(n_tok // tm,),
        in_specs=[
            pl.BlockSpec((tm, D_MODEL), tile),
            pl.BlockSpec((None, None, N_MOD, D_MODEL), lambda i: (layer, 0, 0, 0)),
            pl.BlockSpec((1, D_MODEL), const2),
            pl.BlockSpec((None, D_MODEL, IN_WIDTH), lambda i: (layer, 0, 0)),
            pl.BlockSpec((ATTN_WIDTH, ATTN_WIDTH), const2),
            pl.BlockSpec((1, ATTN_WIDTH), const2),
            pl.BlockSpec((1, KV_WIDTH), const2),
            pl.BlockSpec((None, 4, POOL_GROUP_DIM, POOL_GROUP_DIM), lambda i: (layer, 0, 0, 0)),
            pl.BlockSpec((1, POOL_WIDTH), const2),
            pl.BlockSpec((None, D_MODEL, D_MODEL), lambda i: (layer, 0, 0)),
            pl.BlockSpec((1, D_MODEL), const2),
        ],
        out_specs=(
            pl.BlockSpec((tm, KV_WIDTH), tile),
            pl.BlockSpec((tm, KV_WIDTH), tile),
            pl.BlockSpec((tm, D_MODEL), tile),
            pl.BlockSpec((tm, D_MODEL), tile),
        ),
        compiler_params=pltpu.CompilerParams(
            dimension_semantics=("arbitrary",), vmem_limit_bytes=VMEM_LIMIT),
        name=f"ctx_front_{layer}",
    )(xp, mods, norm1_g, w_in, bd, qg, kg, w_pool, pool_scale, w_out, norm2_g)


def _lat_proj_kernel(xprev_ref, x_ref, xnext_ref, mod_ref, n1g_ref, win_ref, bd_ref, qg_ref, kg_ref,
                     cos_ref, sin_ref, wpool_ref, pscale_ref, q_out, k_out, v_out, mixed_out, *, seq_len):
    tm = x_ref.shape[0]
    halo = POOL_HALO
    xe = jnp.concatenate([xprev_ref[...], x_ref[...], xnext_ref[...]], axis=0)
    sh1, sc1 = mod_ref[0:1, :], mod_ref[1:2, :]
    h = _rms_mod(xe, n1g_ref[...], 1.0 + sc1, sh1).astype(BF16)
    proj = _dot(h, win_ref[...])
    main = proj[halo:halo + tm]
    bd = bd_ref[...]
    cos_t, sin_t = cos_ref[...], sin_ref[...]
    q = _head_norm(main[:, 0:ATTN_WIDTH], bd, qg_ref[...])
    q_out[...] = (_rope(q, cos_t, sin_t) * (HEAD_DIM ** -0.5)).astype(BF16)
    k = _head_norm(main[:, ATTN_WIDTH:ATTN_WIDTH + KV_WIDTH], bd[0:KV_WIDTH, 0:KV_WIDTH], kg_ref[...])
    k_out[...] = _rope(k, cos_t, sin_t).astype(BF16)
    v_out[...] = main[:, ATTN_WIDTH + KV_WIDTH:ATTN_WIDTH + 2 * KV_WIDTH].astype(BF16)
    u = proj[:, ATTN_WIDTH + 2 * KV_WIDTH:IN_WIDTH]
    t_loc = (lax.broadcasted_iota(jnp.int32, (tm + 2 * halo, LANES), 0)
             + (pl.program_id(1) * tm - halo))
    mixed = _pool_mix(u, t_loc, seq_len, wpool_ref, halo, tm) * pscale_ref[...]
    mixed_out[...] = mixed.astype(BF16)


def _lat_proj(layer, xs, mods, norm1_g, w_in, bd, qg, kg, cos_t, sin_t, w_pool, pool_scale):
    nb, seq_len, _ = xs.shape
    tm = LAT_TILE
    halo = POOL_HALO
    n_halo_blocks = seq_len // halo
    const2 = lambda b, i: (0, 0)
    tile = lambda b, i: (b, i, 0)
    return pl.pallas_call(
        functools.partial(_lat_proj_kernel, seq_len=seq_len),
        out_shape=(
            jax.ShapeDtypeStruct((nb, seq_len, ATTN_WIDTH), BF16),
            jax.ShapeDtypeStruct((nb, seq_len, KV_WIDTH), BF16),
            jax.ShapeDtypeStruct((nb, seq_len, KV_WIDTH), BF16),
            jax.ShapeDtypeStruct((nb, seq_len, POOL_WIDTH), BF16),
        ),
        grid=(nb, seq_len // tm),
        in_specs=[
            pl.BlockSpec((None, halo, D_MODEL),
                         lambda b, i: (b, jnp.maximum(i * (tm // halo) - 1, 0), 0)),
            pl.BlockSpec((None, tm, D_MODEL), tile),
            pl.BlockSpec((None, halo, D_MODEL),
                         lambda b, i: (b, jnp.minimum((i + 1) * (tm // halo), n_halo_blocks - 1), 0)),
            pl.BlockSpec((None, None, N_MOD, D_MODEL), lambda b, i: (layer, 1 + b, 0, 0)),
            pl.BlockSpec((1, D_MODEL), const2),
            pl.BlockSpec((None, D_MODEL, IN_WIDTH), lambda b, i: (layer, 0, 0)),
            pl.BlockSpec((ATTN_WIDTH, ATTN_WIDTH), const2),
            pl.BlockSpec((1, ATTN_WIDTH), const2),
            pl.BlockSpec((1, KV_WIDTH), const2),
            pl.BlockSpec((tm, LANES), lambda b, i: (i, 0)),
            pl.BlockSpec((tm, LANES), lambda b, i: (i, 0)),
            pl.BlockSpec((None, 4, POOL_GROUP_DIM, POOL_GROUP_DIM), lambda b, i: (layer, 0, 0, 0)),
            pl.BlockSpec((1, POOL_WIDTH), const2),
        ],
        out_specs=(
            pl.BlockSpec((None, tm, ATTN_WIDTH), tile),
            pl.BlockSpec((None, tm, KV_WIDTH), tile),
            pl.BlockSpec((None, tm, KV_WIDTH), tile),
            pl.BlockSpec((None, tm, POOL_WIDTH), tile),
        ),
        compiler_params=pltpu.CompilerParams(
            dimension_semantics=("arbitrary", "arbitrary"), vmem_limit_bytes=VMEM_LIMIT),
        name=f"lat_proj_{layer}",
    )(xs, xs, xs, mods, norm1_g, w_in, bd, qg, kg, cos_t, sin_t, w_pool, pool_scale)


def _lat_attn_kernel(q_ref, klat_ref, vlat_ref, ck_ref, cv_ref, mixed_ref, x_ref, mod_ref, wout_ref,
                     n2g_ref, x1_out, h2_out, kdup_scr, vdup_scr):
    @pl.when(pl.program_id(1) == 0)
    def _():
        k_all = jnp.concatenate([ck_ref[...], klat_ref[...].astype(F32)], axis=0)
        v_all = jnp.concatenate([cv_ref[...], vlat_ref[...].astype(F32)], axis=0)
        k0, k1 = _dup_halves(k_all)
        v0, v1 = _dup_halves(v_all)
        kdup_scr[0] = k0
        kdup_scr[1] = k1
        vdup_scr[0] = v0
        vdup_scr[1] = v1

    q = q_ref[...]
    blocks = []
    for g in range(N_KV_HEADS):
        q_a = q[:, (2 * g) * LANES:(2 * g + 1) * LANES]
        q_b = q[:, (2 * g + 1) * LANES:(2 * g + 2) * LANES]
        blocks.extend(_attend_group(q_a, q_b, kdup_scr[g], vdup_scr[g]))
    attn = jnp.concatenate(blocks, axis=1)
    _mix_residual_norm2(x_ref[...], attn, mixed_ref[...], wout_ref, mod_ref, n2g_ref, x1_out, h2_out)


def _lat_attn(layer, q, k, v, cache_k, cache_v, mixed, xs, mods, w_out, norm2_g):
    nb, seq_len, _ = xs.shape
    past = cache_k.shape[2]
    tq = LAT_Q_TILE
    n_keys = past + seq_len
    tile = lambda b, j: (b, j, 0)
    whole = lambda b, j: (b, 0, 0)
    return pl.pallas_call(
        _lat_attn_kernel,
        out_shape=(
            jax.ShapeDtypeStruct((nb, seq_len, D_MODEL), F32),
            jax.ShapeDtypeStruct((nb, seq_len, D_MODEL), BF16),
        ),
        grid=(nb, seq_len // tq),
        in_specs=[
            pl.BlockSpec((None, tq, ATTN_WIDTH), tile),
            pl.BlockSpec((None, seq_len, KV_WIDTH), whole),
            pl.BlockSpec((None, seq_len, KV_WIDTH), whole),
            pl.BlockSpec((None, None, past, KV_WIDTH), lambda b, j: (b, layer, 0, 0)),
            pl.BlockSpec((None, None, past, KV_WIDTH), lambda b, j: (b, layer, 0, 0)),
            pl.BlockSpec((None, tq, POOL_WIDTH), tile),
            pl.BlockSpec((None, tq, D_MODEL), tile),
            pl.BlockSpec((None, None, N_MOD, D_MODEL), lambda b, j: (layer, 1 + b, 0, 0)),
            pl.BlockSpec((None, D_MODEL, D_MODEL), lambda b, j: (layer, 0, 0)),
            pl.BlockSpec((1, D_MODEL), lambda b, j: (0, 0)),
        ],
        out_specs=(
            pl.BlockSpec((None, tq, D_MODEL), tile),
            pl.BlockSpec((None, tq, D_MODEL), tile),
        ),
        scratch_shapes=[
            pltpu.VMEM((N_KV_HEADS, n_keys, LANES), BF16),
            pltpu.VMEM((N_KV_HEADS, n_keys, LANES), BF16),
        ],
        compiler_params=pltpu.CompilerParams(
            dimension_semantics=("arbitrary", "arbitrary"), vmem_limit_bytes=VMEM_LIMIT),
        name=f"lat_attn_{layer}",
    )(q, k, v, cache_k, cache_v, mixed, xs, mods, w_out, norm2_g)


def _ffn_kernel(x1_ref, h2_ref, h2prev_ref, h2next_ref, mod_ref, wup_ref, cw_ref, cb_ref, wdn_ref,
                fg_ref, out_ref, *, seq_len, final_norm):
    tm = x1_ref.shape[0]
    halo = BF16_SUBLANES
    he = jnp.concatenate([h2prev_ref[...], h2_ref[...], h2next_ref[...]], axis=0)
    t_loc = (lax.broadcasted_iota(jnp.int32, (tm, FFN_CHUNK), 0) + pl.program_id(1) * tm) % seq_len
    at_start = t_loc == 0
    at_end = t_loc == seq_len - 1

    def conv(z, col0):
        cols = slice(col0, col0 + FFN_CHUNK)
        z_prev = jnp.where(at_start, 0.0, pltpu.roll(z, 1, 0)[halo:halo + tm])
        z_next = jnp.where(at_end, 0.0, pltpu.roll(z, tm + 2 * halo - 1, 0)[halo:halo + tm])
        return (z_prev * cw_ref[0:1, cols] + z[halo:halo + tm] * cw_ref[1:2, cols]
                + z_next * cw_ref[2:3, cols] + cb_ref[0:1, cols])

    acc = jnp.zeros((tm, D_MODEL), F32)
    for ci in range(D_FF // FFN_CHUNK):
        c0 = ci * FFN_CHUNK
        a = conv(_dot(he, wup_ref[:, c0:c0 + FFN_CHUNK]), c0)
        g = conv(_dot(he, wup_ref[:, D_FF + c0:D_FF + c0 + FFN_CHUNK]), D_FF + c0)
        act = (jax.nn.silu(a) * g).astype(BF16)
        acc = acc + _dot(act, wdn_ref[c0:c0 + FFN_CHUNK, :])
    x2 = x1_ref[...] + mod_ref[5:6, :] * acc
    if final_norm:
        ms = jnp.mean(x2 * x2, axis=-1, keepdims=True)
        x2 = x2 * lax.rsqrt(ms + EPS) * fg_ref[...]
    out_ref[...] = x2


def _ffn(layer, x1, h2, seq_len, mod_row0, mods, w_up, conv_w, conv_b, w_down, final_g, final_norm):
    nb, n_tok, _ = x1.shape
    tm = LAT_TILE
    halo = BF16_SUBLANES
    n_halo_blocks = n_tok // halo
    tile = lambda b, i: (b, i, 0)
    return pl.pallas_call(
        functools.partial(_ffn_kernel, seq_len=seq_len, final_norm=final_norm),
        out_shape=jax.ShapeDtypeStruct((nb, n_tok, D_MODEL), F32),
        grid=(nb, n_tok // tm),
        in_specs=[
            pl.BlockSpec((None, tm, D_MODEL), tile),
            pl.BlockSpec((None, tm, D_MODEL), tile),
            pl.BlockSpec((None, halo, D_MODEL),
                         lambda b, i: (b, jnp.maximum(i * (tm // halo) - 1, 0), 0)),
            pl.BlockSpec((None, halo, D_MODEL),
                         lambda b, i: (b, jnp.minimum((i + 1) * (tm // halo), n_halo_blocks - 1), 0)),
            pl.BlockSpec((None, None, N_MOD, D_MODEL), lambda b, i: (layer, mod_row0 + b, 0, 0)),
            pl.BlockSpec((None, D_MODEL, 2 * D_FF), lambda b, i: (layer, 0, 0),
                         pipeline_mode=pl.Buffered(1)),
            pl.BlockSpec((None, 3, 2 * D_FF), lambda b, i: (layer, 0, 0)),
            pl.BlockSpec((None, 1, 2 * D_FF), lambda b, i: (layer, 0, 0)),
            pl.BlockSpec((None, D_FF, D_MODEL), lambda b, i: (layer, 0, 0),
                         pipeline_mode=pl.Buffered(1)),
            pl.BlockSpec((1, D_MODEL), lambda b, i: (0, 0)),
        ],
        out_specs=pl.BlockSpec((None, tm, D_MODEL), tile),
        compiler_params=pltpu.CompilerParams(
            dimension_semantics=("arbitrary", "arbitrary"), vmem_limit_bytes=VMEM_LIMIT),
        name=f"ffn_{'lat' if nb > 1 else 'ctx'}_{layer}",
    )(x1, h2, h2, h2, mods, w_up, conv_w, conv_b, w_down, final_g)


def _rope_tables(n_tokens):
    t = jnp.arange(n_tokens)
    row = (t // GRID_W).astype(F32)
    col = (t % GRID_W).astype(F32)
    n_freq = HEAD_DIM // 4
    inv = ROPE_THETA ** (-jnp.arange(n_freq, dtype=F32) / n_freq)
    ang = jnp.stack([row[:, None] * inv, col[:, None] * inv], axis=1)
    cos = jnp.broadcast_to(jnp.cos(ang)[:, :, None, :], (n_tokens, 2, 2, n_freq))
    sin = jnp.sin(ang)[:, :, None, :] * jnp.array([-1.0, 1.0], F32)[None, None, :, None]
    cos = cos.reshape(n_tokens, HEAD_DIM)
    sin = sin.reshape(n_tokens, HEAD_DIM)
    return jnp.tile(cos, (1, 2)), jnp.tile(sin, (1, 2))


def kernel(x_prompt, x_sample, cache_k, cache_v, c, c_ctx, w_mod, b_mod, norm1_g, w_in, q_norm_g, k_norm_g, w_pool, pool_scale, w_out, norm2_g, w_up, conv_w, conv_b, w_down, final_norm_g):
    batch, seq, d = x_prompt.shape
    dec_batch, dec_seq, _ = x_sample.shape
    past = cache_k.shape[2]

    w_in_b = w_in.astype(BF16)
    w_pool_b = w_pool.astype(BF16)
    w_out_b = w_out.astype(BF16)
    w_up_b = w_up.astype(BF16)
    w_down_b = w_down.astype(BF16)

    cond8 = jnp.zeros((8, d), F32).at[0].set(c_ctx).at[1:1 + dec_batch].set(c)
    mods = _modulation(cond8, w_mod, b_mod).reshape(DEPTH, 8, N_MOD, d)

    head_id = jnp.arange(ATTN_WIDTH) // HEAD_DIM
    blockdiag = (head_id[:, None] == head_id[None, :]).astype(BF16)
    cos_t, sin_t = _rope_tables(dec_seq)
    ck = cache_k.reshape(dec_batch, DEPTH, past, KV_WIDTH)
    cv = cache_v.reshape(dec_batch, DEPTH, past, KV_WIDTH)
    final_g = final_norm_g.reshape(1, d)

    xp = x_prompt.reshape(batch * seq, d)
    xs = x_sample
    ks_out, vs_out = [], []
    for l in range(DEPTH):
        n1g = norm1_g[l].reshape(1, d)
        n2g = norm2_g[l].reshape(1, d)
        qg = jnp.tile(q_norm_g[l], N_HEADS).reshape(1, ATTN_WIDTH)
        kg = jnp.tile(k_norm_g[l], N_KV_HEADS).reshape(1, KV_WIDTH)
        ps = pool_scale[l].reshape(1, POOL_WIDTH)
        cb = conv_b.reshape(DEPTH, 1, 2 * D_FF)
        last = l == DEPTH - 1

        k_l, v_l, x1p, h2p = _ctx_front(l, xp, seq, mods, n1g, w_in_b, blockdiag, qg, kg,
                                        w_pool_b, ps, w_out_b, n2g)
        ks_out.append(k_l.reshape(batch, seq, N_KV_HEADS, HEAD_DIM))
        vs_out.append(v_l.reshape(batch, seq, N_KV_HEADS, HEAD_DIM))
        xp = _ffn(l, x1p[None], h2p[None], seq, 0, mods, w_up_b, conv_w, cb, w_down_b,
                  final_g, last)[0]

        q, k, v, mixed = _lat_proj(l, xs, mods, n1g, w_in_b, blockdiag, qg, kg, cos_t, sin_t,
                                   w_pool_b, ps)
        x1s, h2s = _lat_attn(l, q, k, v, ck, cv, mixed, xs, mods, w_out_b, n2g)
        xs = _ffn(l, x1s, h2s, dec_seq, 1, mods, w_up_b, conv_w, cb, w_down_b, final_g, last)

    y_prompt = xp.reshape(batch, seq, d)
    new_cache_k = jnp.stack(ks_out, axis=1)
    new_cache_v = jnp.stack(vs_out, axis=1)
    return (y_prompt, xs, new_cache_k, new_cache_v)
```

```python
import functools

import jax
import jax.numpy as jnp
from jax import lax
from jax.experimental import pallas as pl
from jax.experimental.pallas import tpu as pltpu

D_MODEL = 1024
DEPTH = 4
GRID_W = 64
HEAD_DIM = 64
N_HEADS = 8
N_KV_HEADS = 2
ATTN_WIDTH = N_HEADS * HEAD_DIM
KV_WIDTH = N_KV_HEADS * HEAD_DIM
POOL_WIDTH = D_MODEL - ATTN_WIDTH
POOL_WINDOWS = (2, 4, 8, 16)
POOL_GROUP_DIM = 128
IN_WIDTH = ATTN_WIDTH + 2 * KV_WIDTH + POOL_WIDTH
D_FF = 2816
ROPE_THETA = 10000.0
EPS = 1e-6
N_MOD = 6

LANES = 128
F32_SUBLANES = 8
BF16_SUBLANES = 16
VMEM_LIMIT = 56 * 1024 * 1024

POOL_HALO = 8
FFN_CHUNK = 256
CTX_TILE = 512
LAT_TILE = 512
LAT_Q_TILE = 128

BF16 = jnp.bfloat16
F32 = jnp.float32


def _dot(a, b):
    return jnp.dot(a, b, preferred_element_type=F32)


def _dot_nt(a, b):
    return lax.dot_general(a, b, (((1,), (1,)), ((), ())), preferred_element_type=F32)


def _rms_mod(x, g, scale1p, shift):
    ms = jnp.mean(x * x, axis=-1, keepdims=True)
    return (x * lax.rsqrt(ms + EPS) * g) * scale1p + shift


def _head_norm(t, blockdiag, g_tiled):
    ssq = _dot((t * t).astype(BF16), blockdiag)
    return t * lax.rsqrt(ssq * (1.0 / HEAD_DIM) + EPS) * g_tiled


def _rope(t, cos_t, sin_t):
    lane = lax.broadcasted_iota(jnp.int32, (t.shape[0], LANES), 1)
    first = (lane % 32) < 16
    outs = []
    for ci in range(t.shape[1] // LANES):
        tc = t[:, ci * LANES:(ci + 1) * LANES]
        partner = jnp.where(first, pltpu.roll(tc, LANES - 16, 1), pltpu.roll(tc, 16, 1))
        outs.append(tc * cos_t + partner * sin_t)
    return outs[0] if len(outs) == 1 else jnp.concatenate(outs, axis=1)


def _dup_halves(t):
    lane = lax.broadcasted_iota(jnp.int32, t.shape, 1)
    lo = lane < HEAD_DIM
    sw = pltpu.roll(t, HEAD_DIM, 1)
    return jnp.where(lo, t, sw).astype(BF16), jnp.where(lo, sw, t).astype(BF16)


def _attend_group(q_a, q_b, kdup, vdup):
    tq = q_a.shape[0]
    lane = lax.broadcasted_iota(jnp.int32, (1, LANES), 1)
    m_lo = (lane < HEAD_DIM).astype(BF16)
    m_hi = (lane >= HEAD_DIM).astype(BF16)
    lhs = jnp.concatenate([q_a * m_lo, q_a * m_hi, q_b * m_lo, q_b * m_hi], axis=0)
    s = _dot_nt(lhs, kdup)
    p = jnp.exp(s - jnp.max(s, axis=-1, keepdims=True))
    inv_l = 1.0 / jnp.sum(p, axis=-1, keepdims=True)
    o = _dot(p.astype(BF16), vdup) * inv_l
    lo = lax.broadcasted_iota(jnp.int32, (tq, LANES), 1) < HEAD_DIM
    out_a = jnp.where(lo, o[0:tq], o[tq:2 * tq])
    out_b = jnp.where(lo, o[2 * tq:3 * tq], o[3 * tq:4 * tq])
    return out_a, out_b


def _pool_mix(u, t_loc, seq_len, wpool_ref, lo_row, n_rows):
    m = u.shape[0]
    outs = []
    for gi, w in enumerate(POOL_WINDOWS):
        ug = u[:, gi * LANES:(gi + 1) * LANES]
        half = w // 2
        past, future, d = ug, ug, 1
        while d < half:
            past = past + jnp.where(t_loc >= d, pltpu.roll(past, d, 0), 0.0)
            future = future + jnp.where(t_loc + d < seq_len, pltpu.roll(future, m - d, 0), 0.0)
            d *= 2
        total = jnp.where(t_loc >= 1, pltpu.roll(past, 1, 0), 0.0) + future
        cnt = jnp.minimum(t_loc + half, seq_len) - jnp.maximum(t_loc - half, 0)
        pooled = total / cnt.astype(F32) - ug
        pooled = pooled[lo_row:lo_row + n_rows]
        outs.append(_dot(pooled.astype(BF16), wpool_ref[gi]))
    return jnp.concatenate(outs, axis=1)


def _mix_residual_norm2(x, attn, mixed, wout_ref, mod_ref, n2g_ref, x1_out, h2_out):
    gt1, sh2, sc2 = mod_ref[2:3, :], mod_ref[3:4, :], mod_ref[4:5, :]
    mix = (_dot(attn.astype(BF16), wout_ref[0:ATTN_WIDTH, :])
           + _dot(mixed.astype(BF16), wout_ref[ATTN_WIDTH:D_MODEL, :]))
    x1 = x + gt1 * mix
    x1_out[...] = x1
    h2_out[...] = _rms_mod(x1, n2g_ref[...], 1.0 + sc2, sh2).astype(BF16)


def _mod_kernel(cond_ref, w_ref, b_ref, out_ref):
    s = jax.nn.silu(cond_ref[...]).astype(BF16)
    out_ref[...] = _dot(s, w_ref[...].astype(BF16)) + b_ref[...]


def _modulation(cond8, w_mod, b_mod):
    tn = 1536
    n = N_MOD * D_MODEL
    return pl.pallas_call(
        _mod_kernel,
        out_shape=jax.ShapeDtypeStruct((DEPTH, 8, n), F32),
        grid=(DEPTH, n // tn),
        in_specs=[
            pl.BlockSpec((8, D_MODEL), lambda l, j: (0, 0)),
            pl.BlockSpec((None, D_MODEL, tn), lambda l, j: (l, 0, j)),
            pl.BlockSpec((None, 1, tn), lambda l, j: (l, 0, j)),
        ],
        out_specs=pl.BlockSpec((None, 8, tn), lambda l, j: (l, 0, j)),
        compiler_params=pltpu.CompilerParams(
            dimension_semantics=("arbitrary", "arbitrary"), vmem_limit_bytes=VMEM_LIMIT),
        name="modulation",
    )(cond8, w_mod, b_mod.reshape(DEPTH, 1, n))


def _ctx_front_kernel(x_ref, mod_ref, n1g_ref, win_ref, bd_ref, qg_ref, kg_ref, wpool_ref,
                      pscale_ref, wout_ref, n2g_ref, k_out, v_out, x1_out, h2_out, *, seq_len):
    x = x_ref[...]
    tm = x.shape[0]
    sh1, sc1 = mod_ref[0:1, :], mod_ref[1:2, :]
    h = _rms_mod(x, n1g_ref[...], 1.0 + sc1, sh1).astype(BF16)
    proj = _dot(h, win_ref[...])
    bd = bd_ref[...]
    q = (_head_norm(proj[:, 0:ATTN_WIDTH], bd, qg_ref[...]) * (HEAD_DIM ** -0.5)).astype(BF16)
    k = _head_norm(proj[:, ATTN_WIDTH:ATTN_WIDTH + KV_WIDTH], bd[0:KV_WIDTH, 0:KV_WIDTH], kg_ref[...])
    v = proj[:, ATTN_WIDTH + KV_WIDTH:ATTN_WIDTH + 2 * KV_WIDTH]
    u = proj[:, ATTN_WIDTH + 2 * KV_WIDTH:IN_WIDTH]
    k_out[...] = k
    v_out[...] = v

    seq_outs = []
    for s in range(tm // seq_len):
        r0 = s * seq_len
        kdup = _dup_halves(k[r0:r0 + seq_len])
        vdup = _dup_halves(v[r0:r0 + seq_len])
        blocks = []
        for g in range(N_KV_HEADS):
            q_a = q[r0:r0 + seq_len, (2 * g) * LANES:(2 * g + 1) * LANES]
            q_b = q[r0:r0 + seq_len, (2 * g + 1) * LANES:(2 * g + 2) * LANES]
            blocks.extend(_attend_group(q_a, q_b, kdup[g], vdup[g]))
        seq_outs.append(jnp.concatenate(blocks, axis=1))
    attn = jnp.concatenate(seq_outs, axis=0)

    t_loc = lax.broadcasted_iota(jnp.int32, (tm, LANES), 0) % seq_len
    mixed = _pool_mix(u, t_loc, seq_len, wpool_ref, 0, tm) * pscale_ref[...]
    _mix_residual_norm2(x, attn, mixed, wout_ref, mod_ref, n2g_ref, x1_out, h2_out)


def _ctx_front(layer, xp, seq_len, mods, norm1_g, w_in, bd, qg, kg, w_pool, pool_scale, w_out, norm2_g):
    n_tok = xp.shape[0]
    tm = CTX_TILE
    const2 = lambda i: (0, 0)
    tile = lambda i: (i, 0)
    return pl.pallas_call(
        functools.partial(_ctx_front_kernel, seq_len=seq_len),
        out_shape=(
            jax.ShapeDtypeStruct((n_tok, KV_WIDTH), F32),
            jax.ShapeDtypeStruct((n_tok, KV_WIDTH), F32),
            jax.ShapeDtypeStruct((n_tok, D_MODEL), F32),
            jax.ShapeDtypeStruct((n_tok, D_MODEL), BF16),
        ),
        grid=(n_tok // tm,),
        in_specs=[
            pl.BlockSpec((tm, D_MODEL), tile),
            pl.BlockSpec((None, None, N_MOD, D_MODEL), lambda i: (layer, 0, 0, 0)),
            pl.BlockSpec((1, D_MODEL), const2),
            pl.BlockSpec((None, D_MODEL, IN_WIDTH), lambda i: (layer, 0, 0)),
            pl.BlockSpec((ATTN_WIDTH, ATTN_WIDTH), const2),
            pl.BlockSpec((1, ATTN_WIDTH), const2),
            pl.BlockSpec((1, KV_WIDTH), const2),
            pl.BlockSpec((None, 4, POOL_GROUP_DIM, POOL_GROUP_DIM), lambda i: (layer, 0, 0, 0)),
            pl.BlockSpec((1, POOL_WIDTH), const2),
            pl.BlockSpec((None, D_MODEL, D_MODEL), lambda i: (layer, 0, 0)),
            pl.BlockSpec((1, D_MODEL), const2),
        ],
        out_specs=(
            pl.BlockSpec((tm, KV_WIDTH), tile),
            pl.BlockSpec((tm, KV_WIDTH), tile),
            pl.BlockSpec((tm, D_MODEL), tile),
            pl.BlockSpec((tm, D_MODEL), tile),
        ),
        compiler_params=pltpu.CompilerParams(
            dimension_semantics=("arbitrary",), vmem_limit_bytes=VMEM_LIMIT),
        name=f"ctx_front_{layer}",
    )(xp, mods, norm1_g, w_in, bd, qg, kg, w_pool, pool_scale, w_out, norm2_g)


def _lat_proj_kernel(xprev_ref, x_ref, xnext_ref, mod_ref, n1g_ref, win_ref, bd_ref, qg_ref, kg_ref,
                     cos_ref, sin_ref, wpool_ref, pscale_ref, q_out, k_out, v_out, mixed_out, *, seq_len):
    tm = x_ref.shape[0]
    halo = POOL_HALO
    xe = jnp.concatenate([xprev_ref[...], x_ref[...], xnext_ref[...]], axis=0)
    sh1, sc1 = mod_ref[0:1, :], mod_ref[1:2, :]
    h = _rms_mod(xe, n1g_ref[...], 1.0 + sc1, sh1).astype(BF16)
    proj = _dot(h, win_ref[...])
    main = proj[halo:halo + tm]
    bd = bd_ref[...]
    cos_t, sin_t = cos_ref[...], sin_ref[...]
    q = _head_norm(main[:, 0:ATTN_WIDTH], bd, qg_ref[...])
    q_out[...] = (_rope(q, cos_t, sin_t) * (HEAD_DIM ** -0.5)).astype(BF16)
    k = _head_norm(main[:, ATTN_WIDTH:ATTN_WIDTH + KV_WIDTH], bd[0:KV_WIDTH, 0:KV_WIDTH], kg_ref[...])
    k_out[...] = _rope(k, cos_t, sin_t).astype(BF16)
    v_out[...] = main[:, ATTN_WIDTH + KV_WIDTH:ATTN_WIDTH + 2 * KV_WIDTH].astype(BF16)
    u = proj[:, ATTN_WIDTH + 2 * KV_WIDTH:IN_WIDTH]
    t_loc = (lax.broadcasted_iota(jnp.int32, (tm + 2 * halo, LANES), 0)
             + (pl.program_id(1) * tm - halo))
    mixed = _pool_mix(u, t_loc, seq_len, wpool_ref, halo, tm) * pscale_ref[...]
    mixed_out[...] = mixed.astype(BF16)


def _lat_proj(layer, xs, mods, norm1_g, w_in, bd, qg, kg, cos_t, sin_t, w_pool, pool_scale):
    nb, seq_len, _ = xs.shape
    tm = LAT_TILE
    halo = POOL_HALO
    n_halo_blocks = seq_len // halo
    const2 = lambda b, i: (0, 0)
    tile = lambda b, i: (b, i, 0)
    return pl.pallas_call(
        functools.partial(_lat_proj_kernel, seq_len=seq_len),
        out_shape=(
            jax.ShapeDtypeStruct((nb, seq_len, ATTN_WIDTH), BF16),
            jax.ShapeDtypeStruct((nb, seq_len, KV_WIDTH), BF16),
            jax.ShapeDtypeStruct((nb, seq_len, KV_WIDTH), BF16),
            jax.ShapeDtypeStruct((nb, seq_len, POOL_WIDTH), BF16),
        ),
        grid=(nb, seq_len // tm),
        in_specs=[
            pl.BlockSpec((None, halo, D_MODEL),
                         lambda b, i: (b, jnp.maximum(i * (tm // halo) - 1, 0), 0)),
            pl.BlockSpec((None, tm, D_MODEL), tile),
            pl.BlockSpec((None, halo, D_MODEL),
                         lambda b, i: (b, jnp.minimum((i + 1) * (tm // halo), n_halo_blocks - 1), 0)),
            pl.BlockSpec((None, None, N_MOD, D_MODEL), lambda b, i: (layer, 1 + b, 0, 0)),
            pl.BlockSpec((1, D_MODEL), const2),
            pl.BlockSpec((None, D_MODEL, IN_WIDTH), lambda b, i: (layer, 0, 0)),
            pl.BlockSpec((ATTN_WIDTH, ATTN_WIDTH), const2),
            pl.BlockSpec((1, ATTN_WIDTH), const2),
            pl.BlockSpec((1, KV_WIDTH), const2),
            pl.BlockSpec((tm, LANES), lambda b, i: (i, 0)),
            pl.BlockSpec((tm, LANES), lambda b, i: (i, 0)),
            pl.BlockSpec((None, 4, POOL_GROUP_DIM, POOL_GROUP_DIM), lambda b, i: (layer, 0, 0, 0)),
            pl.BlockSpec((1, POOL_WIDTH), const2),
        ],
        out_specs=(
            pl.BlockSpec((None, tm, ATTN_WIDTH), tile),
            pl.BlockSpec((None, tm, KV_WIDTH), tile),
            pl.BlockSpec((None, tm, KV_WIDTH), tile),
            pl.BlockSpec((None, tm, POOL_WIDTH), tile),
        ),
        compiler_params=pltpu.CompilerParams(
            dimension_semantics=("arbitrary", "arbitrary"), vmem_limit_bytes=VMEM_LIMIT),
        name=f"lat_proj_{layer}",
    )(xs, xs, xs, mods, norm1_g, w_in, bd, qg, kg, cos_t, sin_t, w_pool, pool_scale)


def _lat_attn_kernel(q_ref, klat_ref, vlat_ref, ck_ref, cv_ref, mixed_ref, x_ref, mod_ref, wout_ref,
                     n2g_ref, x1_out, h2_out, kdup_scr, vdup_scr):
    @pl.when(pl.program_id(1) == 0)
    def _():
        k_all = jnp.concatenate([ck_ref[...], klat_ref[...].astype(F32)], axis=0)
        v_all = jnp.concatenate([cv_ref[...], vlat_ref[...].astype(F32)], axis=0)
        k0, k1 = _dup_halves(k_all)
        v0, v1 = _dup_halves(v_all)
        kdup_scr[0] = k0
        kdup_scr[1] = k1
        vdup_scr[0] = v0
        vdup_scr[1] = v1

    q = q_ref[...]
    blocks = []
    for g in range(N_KV_HEADS):
        q_a = q[:, (2 * g) * LANES:(2 * g + 1) * LANES]
        q_b = q[:, (2 * g + 1) * LANES:(2 * g + 2) * LANES]
        blocks.extend(_attend_group(q_a, q_b, kdup_scr[g], vdup_scr[g]))
    attn = jnp.concatenate(blocks, axis=1)
    _mix_residual_norm2(x_ref[...], attn, mixed_ref[...], wout_ref, mod_ref, n2g_ref, x1_out, h2_out)


def _lat_attn(layer, q, k, v, cache_k, cache_v, mixed, xs, mods, w_out, norm2_g):
    nb, seq_len, _ = xs.shape
    past = cache_k.shape[2]
    tq = LAT_Q_TILE
    n_keys = past + seq_len
    tile = lambda b, j: (b, j, 0)
    whole = lambda b, j: (b, 0, 0)
    return pl.pallas_call(
        _lat_attn_kernel,
        out_shape=(
            jax.ShapeDtypeStruct((nb, seq_len, D_MODEL), F32),
            jax.ShapeDtypeStruct((nb, seq_len, D_MODEL), BF16),
        ),
        grid=(nb, seq_len // tq),
        in_specs=[
            pl.BlockSpec((None, tq, ATTN_WIDTH), tile),
            pl.BlockSpec((None, seq_len, KV_WIDTH), whole),
            pl.BlockSpec((None, seq_len, KV_WIDTH), whole),
            pl.BlockSpec((None, None, past, KV_WIDTH), lambda b, j: (b, layer, 0, 0)),
            pl.BlockSpec((None, None, past, KV_WIDTH), lambda b, j: (b, layer, 0, 0)),
            pl.BlockSpec((None, tq, POOL_WIDTH), tile),
            pl.BlockSpec((None, tq, D_MODEL), tile),
            pl.BlockSpec((None, None, N_MOD, D_MODEL), lambda b, j: (layer, 1 + b, 0, 0)),
            pl.BlockSpec((None, D_MODEL, D_MODEL), lambda b, j: (layer, 0, 0)),
            pl.BlockSpec((1, D_MODEL), lambda b, j: (0, 0)),
        ],
        out_specs=(
            pl.BlockSpec((None, tq, D_MODEL), tile),
            pl.BlockSpec((None, tq, D_MODEL), tile),
        ),
        scratch_shapes=[
            pltpu.VMEM((N_KV_HEADS, n_keys, LANES), BF16),
            pltpu.VMEM((N_KV_HEADS, n_keys, LANES), BF16),
        ],
        compiler_params=pltpu.CompilerParams(
            dimension_semantics=("arbitrary", "arbitrary"), vmem_limit_bytes=VMEM_LIMIT),
        name=f"lat_attn_{layer}",
    )(q, k, v, cache_k, cache_v, mixed, xs, mods, w_out, norm2_g)


def _ffn_kernel(*refs, seq_len, final_norm, has_halo):
    if has_halo:
        (x1_ref, h2_ref, h2prev_ref, h2next_ref, mod_ref, wup_ref, cw_ref, cb_ref, wdn_ref, fg_ref,
         out_ref, z_scr, act_scr) = refs
    else:
        (x1_ref, h2_ref, mod_ref, wup_ref, cw_ref, cb_ref, wdn_ref, fg_ref,
         out_ref, z_scr, act_scr) = refs
    tm = x1_ref.shape[0]
    gap = F32_SUBLANES
    if has_halo:
        halo = BF16_SUBLANES
        he = jnp.concatenate([h2prev_ref[...], h2_ref[...], h2next_ref[...]], axis=0)
        segments = [(halo, tm)]
        tile_start = pl.program_id(1) * tm
        zero_before = tile_start % seq_len == 0
        zero_after = (tile_start + tm) % seq_len == 0
    else:
        he = h2_ref[...]
        segments = [(gap + s * (seq_len + gap), seq_len) for s in range(tm // seq_len)]
        for slab in range(z_scr.shape[0]):
            for s in range(tm // seq_len + 1):
                r = s * (seq_len + gap)
                z_scr[slab, r:r + gap, :] = jnp.zeros((gap, LANES), F32)

    def conv(slab, z, col0):
        cols = slice(col0, col0 + LANES)
        if has_halo:
            z_scr[slab, 0:z.shape[0], :] = z
            row0 = segments[0][0]

            @pl.when(zero_before)
            def _():
                z_scr[slab, row0 - gap:row0, :] = jnp.zeros((gap, LANES), F32)

            @pl.when(zero_after)
            def _():
                z_scr[slab, row0 + tm:row0 + tm + gap, :] = jnp.zeros((gap, LANES), F32)
        else:
            for s, (row0, n) in enumerate(segments):
                z_scr[slab, row0:row0 + n, :] = z[s * n:(s + 1) * n]
        outs = []
        for row0, n in segments:
            outs.append(z_scr[slab, pl.ds(row0 - 1, n, stride=1), :] * cw_ref[0:1, cols]
                        + z_scr[slab, row0:row0 + n, :] * cw_ref[1:2, cols]
                        + z_scr[slab, pl.ds(row0 + 1, n, stride=1), :] * cw_ref[2:3, cols]
                        + cb_ref[0:1, cols])
        return outs[0] if len(outs) == 1 else jnp.concatenate(outs, axis=0)

    n_sub = FFN_CHUNK // LANES
    for ci in range(D_FF // FFN_CHUNK):
        c0 = ci * FFN_CHUNK
        slab0 = 2 * n_sub * (ci % 2)
        za = _dot(he, wup_ref[:, c0:c0 + FFN_CHUNK])
        zg = _dot(he, wup_ref[:, D_FF + c0:D_FF + c0 + FFN_CHUNK])
        for j in range(n_sub):
            lanes = slice(j * LANES, (j + 1) * LANES)
            a = conv(slab0 + 2 * j, za[:, lanes], c0 + j * LANES)
            g = conv(slab0 + 2 * j + 1, zg[:, lanes], D_FF + c0 + j * LANES)
            act_scr[:, c0 + j * LANES:c0 + (j + 1) * LANES] = (jax.nn.silu(a) * g).astype(BF16)
    x2 = x1_ref[...] + mod_ref[5:6, :] * _dot(act_scr[...], wdn_ref[...])
    if final_norm:
        ms = jnp.mean(x2 * x2, axis=-1, keepdims=True)
        x2 = x2 * lax.rsqrt(ms + EPS) * fg_ref[...]
    out_ref[...] = x2


def _ffn(layer, x1, h2, seq_len, mod_row0, mods, w_up, conv_w, conv_b, w_down, final_g, final_norm):
    nb, n_tok, _ = x1.shape
    tm = LAT_TILE
    has_halo = seq_len > tm
    assert seq_len % tm == 0 or tm % seq_len == 0
    halo = BF16_SUBLANES
    gap = F32_SUBLANES
    n_halo_blocks = n_tok // halo
    tile = lambda b, i: (b, i, 0)
    halo_specs = [
        pl.BlockSpec((None, halo, D_MODEL),
                     lambda b, i: (b, jnp.maximum(i * (tm // halo) - 1, 0), 0)),
        pl.BlockSpec((None, halo, D_MODEL),
                     lambda b, i: (b, jnp.minimum((i + 1) * (tm // halo), n_halo_blocks - 1), 0)),
    ] if has_halo else []
    halo_args = [h2, h2] if has_halo else []
    z_rows = tm + 2 * halo if has_halo else gap + (tm // seq_len) * (seq_len + gap)
    return pl.pallas_call(
        functools.partial(_ffn_kernel, seq_len=seq_len, final_norm=final_norm, has_halo=has_halo),
        out_shape=jax.ShapeDtypeStruct((nb, n_tok, D_MODEL), F32),
        grid=(nb, n_tok // tm),
        in_specs=[
            pl.BlockSpec((None, tm, D_MODEL), tile),
            pl.BlockSpec((None, tm, D_MODEL), tile),
            *halo_specs,
            pl.BlockSpec((None, None, N_MOD, D_MODEL), lambda b, i: (layer, mod_row0 + b, 0, 0)),
            pl.BlockSpec((None, D_MODEL, 2 * D_FF), lambda b, i: (layer, 0, 0),
                         pipeline_mode=pl.Buffered(1)),
            pl.BlockSpec((None, 3, 2 * D_FF), lambda b, i: (layer, 0, 0)),
            pl.BlockSpec((None, 1, 2 * D_FF), lambda b, i: (layer, 0, 0)),
            pl.BlockSpec((None, D_FF, D_MODEL), lambda b, i: (layer, 0, 0),
                         pipeline_mode=pl.Buffered(1)),
            pl.BlockSpec((1, D_MODEL), lambda b, i: (0, 0)),
        ],
        out_specs=pl.BlockSpec((None, tm, D_MODEL), tile),
        scratch_shapes=[
            pltpu.VMEM((4 * (FFN_CHUNK // LANES), z_rows, LANES), F32),
            pltpu.VMEM((tm, D_FF), BF16),
        ],
        compiler_params=pltpu.CompilerParams(
            dimension_semantics=("arbitrary", "arbitrary"), vmem_limit_bytes=VMEM_LIMIT),
        name=f"ffn_{'lat' if nb > 1 else 'ctx'}_{layer}",
    )(x1, h2, *halo_args, mods, w_up, conv_w, conv_b, w_down, final_g)


def _rope_tables(n_tokens):
    t = jnp.arange(n_tokens)
    row = (t // GRID_W).astype(F32)
    col = (t % GRID_W).astype(F32)
    n_freq = HEAD_DIM // 4
    inv = ROPE_THETA ** (-jnp.arange(n_freq, dtype=F32) / n_freq)
    ang = jnp.stack([row[:, None] * inv, col[:, None] * inv], axis=1)
    cos = jnp.broadcast_to(jnp.cos(ang)[:, :, None, :], (n_tokens, 2, 2, n_freq))
    sin = jnp.sin(ang)[:, :, None, :] * jnp.array([-1.0, 1.0], F32)[None, None, :, None]
    cos = cos.reshape(n_tokens, HEAD_DIM)
    sin = sin.reshape(n_tokens, HEAD_DIM)
    return jnp.tile(cos, (1, 2)), jnp.tile(sin, (1, 2))


def kernel(x_prompt, x_sample, cache_k, cache_v, c, c_ctx, w_mod, b_mod, norm1_g, w_in, q_norm_g, k_norm_g, w_pool, pool_scale, w_out, norm2_g, w_up, conv_w, conv_b, w_down, final_norm_g):
    batch, seq, d = x_prompt.shape
    dec_batch, dec_seq, _ = x_sample.shape
    past = cache_k.shape[2]

    w_in_b = w_in.astype(BF16)
    w_pool_b = w_pool.astype(BF16)
    w_out_b = w_out.astype(BF16)
    w_up_b = w_up.astype(BF16)
    w_down_b = w_down.astype(BF16)

    cond8 = jnp.zeros((8, d), F32).at[0].set(c_ctx).at[1:1 + dec_batch].set(c)
    mods = _modulation(cond8, w_mod, b_mod).reshape(DEPTH, 8, N_MOD, d)

    head_id = jnp.arange(ATTN_WIDTH) // HEAD_DIM
    blockdiag = (head_id[:, None] == head_id[None, :]).astype(BF16)
    cos_t, sin_t = _rope_tables(dec_seq)
    ck = cache_k.reshape(dec_batch, DEPTH, past, KV_WIDTH)
    cv = cache_v.reshape(dec_batch, DEPTH, past, KV_WIDTH)
    final_g = final_norm_g.reshape(1, d)

    xp = x_prompt.reshape(batch * seq, d)
    xs = x_sample
    ks_out, vs_out = [], []
    for l in range(DEPTH):
        n1g = norm1_g[l].reshape(1, d)
        n2g = norm2_g[l].reshape(1, d)
        qg = jnp.tile(q_norm_g[l], N_HEADS).reshape(1, ATTN_WIDTH)
        kg = jnp.tile(k_norm_g[l], N_KV_HEADS).reshape(1, KV_WIDTH)
        ps = pool_scale[l].reshape(1, POOL_WIDTH)
        cb = conv_b.reshape(DEPTH, 1, 2 * D_FF)
        last = l == DEPTH - 1

        k_l, v_l, x1p, h2p = _ctx_front(l, xp, seq, mods, n1g, w_in_b, blockdiag, qg, kg,
                                        w_pool_b, ps, w_out_b, n2g)
        ks_out.append(k_l.reshape(batch, seq, N_KV_HEADS, HEAD_DIM))
        vs_out.append(v_l.reshape(batch, seq, N_KV_HEADS, HEAD_DIM))
        xp = _ffn(l, x1p[None], h2p[None], seq, 0, mods, w_up_b, conv_w, cb, w_down_b,
                  final_g, last)[0]

        q, k, v, mixed = _lat_proj(l, xs, mods, n1g, w_in_b, blockdiag, qg, kg, cos_t, sin_t,
                                   w_pool_b, ps)
        x1s, h2s = _lat_attn(l, q, k, v, ck, cv, mixed, xs, mods, w_out_b, n2g)
        xs = _ffn(l, x1s, h2s, dec_seq, 1, mods, w_up_b, conv_w, cb, w_down_b, final_g, last)

    y_prompt = xp.reshape(batch, seq, d)
    new_cache_k = jnp.stack(ks_out, axis=1)
    new_cache_v = jnp.stack(vs_out, axis=1)
    return (y_prompt, xs, new_cache_k, new_cache_v)
```

```python
import functools

import jax
import jax.numpy as jnp
from jax import lax
from jax.experimental import pallas as pl
from jax.experimental.pallas import tpu as pltpu

D_MODEL = 1024
DEPTH = 4
GRID_W = 64
HEAD_DIM = 64
N_HEADS = 8
N_KV_HEADS = 2
ATTN_WIDTH = N_HEADS * HEAD_DIM
KV_WIDTH = N_KV_HEADS * HEAD_DIM
POOL_WIDTH = D_MODEL - ATTN_WIDTH
POOL_WINDOWS = (2, 4, 8, 16)
POOL_GROUP_DIM = 128
IN_WIDTH = ATTN_WIDTH + 2 * KV_WIDTH + POOL_WIDTH
D_FF = 2816
ROPE_THETA = 10000.0
EPS = 1e-6
N_MOD = 6

LANES = 128
F32_SUBLANES = 8
BF16_SUBLANES = 16
VMEM_LIMIT = 56 * 1024 * 1024

POOL_HALO = 8
FFN_CHUNK = 256
CTX_TILE = 512
LAT_TILE = 512
LAT_Q_TILE = 128
FFN_TILE = 512

BF16 = jnp.bfloat16
F32 = jnp.float32


def _dot(a, b):
    return jnp.dot(a, b, preferred_element_type=F32)


def _dot_nt(a, b):
    return lax.dot_general(a, b, (((1,), (1,)), ((), ())), preferred_element_type=F32)


def _rms_mod(x, g, scale1p, shift):
    ms = jnp.mean(x * x, axis=-1, keepdims=True)
    return (x * lax.rsqrt(ms + EPS) * g) * scale1p + shift


def _head_norm(t, blockdiag, g_tiled):
    ssq = _dot((t * t).astype(BF16), blockdiag)
    return t * lax.rsqrt(ssq * (1.0 / HEAD_DIM) + EPS) * g_tiled


def _rope(t, cos_t, sin_t):
    lane = lax.broadcasted_iota(jnp.int32, (t.shape[0], LANES), 1)
    first = (lane % 32) < 16
    outs = []
    for ci in range(t.shape[1] // LANES):
        tc = t[:, ci * LANES:(ci + 1) * LANES]
        partner = jnp.where(first, pltpu.roll(tc, LANES - 16, 1), pltpu.roll(tc, 16, 1))
        outs.append(tc * cos_t + partner * sin_t)
    return outs[0] if len(outs) == 1 else jnp.concatenate(outs, axis=1)


def _dup_halves(t):
    lane = lax.broadcasted_iota(jnp.int32, t.shape, 1)
    lo = lane < HEAD_DIM
    sw = pltpu.roll(t, HEAD_DIM, 1)
    return jnp.where(lo, t, sw).astype(BF16), jnp.where(lo, sw, t).astype(BF16)


def _attend_group(q_a, q_b, kdup, vdup):
    tq = q_a.shape[0]
    lane = lax.broadcasted_iota(jnp.int32, (1, LANES), 1)
    m_lo = (lane < HEAD_DIM).astype(BF16)
    m_hi = (lane >= HEAD_DIM).astype(BF16)
    lhs = jnp.concatenate([q_a * m_lo, q_a * m_hi, q_b * m_lo, q_b * m_hi], axis=0)
    s = _dot_nt(lhs, kdup)
    p = jnp.exp(s - jnp.max(s, axis=-1, keepdims=True))
    inv_l = 1.0 / jnp.sum(p, axis=-1, keepdims=True)
    o = _dot(p.astype(BF16), vdup) * inv_l
    lo = lax.broadcasted_iota(jnp.int32, (tq, LANES), 1) < HEAD_DIM
    out_a = jnp.where(lo, o[0:tq], o[tq:2 * tq])
    out_b = jnp.where(lo, o[2 * tq:3 * tq], o[3 * tq:4 * tq])
    return out_a, out_b


def _pool_mix(u, t_loc, seq_len, wpool_ref, lo_row, n_rows):
    m = u.shape[0]
    outs = []
    for gi, w in enumerate(POOL_WINDOWS):
        ug = u[:, gi * LANES:(gi + 1) * LANES]
        half = w // 2
        past, future, d = ug, ug, 1
        while d < half:
            past = past + jnp.where(t_loc >= d, pltpu.roll(past, d, 0), 0.0)
            future = future + jnp.where(t_loc + d < seq_len, pltpu.roll(future, m - d, 0), 0.0)
            d *= 2
        total = jnp.where(t_loc >= 1, pltpu.roll(past, 1, 0), 0.0) + future
        cnt = jnp.minimum(t_loc + half, seq_len) - jnp.maximum(t_loc - half, 0)
        pooled = total / cnt.astype(F32) - ug
        pooled = pooled[lo_row:lo_row + n_rows]
        outs.append(_dot(pooled.astype(BF16), wpool_ref[gi]))
    return jnp.concatenate(outs, axis=1)


def _mix_residual_norm2(x, attn, mixed, wout_ref, mod_ref, n2g_ref, x1_out, h2_out):
    gt1, sh2, sc2 = mod_ref[2:3, :], mod_ref[3:4, :], mod_ref[4:5, :]
    mix = (_dot(attn.astype(BF16), wout_ref[0:ATTN_WIDTH, :])
           + _dot(mixed.astype(BF16), wout_ref[ATTN_WIDTH:D_MODEL, :]))
    x1 = x + gt1 * mix
    x1_out[...] = x1
    h2_out[...] = _rms_mod(x1, n2g_ref[...], 1.0 + sc2, sh2).astype(BF16)


def _mod_kernel(cond_ref, w_ref, b_ref, out_ref):
    s = jax.nn.silu(cond_ref[...]).astype(BF16)
    out_ref[...] = _dot(s, w_ref[...].astype(BF16)) + b_ref[...]


def _modulation(cond8, w_mod, b_mod):
    tn = 1536
    n = N_MOD * D_MODEL
    return pl.pallas_call(
        _mod_kernel,
        out_shape=jax.ShapeDtypeStruct((DEPTH, 8, n), F32),
        grid=(DEPTH, n // tn),
        in_specs=[
            pl.BlockSpec((8, D_MODEL), lambda l, j: (0, 0)),
            pl.BlockSpec((None, D_MODEL, tn), lambda l, j: (l, 0, j)),
            pl.BlockSpec((None, 1, tn), lambda l, j: (l, 0, j)),
        ],
        out_specs=pl.BlockSpec((None, 8, tn), lambda l, j: (l, 0, j)),
        compiler_params=pltpu.CompilerParams(
            dimension_semantics=("arbitrary", "arbitrary"), vmem_limit_bytes=VMEM_LIMIT),
        name="modulation",
    )(cond8, w_mod, b_mod.reshape(DEPTH, 1, n))


def _ctx_front_kernel(x_ref, mod_ref, n1g_ref, win_ref, bd_ref, qg_ref, kg_ref, wpool_ref,
                      pscale_ref, wout_ref, n2g_ref, k_out, v_out, x1_out, h2_out, *, seq_len):
    x = x_ref[...]
    tm = x.shape[0]
    sh1, sc1 = mod_ref[0:1, :], mod_ref[1:2, :]
    h = _rms_mod(x, n1g_ref[...], 1.0 + sc1, sh1).astype(BF16)
    proj = _dot(h, win_ref[...])
    bd = bd_ref[...]
    q = (_head_norm(proj[:, 0:ATTN_WIDTH], bd, qg_ref[...]) * (HEAD_DIM ** -0.5)).astype(BF16)
    k = _head_norm(proj[:, ATTN_WIDTH:ATTN_WIDTH + KV_WIDTH], bd[0:KV_WIDTH, 0:KV_WIDTH], kg_ref[...])
    v = proj[:, ATTN_WIDTH + KV_WIDTH:ATTN_WIDTH + 2 * KV_WIDTH]
    u = proj[:, ATTN_WIDTH + 2 * KV_WIDTH:IN_WIDTH]
    k_out[...] = k
    v_out[...] = v

    seq_outs = []
    for s in range(tm // seq_len):
        r0 = s * seq_len
        kdup = _dup_halves(k[r0:r0 + seq_len])
        vdup = _dup_halves(v[r0:r0 + seq_len])
        blocks = []
        for g in range(N_KV_HEADS):
            q_a = q[r0:r0 + seq_len, (2 * g) * LANES:(2 * g + 1) * LANES]
            q_b = q[r0:r0 + seq_len, (2 * g + 1) * LANES:(2 * g + 2) * LANES]
            blocks.extend(_attend_group(q_a, q_b, kdup[g], vdup[g]))
        seq_outs.append(jnp.concatenate(blocks, axis=1))
    attn = jnp.concatenate(seq_outs, axis=0)

    t_loc = lax.broadcasted_iota(jnp.int32, (tm, LANES), 0) % seq_len
    mixed = _pool_mix(u, t_loc, seq_len, wpool_ref, 0, tm) * pscale_ref[...]
    _mix_residual_norm2(x, attn, mixed, wout_ref, mod_ref, n2g_ref, x1_out, h2_out)


def _ctx_front(layer, xp, seq_len, mods, norm1_g, w_in, bd, qg, kg, w_pool, pool_scale, w_out, norm2_g):
    n_tok = xp.shape[0]
    tm = CTX_TILE
    const2 = lambda i: (0, 0)
    tile = lambda i: (i, 0)
    return pl.pallas_call(
        functools.partial(_ctx_front_kernel, seq_len=seq_len),
        out_shape=(
            jax.ShapeDtypeStruct((n_tok, KV_WIDTH), F32),
            jax.ShapeDtypeStruct((n_tok, KV_WIDTH), F32),
            jax.ShapeDtypeStruct((n_tok, D_MODEL), F32),
            jax.ShapeDtypeStruct((n_tok, D_MODEL), BF16),
        ),
        grid=(n_tok // tm,),
        in_specs=[
            pl.BlockSpec((tm, D_MODEL), tile),
            pl.BlockSpec((None, None, N_MOD, D_MODEL), lambda i: (layer, 0, 0, 0)),
            pl.BlockSpec((1, D_MODEL), const2),
            pl.BlockSpec((None, D_MODEL, IN_WIDTH), lambda i: (layer, 0, 0)),
            pl.BlockSpec((ATTN_WIDTH, ATTN_WIDTH), const2),
            pl.BlockSpec((1, ATTN_WIDTH), const2),
            pl.BlockSpec((1, KV_WIDTH), const2),
            pl.BlockSpec((None, 4, POOL_GROUP_DIM, POOL_GROUP_DIM), lambda i: (layer, 0, 0, 0)),
            pl.BlockSpec((1, POOL_WIDTH), const2),
            pl.BlockSpec((None, D_MODEL, D_MODEL), lambda i: (layer, 0, 0)),
            pl.BlockSpec((1, D_MODEL), const2),
        ],
        out_specs=(
            pl.BlockSpec((tm, KV_WIDTH), tile),
            pl.BlockSpec((tm, KV_WIDTH), tile),
            pl.BlockSpec((tm, D_MODEL), tile),
            pl.BlockSpec((tm, D_MODEL), tile),
        ),
        compiler_params=pltpu.CompilerParams(
            dimension_semantics=("arbitrary",), vmem_limit_bytes=VMEM_LIMIT),
        name=f"ctx_front_{layer}",
    )(xp, mods, norm1_g, w_in, bd, qg, kg, w_pool, pool_scale, w_out, norm2_g)


def _lat_proj_kernel(xprev_ref, x_ref, xnext_ref, mod_ref, n1g_ref, win_ref, bd_ref, qg_ref, kg_ref,
                     cos_ref, sin_ref, wpool_ref, pscale_ref, q_out, k_out, v_out, mixed_out, *, seq_len):
    tm = x_ref.shape[0]
    halo = POOL_HALO
    xe = jnp.concatenate([xprev_ref[...], x_ref[...], xnext_ref[...]], axis=0)
    sh1, sc1 = mod_ref[0:1, :], mod_ref[1:2, :]
    h = _rms_mod(xe, n1g_ref[...], 1.0 + sc1, sh1).astype(BF16)
    proj = _dot(h, win_ref[...])
    main = proj[halo:halo + tm]
    bd = bd_ref[...]
    cos_t, sin_t = cos_ref[...], sin_ref[...]
    q = _head_norm(main[:, 0:ATTN_WIDTH], bd, qg_ref[...])
    q_out[...] = (_rope(q, cos_t, sin_t) * (HEAD_DIM ** -0.5)).astype(BF16)
    k = _head_norm(main[:, ATTN_WIDTH:ATTN_WIDTH + KV_WIDTH], bd[0:KV_WIDTH, 0:KV_WIDTH], kg_ref[...])
    k_out[...] = _rope(k, cos_t, sin_t).astype(BF16)
    v_out[...] = main[:, ATTN_WIDTH + KV_WIDTH:ATTN_WIDTH + 2 * KV_WIDTH].astype(BF16)
    u = proj[:, ATTN_WIDTH + 2 * KV_WIDTH:IN_WIDTH]
    t_loc = (lax.broadcasted_iota(jnp.int32, (tm + 2 * halo, LANES), 0)
             + (pl.program_id(1) * tm - halo))
    mixed = _pool_mix(u, t_loc, seq_len, wpool_ref, halo, tm) * pscale_ref[...]
    mixed_out[...] = mixed.astype(BF16)


def _lat_proj(layer, xs, mods, norm1_g, w_in, bd, qg, kg, cos_t, sin_t, w_pool, pool_scale):
    nb, seq_len, _ = xs.shape
    tm = LAT_TILE
    halo = POOL_HALO
    n_halo_blocks = seq_len // halo
    const2 = lambda b, i: (0, 0)
    tile = lambda b, i: (b, i, 0)
    return pl.pallas_call(
        functools.partial(_lat_proj_kernel, seq_len=seq_len),
        out_shape=(
            jax.ShapeDtypeStruct((nb, seq_len, ATTN_WIDTH), BF16),
            jax.ShapeDtypeStruct((nb, seq_len, KV_WIDTH), BF16),
            jax.ShapeDtypeStruct((nb, seq_len, KV_WIDTH), BF16),
            jax.ShapeDtypeStruct((nb, seq_len, POOL_WIDTH), BF16),
        ),
        grid=(nb, seq_len // tm),
        in_specs=[
            pl.BlockSpec((None, halo, D_MODEL),
                         lambda b, i: (b, jnp.maximum(i * (tm // halo) - 1, 0), 0)),
            pl.BlockSpec((None, tm, D_MODEL), tile),
            pl.BlockSpec((None, halo, D_MODEL),
                         lambda b, i: (b, jnp.minimum((i + 1) * (tm // halo), n_halo_blocks - 1), 0)),
            pl.BlockSpec((None, None, N_MOD, D_MODEL), lambda b, i: (layer, 1 + b, 0, 0)),
            pl.BlockSpec((1, D_MODEL), const2),
            pl.BlockSpec((None, D_MODEL, IN_WIDTH), lambda b, i: (layer, 0, 0)),
            pl.BlockSpec((ATTN_WIDTH, ATTN_WIDTH), const2),
            pl.BlockSpec((1, ATTN_WIDTH), const2),
            pl.BlockSpec((1, KV_WIDTH), const2),
            pl.BlockSpec((tm, LANES), lambda b, i: (i, 0)),
            pl.BlockSpec((tm, LANES), lambda b, i: (i, 0)),
            pl.BlockSpec((None, 4, POOL_GROUP_DIM, POOL_GROUP_DIM), lambda b, i: (layer, 0, 0, 0)),
            pl.BlockSpec((1, POOL_WIDTH), const2),
        ],
        out_specs=(
            pl.BlockSpec((None, tm, ATTN_WIDTH), tile),
            pl.BlockSpec((None, tm, KV_WIDTH), tile),
            pl.BlockSpec((None, tm, KV_WIDTH), tile),
            pl.BlockSpec((None, tm, POOL_WIDTH), tile),
        ),
        compiler_params=pltpu.CompilerParams(
            dimension_semantics=("arbitrary", "arbitrary"), vmem_limit_bytes=VMEM_LIMIT),
        name=f"lat_proj_{layer}",
    )(xs, xs, xs, mods, norm1_g, w_in, bd, qg, kg, cos_t, sin_t, w_pool, pool_scale)


def _lat_attn_kernel(q_ref, klat_ref, vlat_ref, ck_ref, cv_ref, mixed_ref, x_ref, mod_ref, wout_ref,
                     n2g_ref, x1_out, h2_out, kdup_scr, vdup_scr):
    @pl.when(pl.program_id(1) == 0)
    def _():
        k_all = jnp.concatenate([ck_ref[...], klat_ref[...].astype(F32)], axis=0)
        v_all = jnp.concatenate([cv_ref[...], vlat_ref[...].astype(F32)], axis=0)
        k0, k1 = _dup_halves(k_all)
        v0, v1 = _dup_halves(v_all)
        kdup_scr[0] = k0
        kdup_scr[1] = k1
        vdup_scr[0] = v0
        vdup_scr[1] = v1

    q = q_ref[...]
    blocks = []
    for g in range(N_KV_HEADS):
        q_a = q[:, (2 * g) * LANES:(2 * g + 1) * LANES]
        q_b = q[:, (2 * g + 1) * LANES:(2 * g + 2) * LANES]
        blocks.extend(_attend_group(q_a, q_b, kdup_scr[g], vdup_scr[g]))
    attn = jnp.concatenate(blocks, axis=1)
    _mix_residual_norm2(x_ref[...], attn, mixed_ref[...], wout_ref, mod_ref, n2g_ref, x1_out, h2_out)


def _lat_attn(layer, q, k, v, cache_k, cache_v, mixed, xs, mods, w_out, norm2_g):
    nb, seq_len, _ = xs.shape
    past = cache_k.shape[2]
    tq = LAT_Q_TILE
    n_keys = past + seq_len
    tile = lambda b, j: (b, j, 0)
    whole = lambda b, j: (b, 0, 0)
    return pl.pallas_call(
        _lat_attn_kernel,
        out_shape=(
            jax.ShapeDtypeStruct((nb, seq_len, D_MODEL), F32),
            jax.ShapeDtypeStruct((nb, seq_len, D_MODEL), BF16),
        ),
        grid=(nb, seq_len // tq),
        in_specs=[
            pl.BlockSpec((None, tq, ATTN_WIDTH), tile),
            pl.BlockSpec((None, seq_len, KV_WIDTH), whole),
            pl.BlockSpec((None, seq_len, KV_WIDTH), whole),
            pl.BlockSpec((None, None, past, KV_WIDTH), lambda b, j: (b, layer, 0, 0)),
            pl.BlockSpec((None, None, past, KV_WIDTH), lambda b, j: (b, layer, 0, 0)),
            pl.BlockSpec((None, tq, POOL_WIDTH), tile),
            pl.BlockSpec((None, tq, D_MODEL), tile),
            pl.BlockSpec((None, None, N_MOD, D_MODEL), lambda b, j: (layer, 1 + b, 0, 0)),
            pl.BlockSpec((None, D_MODEL, D_MODEL), lambda b, j: (layer, 0, 0)),
            pl.BlockSpec((1, D_MODEL), lambda b, j: (0, 0)),
        ],
        out_specs=(
            pl.BlockSpec((None, tq, D_MODEL), tile),
            pl.BlockSpec((None, tq, D_MODEL), tile),
        ),
        scratch_shapes=[
            pltpu.VMEM((N_KV_HEADS, n_keys, LANES), BF16),
            pltpu.VMEM((N_KV_HEADS, n_keys, LANES), BF16),
        ],
        compiler_params=pltpu.CompilerParams(
            dimension_semantics=("arbitrary", "arbitrary"), vmem_limit_bytes=VMEM_LIMIT),
        name=f"lat_attn_{layer}",
    )(q, k, v, cache_k, cache_v, mixed, xs, mods, w_out, norm2_g)


def _ffn_kernel(*refs, seq_len, final_norm, has_halo):
    if has_halo:
        (x1_ref, h2_ref, h2prev_ref, h2next_ref, mod_ref, wup_ref, cw_ref, cb_ref, wdn_ref, fg_ref,
         out_ref, z_scr, act_scr) = refs
    else:
        (x1_ref, h2_ref, mod_ref, wup_ref, cw_ref, cb_ref, wdn_ref, fg_ref,
         out_ref, z_scr, act_scr) = refs
    tm = x1_ref.shape[0]
    gap = F32_SUBLANES
    if has_halo:
        halo = BF16_SUBLANES
        tile_start = pl.program_id(1) * tm
        h2prev = jnp.where(tile_start % seq_len != 0, h2prev_ref[...], jnp.zeros_like(h2prev_ref))
        h2next = jnp.where((tile_start + tm) % seq_len != 0, h2next_ref[...], jnp.zeros_like(h2next_ref))
        he = jnp.concatenate([h2prev, h2_ref[...], h2next], axis=0)
        segments = [(halo, tm)]
    else:
        he = h2_ref[...]
        segments = [(gap + s * (seq_len + gap), seq_len) for s in range(tm // seq_len)]
        for slab in range(z_scr.shape[0]):
            for s in range(tm // seq_len + 1):
                r = s * (seq_len + gap)
                z_scr[slab, r:r + gap, :] = jnp.zeros((gap, LANES), F32)

    def conv(slab, z, col0):
        cols = slice(col0, col0 + LANES)
        if has_halo:
            z_scr[slab, 0:z.shape[0], :] = z
        else:
            for s, (row0, n) in enumerate(segments):
                z_scr[slab, row0:row0 + n, :] = z[s * n:(s + 1) * n]
        outs = []
        for row0, n in segments:
            outs.append(z_scr[slab, pl.ds(row0 - 1, n, stride=1), :] * cw_ref[0:1, cols]
                        + z_scr[slab, row0:row0 + n, :] * cw_ref[1:2, cols]
                        + z_scr[slab, pl.ds(row0 + 1, n, stride=1), :] * cw_ref[2:3, cols]
                        + cb_ref[0:1, cols])
        return outs[0] if len(outs) == 1 else jnp.concatenate(outs, axis=0)

    n_sub = FFN_CHUNK // LANES
    for ci in range(D_FF // FFN_CHUNK):
        c0 = ci * FFN_CHUNK
        slab0 = 2 * n_sub * (ci % 2)
        za = _dot(he, wup_ref[:, c0:c0 + FFN_CHUNK])
        zg = _dot(he, wup_ref[:, D_FF + c0:D_FF + c0 + FFN_CHUNK])
        for j in range(n_sub):
            lanes = slice(j * LANES, (j + 1) * LANES)
            a = conv(slab0 + 2 * j, za[:, lanes], c0 + j * LANES)
            g = conv(slab0 + 2 * j + 1, zg[:, lanes], D_FF + c0 + j * LANES)
            act_scr[:, c0 + j * LANES:c0 + (j + 1) * LANES] = (jax.nn.silu(a) * g).astype(BF16)
    x2 = x1_ref[...] + mod_ref[5:6, :] * _dot(act_scr[...], wdn_ref[...])
    if final_norm:
        ms = jnp.mean(x2 * x2, axis=-1, keepdims=True)
        x2 = x2 * lax.rsqrt(ms + EPS) * fg_ref[...]
    out_ref[...] = x2


def _ffn(layer, x1, h2, seq_len, mod_row0, mods, w_up, conv_w, conv_b, w_down, final_g, final_norm):
    nb, n_tok, _ = x1.shape
    tm = FFN_TILE
    has_halo = seq_len > tm
    assert seq_len % tm == 0 or tm % seq_len == 0
    halo = BF16_SUBLANES
    gap = F32_SUBLANES
    n_halo_blocks = n_tok // halo
    tile = lambda b, i: (b, i, 0)
    halo_specs = [
        pl.BlockSpec((None, halo, D_MODEL),
                     lambda b, i: (b, jnp.maximum(i * (tm // halo) - 1, 0), 0)),
        pl.BlockSpec((None, halo, D_MODEL),
                     lambda b, i: (b, jnp.minimum((i + 1) * (tm // halo), n_halo_blocks - 1), 0)),
    ] if has_halo else []
    halo_args = [h2, h2] if has_halo else []
    z_rows = tm + 2 * halo if has_halo else gap + (tm // seq_len) * (seq_len + gap)
    return pl.pallas_call(
        functools.partial(_ffn_kernel, seq_len=seq_len, final_norm=final_norm, has_halo=has_halo),
        out_shape=jax.ShapeDtypeStruct((nb, n_tok, D_MODEL), F32),
        grid=(nb, n_tok // tm),
        in_specs=[
            pl.BlockSpec((None, tm, D_MODEL), tile),
            pl.BlockSpec((None, tm, D_MODEL), tile),
            *halo_specs,
            pl.BlockSpec((None, None, N_MOD, D_MODEL), lambda b, i: (layer, mod_row0 + b, 0, 0)),
            pl.BlockSpec((None, D_MODEL, 2 * D_FF), lambda b, i: (layer, 0, 0),
                         pipeline_mode=pl.Buffered(1)),
            pl.BlockSpec((None, 3, 2 * D_FF), lambda b, i: (layer, 0, 0)),
            pl.BlockSpec((None, 1, 2 * D_FF), lambda b, i: (layer, 0, 0)),
            pl.BlockSpec((None, D_FF, D_MODEL), lambda b, i: (layer, 0, 0),
                         pipeline_mode=pl.Buffered(1)),
            pl.BlockSpec((1, D_MODEL), lambda b, i: (0, 0)),
        ],
        out_specs=pl.BlockSpec((None, tm, D_MODEL), tile),
        scratch_shapes=[
            pltpu.VMEM((4 * (FFN_CHUNK // LANES), z_rows, LANES), F32),
            pltpu.VMEM((tm, D_FF), BF16),
        ],
        compiler_params=pltpu.CompilerParams(
            dimension_semantics=("arbitrary", "arbitrary"), vmem_limit_bytes=VMEM_LIMIT),
        name=f"ffn_{'lat' if nb > 1 else 'ctx'}_{layer}",
    )(x1, h2, *halo_args, mods, w_up, conv_w, conv_b, w_down, final_g)


def _rope_tables(n_tokens):
    t = jnp.arange(n_tokens)
    row = (t // GRID_W).astype(F32)
    col = (t % GRID_W).astype(F32)
    n_freq = HEAD_DIM // 4
    inv = ROPE_THETA ** (-jnp.arange(n_freq, dtype=F32) / n_freq)
    ang = jnp.stack([row[:, None] * inv, col[:, None] * inv], axis=1)
    cos = jnp.broadcast_to(jnp.cos(ang)[:, :, None, :], (n_tokens, 2, 2, n_freq))
    sin = jnp.sin(ang)[:, :, None, :] * jnp.array([-1.0, 1.0], F32)[None, None, :, None]
    cos = cos.reshape(n_tokens, HEAD_DIM)
    sin = sin.reshape(n_tokens, HEAD_DIM)
    return jnp.tile(cos, (1, 2)), jnp.tile(sin, (1, 2))


def kernel(x_prompt, x_sample, cache_k, cache_v, c, c_ctx, w_mod, b_mod, norm1_g, w_in, q_norm_g, k_norm_g, w_pool, pool_scale, w_out, norm2_g, w_up, conv_w, conv_b, w_down, final_norm_g):
    batch, seq, d = x_prompt.shape
    dec_batch, dec_seq, _ = x_sample.shape
    past = cache_k.shape[2]

    w_in_b = w_in.astype(BF16)
    w_pool_b = w_pool.astype(BF16)
    w_out_b = w_out.astype(BF16)
    w_up_b = w_up.astype(BF16)
    w_down_b = w_down.astype(BF16)

    cond8 = jnp.zeros((8, d), F32).at[0].set(c_ctx).at[1:1 + dec_batch].set(c)
    mods = _modulation(cond8, w_mod, b_mod).reshape(DEPTH, 8, N_MOD, d)

    head_id = jnp.arange(ATTN_WIDTH) // HEAD_DIM
    blockdiag = (head_id[:, None] == head_id[None, :]).astype(BF16)
    cos_t, sin_t = _rope_tables(dec_seq)
    ck = cache_k.reshape(dec_batch, DEPTH, past, KV_WIDTH)
    cv = cache_v.reshape(dec_batch, DEPTH, past, KV_WIDTH)
    final_g = final_norm_g.reshape(1, d)

    xp = x_prompt.reshape(batch * seq, d)
    xs = x_sample
    ks_out, vs_out = [], []
    for l in range(DEPTH):
        n1g = norm1_g[l].reshape(1, d)
        n2g = norm2_g[l].reshape(1, d)
        qg = jnp.tile(q_norm_g[l], N_HEADS).reshape(1, ATTN_WIDTH)
        kg = jnp.tile(k_norm_g[l], N_KV_HEADS).reshape(1, KV_WIDTH)
        ps = pool_scale[l].reshape(1, POOL_WIDTH)
        cb = conv_b.reshape(DEPTH, 1, 2 * D_FF)
        last = l == DEPTH - 1

        k_l, v_l, x1p, h2p = _ctx_front(l, xp, seq, mods, n1g, w_in_b, blockdiag, qg, kg,
                                        w_pool_b, ps, w_out_b, n2g)
        ks_out.append(k_l.reshape(batch, seq, N_KV_HEADS, HEAD_DIM))
        vs_out.append(v_l.reshape(batch, seq, N_KV_HEADS, HEAD_DIM))
        xp = _ffn(l, x1p[None], h2p[None], seq, 0, mods, w_up_b, conv_w, cb, w_down_b,
                  final_g, last)[0]

        q, k, v, mixed = _lat_proj(l, xs, mods, n1g, w_in_b, blockdiag, qg, kg, cos_t, sin_t,
                                   w_pool_b, ps)
        x1s, h2s = _lat_attn(l, q, k, v, ck, cv, mixed, xs, mods, w_out_b, n2g)
        xs = _ffn(l, x1s, h2s, dec_seq, 1, mods, w_up_b, conv_w, cb, w_down_b, final_g, last)

    y_prompt = xp.reshape(batch, seq, d)
    new_cache_k = jnp.stack(ks_out, axis=1)
    new_cache_v = jnp.stack(vs_out, axis=1)
    return (y_prompt, xs, new_cache_k, new_cache_v)
```

```python
import functools

import jax
import jax.numpy as jnp
from jax import lax
from jax.experimental import pallas as pl
from jax.experimental.pallas import tpu as pltpu

D_MODEL = 1024
DEPTH = 4
GRID_W = 64
HEAD_DIM = 64
N_HEADS = 8
N_KV_HEADS = 2
ATTN_WIDTH = N_HEADS * HEAD_DIM
KV_WIDTH = N_KV_HEADS * HEAD_DIM
POOL_WIDTH = D_MODEL - ATTN_WIDTH
POOL_WINDOWS = (2, 4, 8, 16)
POOL_GROUP_DIM = 128
IN_WIDTH = ATTN_WIDTH + 2 * KV_WIDTH + POOL_WIDTH
D_FF = 2816
ROPE_THETA = 10000.0
EPS = 1e-6
N_MOD = 6
LOG2_E = 1.4426950408889634

LANES = 128
F32_SUBLANES = 8
BF16_SUBLANES = 16
VMEM_LIMIT = 56 * 1024 * 1024

POOL_HALO = 8
FFN_CHUNK = 256
CTX_TILE = 512
LAT_TILE = 512
LAT_Q_TILE = 128
FFN_TILE = 512
SCORE_LOOKAHEAD = 2

BF16 = jnp.bfloat16
F32 = jnp.float32


def _dot(a, b):
    return jnp.dot(a, b, preferred_element_type=F32)


def _dot_nt(a, b):
    return lax.dot_general(a, b, (((1,), (1,)), ((), ())), preferred_element_type=F32)


def _rms_mod(x, g, scale1p, shift):
    ms = jnp.mean(x * x, axis=-1, keepdims=True)
    return (x * lax.rsqrt(ms + EPS) * g) * scale1p + shift


def _head_norm(t, blockdiag, g_tiled):
    ssq = _dot((t * t).astype(BF16), blockdiag)
    return t * lax.rsqrt(ssq * (1.0 / HEAD_DIM) + EPS) * g_tiled


def _rope(t, cos_t, sin_t):
    lane = lax.broadcasted_iota(jnp.int32, (t.shape[0], LANES), 1)
    first = (lane % 32) < 16
    outs = []
    for ci in range(t.shape[1] // LANES):
        tc = t[:, ci * LANES:(ci + 1) * LANES]
        partner = jnp.where(first, pltpu.roll(tc, LANES - 16, 1), pltpu.roll(tc, 16, 1))
        outs.append(tc * cos_t + partner * sin_t)
    return outs[0] if len(outs) == 1 else jnp.concatenate(outs, axis=1)


def _dup_halves(t):
    lane = lax.broadcasted_iota(jnp.int32, t.shape, 1)
    lo = lane < HEAD_DIM
    sw = pltpu.roll(t, HEAD_DIM, 1)
    return jnp.where(lo, t, sw).astype(BF16), jnp.where(lo, sw, t).astype(BF16)


def _attend_group(q_a, q_b, kdup, vdup):
    tq = q_a.shape[0]
    lane = lax.broadcasted_iota(jnp.int32, (1, LANES), 1)
    m_lo = (lane < HEAD_DIM).astype(BF16)
    m_hi = (lane >= HEAD_DIM).astype(BF16)
    lhs = jnp.concatenate([q_a * m_lo, q_a * m_hi, q_b * m_lo, q_b * m_hi], axis=0)
    s = _dot_nt(lhs, kdup)
    p = jnp.exp(s - jnp.max(s, axis=-1, keepdims=True))
    inv_l = 1.0 / jnp.sum(p, axis=-1, keepdims=True)
    o = _dot(p.astype(BF16), vdup) * inv_l
    lo = lax.broadcasted_iota(jnp.int32, (tq, LANES), 1) < HEAD_DIM
    out_a = jnp.where(lo, o[0:tq], o[tq:2 * tq])
    out_b = jnp.where(lo, o[2 * tq:3 * tq], o[3 * tq:4 * tq])
    return out_a, out_b


def _value_rows(v_all):
    vt = v_all.T
    ones = jnp.ones((HEAD_DIM, vt.shape[1]), F32)
    out = []
    for g in range(N_KV_HEADS):
        vg = vt[g * HEAD_DIM:(g + 1) * HEAD_DIM]
        out.append(jnp.concatenate([vg, ones], axis=0).astype(BF16))
        out.append(jnp.concatenate([ones, vg], axis=0).astype(BF16))
    return out


def _attend_keys_major(q, kdup_ref, vt_ref):
    tq = q.shape[0]
    lane = lax.broadcasted_iota(jnp.int32, (1, LANES), 1)
    masks = ((lane < HEAD_DIM).astype(BF16), (lane >= HEAD_DIM).astype(BF16))
    units = [(g, half) for g in range(N_KV_HEADS) for half in range(2)]

    def scores(g, half):
        q_a = q[:, (2 * g) * LANES:(2 * g + 1) * LANES]
        q_b = q[:, (2 * g + 1) * LANES:(2 * g + 2) * LANES]
        q_rows = jnp.concatenate([q_a * masks[half], q_b * masks[half]], axis=0)
        return _dot_nt(kdup_ref[g], q_rows)

    def values(g, half, st):
        pt = jnp.exp2(st - jnp.max(st, axis=0, keepdims=True)).astype(BF16)
        return _dot(vt_ref[2 * g + half], pt).T

    outs = []
    pending = [scores(*unit) for unit in units[:SCORE_LOOKAHEAD]]
    for i, unit in enumerate(units):
        if i + SCORE_LOOKAHEAD < len(units):
            pending.append(scores(*units[i + SCORE_LOOKAHEAD]))
        outs.append(values(*unit, pending.pop(0)))

    lo = lax.broadcasted_iota(jnp.int32, (2 * tq, LANES), 1) < HEAD_DIM
    blocks = []
    for g in range(N_KV_HEADS):
        o_even, o_odd = outs[2 * g], outs[2 * g + 1]
        num = jnp.where(lo, o_even, o_odd)
        den = jnp.where(lo, pltpu.roll(o_even, HEAD_DIM, 1), pltpu.roll(o_odd, HEAD_DIM, 1))
        out = num / den
        blocks.extend([out[0:tq], out[tq:2 * tq]])
    return jnp.concatenate(blocks, axis=1)


def _pool_mix(u, t_loc, seq_len, wpool_ref, lo_row, n_rows):
    m = u.shape[0]
    outs = []
    for gi, w in enumerate(POOL_WINDOWS):
        ug = u[:, gi * LANES:(gi + 1) * LANES]
        half = w // 2
        past, future, d = ug, ug, 1
        while d < half:
            past = past + jnp.where(t_loc >= d, pltpu.roll(past, d, 0), 0.0)
            future = future + jnp.where(t_loc + d < seq_len, pltpu.roll(future, m - d, 0), 0.0)
            d *= 2
        total = jnp.where(t_loc >= 1, pltpu.roll(past, 1, 0), 0.0) + future
        cnt = jnp.minimum(t_loc + half, seq_len) - jnp.maximum(t_loc - half, 0)
        pooled = total / cnt.astype(F32) - ug
        pooled = pooled[lo_row:lo_row + n_rows]
        outs.append(_dot(pooled.astype(BF16), wpool_ref[gi]))
    return jnp.concatenate(outs, axis=1)


def _mix_residual_norm2(x, attn, mixed, wout_ref, mod_ref, n2g_ref, x1_out, h2_out):
    gt1, sh2, sc2 = mod_ref[2:3, :], mod_ref[3:4, :], mod_ref[4:5, :]
    mix = (_dot(attn.astype(BF16), wout_ref[0:ATTN_WIDTH, :])
           + _dot(mixed.astype(BF16), wout_ref[ATTN_WIDTH:D_MODEL, :]))
    x1 = x + gt1 * mix
    x1_out[...] = x1
    h2_out[...] = _rms_mod(x1, n2g_ref[...], 1.0 + sc2, sh2).astype(BF16)


def _mod_kernel(cond_ref, w_ref, b_ref, out_ref):
    s = jax.nn.silu(cond_ref[...]).astype(BF16)
    out_ref[...] = _dot(s, w_ref[...].astype(BF16)) + b_ref[...]


def _modulation(cond8, w_mod, b_mod):
    tn = 1536
    n = N_MOD * D_MODEL
    return pl.pallas_call(
        _mod_kernel,
        out_shape=jax.ShapeDtypeStruct((DEPTH, 8, n), F32),
        grid=(DEPTH, n // tn),
        in_specs=[
            pl.BlockSpec((8, D_MODEL), lambda l, j: (0, 0)),
            pl.BlockSpec((None, D_MODEL, tn), lambda l, j: (l, 0, j)),
            pl.BlockSpec((None, 1, tn), lambda l, j: (l, 0, j)),
        ],
        out_specs=pl.BlockSpec((None, 8, tn), lambda l, j: (l, 0, j)),
        compiler_params=pltpu.CompilerParams(
            dimension_semantics=("arbitrary", "arbitrary"), vmem_limit_bytes=VMEM_LIMIT),
        name="modulation",
    )(cond8, w_mod, b_mod.reshape(DEPTH, 1, n))


def _ctx_front_kernel(x_ref, mod_ref, n1g_ref, win_ref, bd_ref, qg_ref, kg_ref, wpool_ref,
                      pscale_ref, wout_ref, n2g_ref, k_out, v_out, x1_out, h2_out, *, seq_len):
    x = x_ref[...]
    tm = x.shape[0]
    sh1, sc1 = mod_ref[0:1, :], mod_ref[1:2, :]
    h = _rms_mod(x, n1g_ref[...], 1.0 + sc1, sh1).astype(BF16)
    proj = _dot(h, win_ref[...])
    bd = bd_ref[...]
    q = (_head_norm(proj[:, 0:ATTN_WIDTH], bd, qg_ref[...]) * (HEAD_DIM ** -0.5)).astype(BF16)
    k = _head_norm(proj[:, ATTN_WIDTH:ATTN_WIDTH + KV_WIDTH], bd[0:KV_WIDTH, 0:KV_WIDTH], kg_ref[...])
    v = proj[:, ATTN_WIDTH + KV_WIDTH:ATTN_WIDTH + 2 * KV_WIDTH]
    u = proj[:, ATTN_WIDTH + 2 * KV_WIDTH:IN_WIDTH]
    k_out[...] = k
    v_out[...] = v

    seq_outs = []
    for s in range(tm // seq_len):
        r0 = s * seq_len
        kdup = _dup_halves(k[r0:r0 + seq_len])
        vdup = _dup_halves(v[r0:r0 + seq_len])
        blocks = []
        for g in range(N_KV_HEADS):
            q_a = q[r0:r0 + seq_len, (2 * g) * LANES:(2 * g + 1) * LANES]
            q_b = q[r0:r0 + seq_len, (2 * g + 1) * LANES:(2 * g + 2) * LANES]
            blocks.extend(_attend_group(q_a, q_b, kdup[g], vdup[g]))
        seq_outs.append(jnp.concatenate(blocks, axis=1))
    attn = jnp.concatenate(seq_outs, axis=0)

    t_loc = lax.broadcasted_iota(jnp.int32, (tm, LANES), 0) % seq_len
    mixed = _pool_mix(u, t_loc, seq_len, wpool_ref, 0, tm) * pscale_ref[...]
    _mix_residual_norm2(x, attn, mixed, wout_ref, mod_ref, n2g_ref, x1_out, h2_out)


def _ctx_front(layer, xp, seq_len, mods, norm1_g, w_in, bd, qg, kg, w_pool, pool_scale, w_out, norm2_g):
    n_tok = xp.shape[0]
    tm = CTX_TILE
    const2 = lambda i: (0, 0)
    tile = lambda i: (i, 0)
    return pl.pallas_call(
        functools.partial(_ctx_front_kernel, seq_len=seq_len),
        out_shape=(
            jax.ShapeDtypeStruct((n_tok, KV_WIDTH), F32),
            jax.ShapeDtypeStruct((n_tok, KV_WIDTH), F32),
            jax.ShapeDtypeStruct((n_tok, D_MODEL), F32),
            jax.ShapeDtypeStruct((n_tok, D_MODEL), BF16),
        ),
        grid=(n_tok // tm,),
        in_specs=[
            pl.BlockSpec((tm, D_MODEL), tile),
            pl.BlockSpec((None, None, N_MOD, D_MODEL), lambda i: (layer, 0, 0, 0)),
            pl.BlockSpec((1, D_MODEL), const2),
            pl.BlockSpec((None, D_MODEL, IN_WIDTH), lambda i: (layer, 0, 0)),
            pl.BlockSpec((ATTN_WIDTH, ATTN_WIDTH), const2),
            pl.BlockSpec((1, ATTN_WIDTH), const2),
            pl.BlockSpec((1, KV_WIDTH), const2),
            pl.BlockSpec((None, 4, POOL_GROUP_DIM, POOL_GROUP_DIM), lambda i: (layer, 0, 0, 0)),
            pl.BlockSpec((1, POOL_WIDTH), const2),
            pl.BlockSpec((None, D_MODEL, D_MODEL), lambda i: (layer, 0, 0)),
            pl.BlockSpec((1, D_MODEL), const2),
        ],
        out_specs=(
            pl.BlockSpec((tm, KV_WIDTH), tile),
            pl.BlockSpec((tm, KV_WIDTH), tile),
            pl.BlockSpec((tm, D_MODEL), tile),
            pl.BlockSpec((tm, D_MODEL), tile),
        ),
        compiler_params=pltpu.CompilerParams(
            dimension_semantics=("arbitrary",), vmem_limit_bytes=VMEM_LIMIT),
        name=f"ctx_front_{layer}",
    )(xp, mods, norm1_g, w_in, bd, qg, kg, w_pool, pool_scale, w_out, norm2_g)


def _lat_proj_kernel(xprev_ref, x_ref, xnext_ref, mod_ref, n1g_ref, win_ref, bd_ref, qg_ref, kg_ref,
                     cos_ref, sin_ref, wpool_ref, pscale_ref, q_out, k_out, v_out, mixed_out, *, seq_len):
    tm = x_ref.shape[0]
    halo = POOL_HALO
    xe = jnp.concatenate([xprev_ref[...], x_ref[...], xnext_ref[...]], axis=0)
    sh1, sc1 = mod_ref[0:1, :], mod_ref[1:2, :]
    h = _rms_mod(xe, n1g_ref[...], 1.0 + sc1, sh1).astype(BF16)
    proj = _dot(h, win_ref[...])
    main = proj[halo:halo + tm]
    bd = bd_ref[...]
    cos_t, sin_t = cos_ref[...], sin_ref[...]
    q = _head_norm(main[:, 0:ATTN_WIDTH], bd, qg_ref[...])
    q_out[...] = (_rope(q, cos_t, sin_t) * (HEAD_DIM ** -0.5 * LOG2_E)).astype(BF16)
    k = _head_norm(main[:, ATTN_WIDTH:ATTN_WIDTH + KV_WIDTH], bd[0:KV_WIDTH, 0:KV_WIDTH], kg_ref[...])
    k_out[...] = _rope(k, cos_t, sin_t).astype(BF16)
    v_out[...] = main[:, ATTN_WIDTH + KV_WIDTH:ATTN_WIDTH + 2 * KV_WIDTH].astype(BF16)
    u = proj[:, ATTN_WIDTH + 2 * KV_WIDTH:IN_WIDTH]
    t_loc = (lax.broadcasted_iota(jnp.int32, (tm + 2 * halo, LANES), 0)
             + (pl.program_id(1) * tm - halo))
    mixed = _pool_mix(u, t_loc, seq_len, wpool_ref, halo, tm) * pscale_ref[...]
    mixed_out[...] = mixed.astype(BF16)


def _lat_proj(layer, xs, mods, norm1_g, w_in, bd, qg, kg, cos_t, sin_t, w_pool, pool_scale):
    nb, seq_len, _ = xs.shape
    tm = LAT_TILE
    halo = POOL_HALO
    n_halo_blocks = seq_len // halo
    const2 = lambda b, i: (0, 0)
    tile = lambda b, i: (b, i, 0)
    return pl.pallas_call(
        functools.partial(_lat_proj_kernel, seq_len=seq_len),
        out_shape=(
            jax.ShapeDtypeStruct((nb, seq_len, ATTN_WIDTH), BF16),
            jax.ShapeDtypeStruct((nb, seq_len, KV_WIDTH), BF16),
            jax.ShapeDtypeStruct((nb, seq_len, KV_WIDTH), BF16),
            jax.ShapeDtypeStruct((nb, seq_len, POOL_WIDTH), BF16),
        ),
        grid=(nb, seq_len // tm),
        in_specs=[
            pl.BlockSpec((None, halo, D_MODEL),
                         lambda b, i: (b, jnp.maximum(i * (tm // halo) - 1, 0), 0)),
            pl.BlockSpec((None, tm, D_MODEL), tile),
            pl.BlockSpec((None, halo, D_MODEL),
                         lambda b, i: (b, jnp.minimum((i + 1) * (tm // halo), n_halo_blocks - 1), 0)),
            pl.BlockSpec((None, None, N_MOD, D_MODEL), lambda b, i: (layer, 1 + b, 0, 0)),
            pl.BlockSpec((1, D_MODEL), const2),
            pl.BlockSpec((None, D_MODEL, IN_WIDTH), lambda b, i: (layer, 0, 0)),
            pl.BlockSpec((ATTN_WIDTH, ATTN_WIDTH), const2),
            pl.BlockSpec((1, ATTN_WIDTH), const2),
            pl.BlockSpec((1, KV_WIDTH), const2),
            pl.BlockSpec((tm, LANES), lambda b, i: (i, 0)),
            pl.BlockSpec((tm, LANES), lambda b, i: (i, 0)),
            pl.BlockSpec((None, 4, POOL_GROUP_DIM, POOL_GROUP_DIM), lambda b, i: (layer, 0, 0, 0)),
            pl.BlockSpec((1, POOL_WIDTH), const2),
        ],
        out_specs=(
            pl.BlockSpec((None, tm, ATTN_WIDTH), tile),
            pl.BlockSpec((None, tm, KV_WIDTH), tile),
            pl.BlockSpec((None, tm, KV_WIDTH), tile),
            pl.BlockSpec((None, tm, POOL_WIDTH), tile),
        ),
        compiler_params=pltpu.CompilerParams(
            dimension_semantics=("arbitrary", "arbitrary"), vmem_limit_bytes=VMEM_LIMIT),
        name=f"lat_proj_{layer}",
    )(xs, xs, xs, mods, norm1_g, w_in, bd, qg, kg, cos_t, sin_t, w_pool, pool_scale)


def _lat_attn_kernel(q_ref, klat_ref, vlat_ref, ck_ref, cv_ref, mixed_ref, x_ref, mod_ref, wout_ref,
                     n2g_ref, x1_out, h2_out, kdup_scr, vt_scr):
    @pl.when(pl.program_id(1) == 0)
    def _():
        k_all = jnp.concatenate([ck_ref[...], klat_ref[...].astype(F32)], axis=0)
        v_all = jnp.concatenate([cv_ref[...], vlat_ref[...].astype(F32)], axis=0)
        k0, k1 = _dup_halves(k_all)
        kdup_scr[0] = k0
        kdup_scr[1] = k1
        for i, vt in enumerate(_value_rows(v_all)):
            vt_scr[i] = vt

    attn = _attend_keys_major(q_ref[...], kdup_scr, vt_scr)
    _mix_residual_norm2(x_ref[...], attn, mixed_ref[...], wout_ref, mod_ref, n2g_ref, x1_out, h2_out)


def _lat_attn(layer, q, k, v, cache_k, cache_v, mixed, xs, mods, w_out, norm2_g):
    nb, seq_len, _ = xs.shape
    past = cache_k.shape[2]
    tq = LAT_Q_TILE
    n_keys = past + seq_len
    tile = lambda b, j: (b, j, 0)
    whole = lambda b, j: (b, 0, 0)
    return pl.pallas_call(
        _lat_attn_kernel,
        out_shape=(
            jax.ShapeDtypeStruct((nb, seq_len, D_MODEL), F32),
            jax.ShapeDtypeStruct((nb, seq_len, D_MODEL), BF16),
        ),
        grid=(nb, seq_len // tq),
        in_specs=[
            pl.BlockSpec((None, tq, ATTN_WIDTH), tile),
            pl.BlockSpec((None, seq_len, KV_WIDTH), whole),
            pl.BlockSpec((None, seq_len, KV_WIDTH), whole),
            pl.BlockSpec((None, None, past, KV_WIDTH), lambda b, j: (b, layer, 0, 0)),
            pl.BlockSpec((None, None, past, KV_WIDTH), lambda b, j: (b, layer, 0, 0)),
            pl.BlockSpec((None, tq, POOL_WIDTH), tile),
            pl.BlockSpec((None, tq, D_MODEL), tile),
            pl.BlockSpec((None, None, N_MOD, D_MODEL), lambda b, j: (layer, 1 + b, 0, 0)),
            pl.BlockSpec((None, D_MODEL, D_MODEL), lambda b, j: (layer, 0, 0)),
            pl.BlockSpec((1, D_MODEL), lambda b, j: (0, 0)),
        ],
        out_specs=(
            pl.BlockSpec((None, tq, D_MODEL), tile),
            pl.BlockSpec((None, tq, D_MODEL), tile),
        ),
        scratch_shapes=[
            pltpu.VMEM((N_KV_HEADS, n_keys, LANES), BF16),
            pltpu.VMEM((2 * N_KV_HEADS, LANES, n_keys), BF16),
        ],
        compiler_params=pltpu.CompilerParams(
            dimension_semantics=("arbitrary", "arbitrary"), vmem_limit_bytes=VMEM_LIMIT),
        name=f"lat_attn_{layer}",
    )(q, k, v, cache_k, cache_v, mixed, xs, mods, w_out, norm2_g)


def _ffn_kernel(*refs, seq_len, final_norm, has_halo):
    if has_halo:
        (x1_ref, h2_ref, h2prev_ref, h2next_ref, mod_ref, wup_ref, cw_ref, cb_ref, wdn_ref, fg_ref,
         out_ref, z_scr, act_scr) = refs
    else:
        (x1_ref, h2_ref, mod_ref, wup_ref, cw_ref, cb_ref, wdn_ref, fg_ref,
         out_ref, z_scr, act_scr) = refs
    tm = x1_ref.shape[0]
    gap = F32_SUBLANES
    if has_halo:
        halo = BF16_SUBLANES
        tile_start = pl.program_id(1) * tm
        h2prev = jnp.where(tile_start % seq_len != 0, h2prev_ref[...], jnp.zeros_like(h2prev_ref))
        h2next = jnp.where((tile_start + tm) % seq_len != 0, h2next_ref[...], jnp.zeros_like(h2next_ref))
        he = jnp.concatenate([h2prev, h2_ref[...], h2next], axis=0)
        segments = [(halo, tm)]
    else:
        he = h2_ref[...]
        segments = [(gap + s * (seq_len + gap), seq_len) for s in range(tm // seq_len)]
        for slab in range(z_scr.shape[0]):
            for s in range(tm // seq_len + 1):
                r = s * (seq_len + gap)
                z_scr[slab, r:r + gap, :] = jnp.zeros((gap, LANES), F32)

    def conv(slab, z, col0):
        cols = slice(col0, col0 + LANES)
        if has_halo:
            z_scr[slab, 0:z.shape[0], :] = z
        else:
            for s, (row0, n) in enumerate(segments):
                z_scr[slab, row0:row0 + n, :] = z[s * n:(s + 1) * n]
        outs = []
        for row0, n in segments:
            outs.append(z_scr[slab, pl.ds(row0 - 1, n, stride=1), :] * cw_ref[0:1, cols]
                        + z_scr[slab, row0:row0 + n, :] * cw_ref[1:2, cols]
                        + z_scr[slab, pl.ds(row0 + 1, n, stride=1), :] * cw_ref[2:3, cols]
                        + cb_ref[0:1, cols])
        return outs[0] if len(outs) == 1 else jnp.concatenate(outs, axis=0)

    n_sub = FFN_CHUNK // LANES
    for ci in range(D_FF // FFN_CHUNK):
        c0 = ci * FFN_CHUNK
        slab0 = 2 * n_sub * (ci % 2)
        za = _dot(he, wup_ref[:, c0:c0 + FFN_CHUNK])
        zg = _dot(he, wup_ref[:, D_FF + c0:D_FF + c0 + FFN_CHUNK])
        for j in range(n_sub):
            lanes = slice(j * LANES, (j + 1) * LANES)
            a = conv(slab0 + 2 * j, za[:, lanes], c0 + j * LANES)
            g = conv(slab0 + 2 * j + 1, zg[:, lanes], D_FF + c0 + j * LANES)
            act_scr[:, c0 + j * LANES:c0 + (j + 1) * LANES] = (jax.nn.silu(a) * g).astype(BF16)
    x2 = x1_ref[...] + mod_ref[5:6, :] * _dot(act_scr[...], wdn_ref[...])
    if final_norm:
        ms = jnp.mean(x2 * x2, axis=-1, keepdims=True)
        x2 = x2 * lax.rsqrt(ms + EPS) * fg_ref[...]
    out_ref[...] = x2


def _ffn(layer, x1, h2, seq_len, mod_row0, mods, w_up, conv_w, conv_b, w_down, final_g, final_norm):
    nb, n_tok, _ = x1.shape
    tm = FFN_TILE
    has_halo = seq_len > tm
    assert seq_len % tm == 0 or tm % seq_len == 0
    halo = BF16_SUBLANES
    gap = F32_SUBLANES
    n_halo_blocks = n_tok // halo
    tile = lambda b, i: (b, i, 0)
    halo_specs = [
        pl.BlockSpec((None, halo, D_MODEL),
                     lambda b, i: (b, jnp.maximum(i * (tm // halo) - 1, 0), 0)),
        pl.BlockSpec((None, halo, D_MODEL),
                     lambda b, i: (b, jnp.minimum((i + 1) * (tm // halo), n_halo_blocks - 1), 0)),
    ] if has_halo else []
    halo_args = [h2, h2] if has_halo else []
    z_rows = tm + 2 * halo if has_halo else gap + (tm // seq_len) * (seq_len + gap)
    return pl.pallas_call(
        functools.partial(_ffn_kernel, seq_len=seq_len, final_norm=final_norm, has_halo=has_halo),
        out_shape=jax.ShapeDtypeStruct((nb, n_tok, D_MODEL), F32),
        grid=(nb, n_tok // tm),
        in_specs=[
            pl.BlockSpec((None, tm, D_MODEL), tile),
            pl.BlockSpec((None, tm, D_MODEL), tile),
            *halo_specs,
            pl.BlockSpec((None, None, N_MOD, D_MODEL), lambda b, i: (layer, mod_row0 + b, 0, 0)),
            pl.BlockSpec((None, D_MODEL, 2 * D_FF), lambda b, i: (layer, 0, 0),
                         pipeline_mode=pl.Buffered(1)),
            pl.BlockSpec((None, 3, 2 * D_FF), lambda b, i: (layer, 0, 0)),
            pl.BlockSpec((None, 1, 2 * D_FF), lambda b, i: (layer, 0, 0)),
            pl.BlockSpec((None, D_FF, D_MODEL), lambda b, i: (layer, 0, 0),
                         pipeline_mode=pl.Buffered(1)),
            pl.BlockSpec((1, D_MODEL), lambda b, i: (0, 0)),
        ],
        out_specs=pl.BlockSpec((None, tm, D_MODEL), tile),
        scratch_shapes=[
            pltpu.VMEM((4 * (FFN_CHUNK // LANES), z_rows, LANES), F32),
            pltpu.VMEM((tm, D_FF), BF16),
        ],
        compiler_params=pltpu.CompilerParams(
            dimension_semantics=("arbitrary", "arbitrary"), vmem_limit_bytes=VMEM_LIMIT),
        name=f"ffn_{'lat' if nb > 1 else 'ctx'}_{layer}",
    )(x1, h2, *halo_args, mods, w_up, conv_w, conv_b, w_down, final_g)


def _rope_tables(n_tokens):
    t = jnp.arange(n_tokens)
    row = (t // GRID_W).astype(F32)
    col = (t % GRID_W).astype(F32)
    n_freq = HEAD_DIM // 4
    inv = ROPE_THETA ** (-jnp.arange(n_freq, dtype=F32) / n_freq)
    ang = jnp.stack([row[:, None] * inv, col[:, None] * inv], axis=1)
    cos = jnp.broadcast_to(jnp.cos(ang)[:, :, None, :], (n_tokens, 2, 2, n_freq))
    sin = jnp.sin(ang)[:, :, None, :] * jnp.array([-1.0, 1.0], F32)[None, None, :, None]
    cos = cos.reshape(n_tokens, HEAD_DIM)
    sin = sin.reshape(n_tokens, HEAD_DIM)
    return jnp.tile(cos, (1, 2)), jnp.tile(sin, (1, 2))


def kernel(x_prompt, x_sample, cache_k, cache_v, c, c_ctx, w_mod, b_mod, norm1_g, w_in, q_norm_g, k_norm_g, w_pool, pool_scale, w_out, norm2_g, w_up, conv_w, conv_b, w_down, final_norm_g):
    batch, seq, d = x_prompt.shape
    dec_batch, dec_seq, _ = x_sample.shape
    past = cache_k.shape[2]

    w_in_b = w_in.astype(BF16)
    w_pool_b = w_pool.astype(BF16)
    w_out_b = w_out.astype(BF16)
    w_up_b = w_up.astype(BF16)
    w_down_b = w_down.astype(BF16)

    cond8 = jnp.zeros((8, d), F32).at[0].set(c_ctx).at[1:1 + dec_batch].set(c)
    mods = _modulation(cond8, w_mod, b_mod).reshape(DEPTH, 8, N_MOD, d)

    head_id = jnp.arange(ATTN_WIDTH) // HEAD_DIM
    blockdiag = (head_id[:, None] == head_id[None, :]).astype(BF16)
    cos_t, sin_t = _rope_tables(dec_seq)
    ck = cache_k.reshape(dec_batch, DEPTH, past, KV_WIDTH)
    cv = cache_v.reshape(dec_batch, DEPTH, past, KV_WIDTH)
    final_g = final_norm_g.reshape(1, d)

    xp = x_prompt.reshape(batch * seq, d)
    xs = x_sample
    ks_out, vs_out = [], []
    for l in range(DEPTH):
        n1g = norm1_g[l].reshape(1, d)
        n2g = norm2_g[l].reshape(1, d)
        qg = jnp.tile(q_norm_g[l], N_HEADS).reshape(1, ATTN_WIDTH)
        kg = jnp.tile(k_norm_g[l], N_KV_HEADS).reshape(1, KV_WIDTH)
        ps = pool_scale[l].reshape(1, POOL_WIDTH)
        cb = conv_b.reshape(DEPTH, 1, 2 * D_FF)
        last = l == DEPTH - 1

        k_l, v_l, x1p, h2p = _ctx_front(l, xp, seq, mods, n1g, w_in_b, blockdiag, qg, kg,
                                        w_pool_b, ps, w_out_b, n2g)
        ks_out.append(k_l.reshape(batch, seq, N_KV_HEADS, HEAD_DIM))
        vs_out.append(v_l.reshape(batch, seq, N_KV_HEADS, HEAD_DIM))
        xp = _ffn(l, x1p[None], h2p[None], seq, 0, mods, w_up_b, conv_w, cb, w_down_b,
                  final_g, last)[0]

        q, k, v, mixed = _lat_proj(l, xs, mods, n1g, w_in_b, blockdiag, qg, kg, cos_t, sin_t,
                                   w_pool_b, ps)
        x1s, h2s = _lat_attn(l, q, k, v, ck, cv, mixed, xs, mods, w_out_b, n2g)
        xs = _ffn(l, x1s, h2s, dec_seq, 1, mods, w_up_b, conv_w, cb, w_down_b, final_g, last)

    y_prompt = xp.reshape(batch, seq, d)
    new_cache_k = jnp.stack(ks_out, axis=1)
    new_cache_v = jnp.stack(vs_out, axis=1)
    return (y_prompt, xs, new_cache_k, new_cache_v)
```

```python
import functools

import jax
import jax.numpy as jnp
from jax import lax
from jax.experimental import pallas as pl
from jax.experimental.pallas import tpu as pltpu

D_MODEL = 1024
DEPTH = 4
GRID_W = 64
HEAD_DIM = 64
N_HEADS = 8
N_KV_HEADS = 2
ATTN_WIDTH = N_HEADS * HEAD_DIM
KV_WIDTH = N_KV_HEADS * HEAD_DIM
POOL_WIDTH = D_MODEL - ATTN_WIDTH
POOL_WINDOWS = (2, 4, 8, 16)
POOL_GROUP_DIM = 128
IN_WIDTH = ATTN_WIDTH + 2 * KV_WIDTH + POOL_WIDTH
D_FF = 2816
ROPE_THETA = 10000.0
EPS = 1e-6
N_MOD = 6
LOG2_E = 1.4426950408889634

LANES = 128
F32_SUBLANES = 8
BF16_SUBLANES = 16
VMEM_LIMIT = 56 * 1024 * 1024

POOL_HALO = 8
FFN_CHUNK = 256
CTX_TILE = 512
LAT_TILE = 512
LAT_Q_TILE = 256
FFN_TILE = 512
SCORE_LOOKAHEAD = 2

BF16 = jnp.bfloat16
F32 = jnp.float32


def _dot(a, b):
    return jnp.dot(a, b, preferred_element_type=F32)


def _dot_nt(a, b):
    return lax.dot_general(a, b, (((1,), (1,)), ((), ())), preferred_element_type=F32)


def _rms_mod(x, g, scale1p, shift):
    ms = jnp.mean(x * x, axis=-1, keepdims=True)
    return (x * lax.rsqrt(ms + EPS) * g) * scale1p + shift


def _head_norm(t, blockdiag, g_tiled):
    ssq = _dot((t * t).astype(BF16), blockdiag)
    return t * lax.rsqrt(ssq * (1.0 / HEAD_DIM) + EPS) * g_tiled


def _rope(t, cos_t, sin_t):
    lane = lax.broadcasted_iota(jnp.int32, (t.shape[0], LANES), 1)
    first = (lane % 32) < 16
    outs = []
    for ci in range(t.shape[1] // LANES):
        tc = t[:, ci * LANES:(ci + 1) * LANES]
        partner = jnp.where(first, pltpu.roll(tc, LANES - 16, 1), pltpu.roll(tc, 16, 1))
        outs.append(tc * cos_t + partner * sin_t)
    return outs[0] if len(outs) == 1 else jnp.concatenate(outs, axis=1)


def _dup_halves(t):
    lane = lax.broadcasted_iota(jnp.int32, t.shape, 1)
    lo = lane < HEAD_DIM
    sw = pltpu.roll(t, HEAD_DIM, 1)
    return jnp.where(lo, t, sw).astype(BF16), jnp.where(lo, sw, t).astype(BF16)


def _value_rows(v_all):
    vt = v_all.T
    ones = jnp.ones((HEAD_DIM, vt.shape[1]), F32)
    out = []
    for g in range(N_KV_HEADS):
        vg = vt[g * HEAD_DIM:(g + 1) * HEAD_DIM]
        out.append(jnp.concatenate([vg, ones], axis=0).astype(BF16))
        out.append(jnp.concatenate([ones, vg], axis=0).astype(BF16))
    return out


def _attend_keys_major(q, kdup_ref, vt_ref):
    tq = q.shape[0]
    lane = lax.broadcasted_iota(jnp.int32, (1, LANES), 1)
    masks = ((lane < HEAD_DIM).astype(BF16), (lane >= HEAD_DIM).astype(BF16))
    units = [(g, half) for g in range(N_KV_HEADS) for half in range(2)]

    def scores(g, half):
        q_a = q[:, (2 * g) * LANES:(2 * g + 1) * LANES]
        q_b = q[:, (2 * g + 1) * LANES:(2 * g + 2) * LANES]
        q_rows = jnp.concatenate([q_a * masks[half], q_b * masks[half]], axis=0)
        return _dot_nt(kdup_ref[g], q_rows)

    def values(g, half, st):
        pt = jnp.exp2(st - jnp.max(st, axis=0, keepdims=True)).astype(BF16)
        return _dot(vt_ref[2 * g + half], pt).T

    outs = []
    pending = [scores(*unit) for unit in units[:SCORE_LOOKAHEAD]]
    for i, unit in enumerate(units):
        if i + SCORE_LOOKAHEAD < len(units):
            pending.append(scores(*units[i + SCORE_LOOKAHEAD]))
        outs.append(values(*unit, pending.pop(0)))

    lo = lax.broadcasted_iota(jnp.int32, (2 * tq, LANES), 1) < HEAD_DIM
    blocks = []
    for g in range(N_KV_HEADS):
        o_even, o_odd = outs[2 * g], outs[2 * g + 1]
        num = jnp.where(lo, o_even, o_odd)
        den = jnp.where(lo, pltpu.roll(o_even, HEAD_DIM, 1), pltpu.roll(o_odd, HEAD_DIM, 1))
        out = num / den
        blocks.extend([out[0:tq], out[tq:2 * tq]])
    return jnp.concatenate(blocks, axis=1)


def _pool_mix(u, t_loc, seq_len, wpool_ref, lo_row, n_rows):
    m = u.shape[0]
    outs = []
    for gi, w in enumerate(POOL_WINDOWS):
        ug = u[:, gi * LANES:(gi + 1) * LANES]
        half = w // 2
        past, future, d = ug, ug, 1
        while d < half:
            past = past + jnp.where(t_loc >= d, pltpu.roll(past, d, 0), 0.0)
            future = future + jnp.where(t_loc + d < seq_len, pltpu.roll(future, m - d, 0), 0.0)
            d *= 2
        total = jnp.where(t_loc >= 1, pltpu.roll(past, 1, 0), 0.0) + future
        cnt = jnp.minimum(t_loc + half, seq_len) - jnp.maximum(t_loc - half, 0)
        pooled = total / cnt.astype(F32) - ug
        pooled = pooled[lo_row:lo_row + n_rows]
        outs.append(_dot(pooled.astype(BF16), wpool_ref[gi]))
    return jnp.concatenate(outs, axis=1)


def _mix_residual_norm2(x, attn, mixed, wout_ref, mod_ref, n2g_ref, x1_out, h2_out):
    gt1, sh2, sc2 = mod_ref[2:3, :], mod_ref[3:4, :], mod_ref[4:5, :]
    mix = (_dot(attn.astype(BF16), wout_ref[0:ATTN_WIDTH, :])
           + _dot(mixed.astype(BF16), wout_ref[ATTN_WIDTH:D_MODEL, :]))
    x1 = x + gt1 * mix
    x1_out[...] = x1
    h2_out[...] = _rms_mod(x1, n2g_ref[...], 1.0 + sc2, sh2).astype(BF16)


def _mod_kernel(cond_ref, w_ref, b_ref, out_ref):
    s = jax.nn.silu(cond_ref[...]).astype(BF16)
    out_ref[...] = _dot(s, w_ref[...].astype(BF16)) + b_ref[...]


def _modulation(cond8, w_mod, b_mod):
    tn = 1536
    n = N_MOD * D_MODEL
    return pl.pallas_call(
        _mod_kernel,
        out_shape=jax.ShapeDtypeStruct((DEPTH, 8, n), F32),
        grid=(DEPTH, n // tn),
        in_specs=[
            pl.BlockSpec((8, D_MODEL), lambda l, j: (0, 0)),
            pl.BlockSpec((None, D_MODEL, tn), lambda l, j: (l, 0, j)),
            pl.BlockSpec((None, 1, tn), lambda l, j: (l, 0, j)),
        ],
        out_specs=pl.BlockSpec((None, 8, tn), lambda l, j: (l, 0, j)),
        compiler_params=pltpu.CompilerParams(
            dimension_semantics=("arbitrary", "arbitrary"), vmem_limit_bytes=VMEM_LIMIT),
        name="modulation",
    )(cond8, w_mod, b_mod.reshape(DEPTH, 1, n))


def _ctx_front_kernel(x_ref, mod_ref, n1g_ref, win_ref, bd_ref, qg_ref, kg_ref, wpool_ref,
                      pscale_ref, wout_ref, n2g_ref, k_out, v_out, x1_out, h2_out, *, seq_len):
    x = x_ref[...]
    tm = x.shape[0]
    sh1, sc1 = mod_ref[0:1, :], mod_ref[1:2, :]
    h = _rms_mod(x, n1g_ref[...], 1.0 + sc1, sh1).astype(BF16)
    proj = _dot(h, win_ref[...])
    bd = bd_ref[...]
    q = (_head_norm(proj[:, 0:ATTN_WIDTH], bd, qg_ref[...]) * (HEAD_DIM ** -0.5 * LOG2_E)).astype(BF16)
    k = _head_norm(proj[:, ATTN_WIDTH:ATTN_WIDTH + KV_WIDTH], bd[0:KV_WIDTH, 0:KV_WIDTH], kg_ref[...])
    v = proj[:, ATTN_WIDTH + KV_WIDTH:ATTN_WIDTH + 2 * KV_WIDTH]
    u = proj[:, ATTN_WIDTH + 2 * KV_WIDTH:IN_WIDTH]
    k_out[...] = k
    v_out[...] = v

    seq_outs = []
    for s in range(tm // seq_len):
        r0 = s * seq_len
        kdup = _dup_halves(k[r0:r0 + seq_len])
        vt = _value_rows(v[r0:r0 + seq_len])
        seq_outs.append(_attend_keys_major(q[r0:r0 + seq_len], kdup, vt))
    attn = jnp.concatenate(seq_outs, axis=0)

    t_loc = lax.broadcasted_iota(jnp.int32, (tm, LANES), 0) % seq_len
    mixed = _pool_mix(u, t_loc, seq_len, wpool_ref, 0, tm) * pscale_ref[...]
    _mix_residual_norm2(x, attn, mixed, wout_ref, mod_ref, n2g_ref, x1_out, h2_out)


def _ctx_front(layer, xp, seq_len, mods, norm1_g, w_in, bd, qg, kg, w_pool, pool_scale, w_out, norm2_g):
    n_tok = xp.shape[0]
    tm = CTX_TILE
    const2 = lambda i: (0, 0)
    tile = lambda i: (i, 0)
    return pl.pallas_call(
        functools.partial(_ctx_front_kernel, seq_len=seq_len),
        out_shape=(
            jax.ShapeDtypeStruct((n_tok, KV_WIDTH), F32),
            jax.ShapeDtypeStruct((n_tok, KV_WIDTH), F32),
            jax.ShapeDtypeStruct((n_tok, D_MODEL), F32),
            jax.ShapeDtypeStruct((n_tok, D_MODEL), BF16),
        ),
        grid=(n_tok // tm,),
        in_specs=[
            pl.BlockSpec((tm, D_MODEL), tile),
            pl.BlockSpec((None, None, N_MOD, D_MODEL), lambda i: (layer, 0, 0, 0)),
            pl.BlockSpec((1, D_MODEL), const2),
            pl.BlockSpec((None, D_MODEL, IN_WIDTH), lambda i: (layer, 0, 0)),
            pl.BlockSpec((ATTN_WIDTH, ATTN_WIDTH), const2),
            pl.BlockSpec((1, ATTN_WIDTH), const2),
            pl.BlockSpec((1, KV_WIDTH), const2),
            pl.BlockSpec((None, 4, POOL_GROUP_DIM, POOL_GROUP_DIM), lambda i: (layer, 0, 0, 0)),
            pl.BlockSpec((1, POOL_WIDTH), const2),
            pl.BlockSpec((None, D_MODEL, D_MODEL), lambda i: (layer, 0, 0)),
            pl.BlockSpec((1, D_MODEL), const2),
        ],
        out_specs=(
            pl.BlockSpec((tm, KV_WIDTH), tile),
            pl.BlockSpec((tm, KV_WIDTH), tile),
            pl.BlockSpec((tm, D_MODEL), tile),
            pl.BlockSpec((tm, D_MODEL), tile),
        ),
        compiler_params=pltpu.CompilerParams(
            dimension_semantics=("arbitrary",), vmem_limit_bytes=VMEM_LIMIT),
        name=f"ctx_front_{layer}",
    )(xp, mods, norm1_g, w_in, bd, qg, kg, w_pool, pool_scale, w_out, norm2_g)


def _lat_proj_kernel(xprev_ref, x_ref, xnext_ref, mod_ref, n1g_ref, win_ref, bd_ref, qg_ref, kg_ref,
                     cos_ref, sin_ref, wpool_ref, pscale_ref, q_out, k_out, v_out, mixed_out, *, seq_len):
    tm = x_ref.shape[0]
    halo = POOL_HALO
    xe = jnp.concatenate([xprev_ref[...], x_ref[...], xnext_ref[...]], axis=0)
    sh1, sc1 = mod_ref[0:1, :], mod_ref[1:2, :]
    h = _rms_mod(xe, n1g_ref[...], 1.0 + sc1, sh1).astype(BF16)
    proj = _dot(h, win_ref[...])
    main = proj[halo:halo + tm]
    bd = bd_ref[...]
    cos_t, sin_t = cos_ref[...], sin_ref[...]
    q = _head_norm(main[:, 0:ATTN_WIDTH], bd, qg_ref[...])
    q_out[...] = (_rope(q, cos_t, sin_t) * (HEAD_DIM ** -0.5 * LOG2_E)).astype(BF16)
    k = _head_norm(main[:, ATTN_WIDTH:ATTN_WIDTH + KV_WIDTH], bd[0:KV_WIDTH, 0:KV_WIDTH], kg_ref[...])
    k_out[...] = _rope(k, cos_t, sin_t).astype(BF16)
    v_out[...] = main[:, ATTN_WIDTH + KV_WIDTH:ATTN_WIDTH + 2 * KV_WIDTH].astype(BF16)
    u = proj[:, ATTN_WIDTH + 2 * KV_WIDTH:IN_WIDTH]
    t_loc = (lax.broadcasted_iota(jnp.int32, (tm + 2 * halo, LANES), 0)
             + (pl.program_id(1) * tm - halo))
    mixed = _pool_mix(u, t_loc, seq_len, wpool_ref, halo, tm) * pscale_ref[...]
    mixed_out[...] = mixed.astype(BF16)


def _lat_proj(layer, xs, mods, norm1_g, w_in, bd, qg, kg, cos_t, sin_t, w_pool, pool_scale):
    nb, seq_len, _ = xs.shape
    tm = LAT_TILE
    halo = POOL_HALO
    n_halo_blocks = seq_len // halo
    const2 = lambda b, i: (0, 0)
    tile = lambda b, i: (b, i, 0)
    return pl.pallas_call(
        functools.partial(_lat_proj_kernel, seq_len=seq_len),
        out_shape=(
            jax.ShapeDtypeStruct((nb, seq_len, ATTN_WIDTH), BF16),
            jax.ShapeDtypeStruct((nb, seq_len, KV_WIDTH), BF16),
            jax.ShapeDtypeStruct((nb, seq_len, KV_WIDTH), BF16),
            jax.ShapeDtypeStruct((nb, seq_len, POOL_WIDTH), BF16),
        ),
        grid=(nb, seq_len // tm),
        in_specs=[
            pl.BlockSpec((None, halo, D_MODEL),
                         lambda b, i: (b, jnp.maximum(i * (tm // halo) - 1, 0), 0)),
            pl.BlockSpec((None, tm, D_MODEL), tile),
            pl.BlockSpec((None, halo, D_MODEL),
                         lambda b, i: (b, jnp.minimum((i + 1) * (tm // halo), n_halo_blocks - 1), 0)),
            pl.BlockSpec((None, None, N_MOD, D_MODEL), lambda b, i: (layer, 1 + b, 0, 0)),
            pl.BlockSpec((1, D_MODEL), const2),
            pl.BlockSpec((None, D_MODEL, IN_WIDTH), lambda b, i: (layer, 0, 0)),
            pl.BlockSpec((ATTN_WIDTH, ATTN_WIDTH), const2),
            pl.BlockSpec((1, ATTN_WIDTH), const2),
            pl.BlockSpec((1, KV_WIDTH), const2),
            pl.BlockSpec((tm, LANES), lambda b, i: (i, 0)),
            pl.BlockSpec((tm, LANES), lambda b, i: (i, 0)),
            pl.BlockSpec((None, 4, POOL_GROUP_DIM, POOL_GROUP_DIM), lambda b, i: (layer, 0, 0, 0)),
            pl.BlockSpec((1, POOL_WIDTH), const2),
        ],
        out_specs=(
            pl.BlockSpec((None, tm, ATTN_WIDTH), tile),
            pl.BlockSpec((None, tm, KV_WIDTH), tile),
            pl.BlockSpec((None, tm, KV_WIDTH), tile),
            pl.BlockSpec((None, tm, POOL_WIDTH), tile),
        ),
        compiler_params=pltpu.CompilerParams(
            dimension_semantics=("arbitrary", "arbitrary"), vmem_limit_bytes=VMEM_LIMIT),
        name=f"lat_proj_{layer}",
    )(xs, xs, xs, mods, norm1_g, w_in, bd, qg, kg, cos_t, sin_t, w_pool, pool_scale)


def _lat_attn_kernel(q_ref, klat_ref, vlat_ref, ck_ref, cv_ref, mixed_ref, x_ref, mod_ref, wout_ref,
                     n2g_ref, x1_out, h2_out, kdup_scr, vt_scr):
    @pl.when(pl.program_id(1) == 0)
    def _():
        k_all = jnp.concatenate([ck_ref[...], klat_ref[...].astype(F32)], axis=0)
        v_all = jnp.concatenate([cv_ref[...], vlat_ref[...].astype(F32)], axis=0)
        k0, k1 = _dup_halves(k_all)
        kdup_scr[0] = k0
        kdup_scr[1] = k1
        for i, vt in enumerate(_value_rows(v_all)):
            vt_scr[i] = vt

    attn = _attend_keys_major(q_ref[...], kdup_scr, vt_scr)
    _mix_residual_norm2(x_ref[...], attn, mixed_ref[...], wout_ref, mod_ref, n2g_ref, x1_out, h2_out)


def _lat_attn(layer, q, k, v, cache_k, cache_v, mixed, xs, mods, w_out, norm2_g):
    nb, seq_len, _ = xs.shape
    past = cache_k.shape[2]
    tq = LAT_Q_TILE
    n_keys = past + seq_len
    tile = lambda b, j: (b, j, 0)
    whole = lambda b, j: (b, 0, 0)
    return pl.pallas_call(
        _lat_attn_kernel,
        out_shape=(
            jax.ShapeDtypeStruct((nb, seq_len, D_MODEL), F32),
            jax.ShapeDtypeStruct((nb, seq_len, D_MODEL), BF16),
        ),
        grid=(nb, seq_len // tq),
        in_specs=[
            pl.BlockSpec((None, tq, ATTN_WIDTH), tile),
            pl.BlockSpec((None, seq_len, KV_WIDTH), whole),
            pl.BlockSpec((None, seq_len, KV_WIDTH), whole),
            pl.BlockSpec((None, None, past, KV_WIDTH), lambda b, j: (b, layer, 0, 0)),
            pl.BlockSpec((None, None, past, KV_WIDTH), lambda b, j: (b, layer, 0, 0)),
            pl.BlockSpec((None, tq, POOL_WIDTH), tile),
            pl.BlockSpec((None, tq, D_MODEL), tile),
            pl.BlockSpec((None, None, N_MOD, D_MODEL), lambda b, j: (layer, 1 + b, 0, 0)),
            pl.BlockSpec((None, D_MODEL, D_MODEL), lambda b, j: (layer, 0, 0)),
            pl.BlockSpec((1, D_MODEL), lambda b, j: (0, 0)),
        ],
        out_specs=(
            pl.BlockSpec((None, tq, D_MODEL), tile),
            pl.BlockSpec((None, tq, D_MODEL), tile),
        ),
        scratch_shapes=[
            pltpu.VMEM((N_KV_HEADS, n_keys, LANES), BF16),
            pltpu.VMEM((2 * N_KV_HEADS, LANES, n_keys), BF16),
        ],
        compiler_params=pltpu.CompilerParams(
            dimension_semantics=("arbitrary", "arbitrary"), vmem_limit_bytes=VMEM_LIMIT),
        name=f"lat_attn_{layer}",
    )(q, k, v, cache_k, cache_v, mixed, xs, mods, w_out, norm2_g)


def _ffn_kernel(*refs, seq_len, final_norm, has_halo):
    if has_halo:
        (x1_ref, h2_ref, h2prev_ref, h2next_ref, mod_ref, wup_ref, cw_ref, cb_ref, wdn_ref, fg_ref,
         out_ref, z_scr, act_scr) = refs
    else:
        (x1_ref, h2_ref, mod_ref, wup_ref, cw_ref, cb_ref, wdn_ref, fg_ref,
         out_ref, z_scr, act_scr) = refs
    tm = x1_ref.shape[0]
    gap = F32_SUBLANES
    if has_halo:
        halo = BF16_SUBLANES
        tile_start = pl.program_id(1) * tm
        h2prev = jnp.where(tile_start % seq_len != 0, h2prev_ref[...], jnp.zeros_like(h2prev_ref))
        h2next = jnp.where((tile_start + tm) % seq_len != 0, h2next_ref[...], jnp.zeros_like(h2next_ref))
        he = jnp.concatenate([h2prev, h2_ref[...], h2next], axis=0)
        segments = [(halo, tm)]
    else:
        he = h2_ref[...]
        segments = [(gap + s * (seq_len + gap), seq_len) for s in range(tm // seq_len)]
        for slab in range(z_scr.shape[0]):
            for s in range(tm // seq_len + 1):
                r = s * (seq_len + gap)
                z_scr[slab, r:r + gap, :] = jnp.zeros((gap, LANES), F32)

    def store_z(slab, z):
        if has_halo:
            z_scr[slab, 0:z.shape[0], :] = z
        else:
            for s, (row0, n) in enumerate(segments):
                z_scr[slab, row0:row0 + n, :] = z[s * n:(s + 1) * n]

    def conv_rows(slab, row, n, col0):
        cols = slice(col0, col0 + LANES)
        return (z_scr[slab, pl.ds(row - 1, n, stride=1), :] * cw_ref[0:1, cols]
                + z_scr[slab, row:row + n, :] * cw_ref[1:2, cols]
                + z_scr[slab, pl.ds(row + 1, n, stride=1), :] * cw_ref[2:3, cols]
                + cb_ref[0:1, cols])

    n_sub = FFN_CHUNK // LANES
    for ci in range(D_FF // FFN_CHUNK):
        c0 = ci * FFN_CHUNK
        slab0 = 2 * n_sub * (ci % 2)
        za = _dot(he, wup_ref[:, c0:c0 + FFN_CHUNK])
        zg = _dot(he, wup_ref[:, D_FF + c0:D_FF + c0 + FFN_CHUNK])
        for j in range(n_sub):
            lanes = slice(j * LANES, (j + 1) * LANES)
            slab_a, slab_g = slab0 + 2 * j, slab0 + 2 * j + 1
            col = c0 + j * LANES
            store_z(slab_a, za[:, lanes])
            store_z(slab_g, zg[:, lanes])
            for s, (row0, n) in enumerate(segments):
                a = conv_rows(slab_a, row0, n, col)
                g = conv_rows(slab_g, row0, n, D_FF + col)
                act_scr[s * n:(s + 1) * n, col:col + LANES] = (jax.nn.silu(a) * g).astype(BF16)
    x2 = x1_ref[...] + mod_ref[5:6, :] * _dot(act_scr[...], wdn_ref[...])
    if final_norm:
        ms = jnp.mean(x2 * x2, axis=-1, keepdims=True)
        x2 = x2 * lax.rsqrt(ms + EPS) * fg_ref[...]
    out_ref[...] = x2


def _ffn(layer, x1, h2, seq_len, mod_row0, mods, w_up, conv_w, conv_b, w_down, final_g, final_norm):
    nb, n_tok, _ = x1.shape
    tm = FFN_TILE
    has_halo = seq_len > tm
    assert seq_len % tm == 0 or tm % seq_len == 0
    halo = BF16_SUBLANES
    gap = F32_SUBLANES
    n_halo_blocks = n_tok // halo
    tile = lambda b, i: (b, i, 0)
    halo_specs = [
        pl.BlockSpec((None, halo, D_MODEL),
                     lambda b, i: (b, jnp.maximum(i * (tm // halo) - 1, 0), 0)),
        pl.BlockSpec((None, halo, D_MODEL),
                     lambda b, i: (b, jnp.minimum((i + 1) * (tm // halo), n_halo_blocks - 1), 0)),
    ] if has_halo else []
    halo_args = [h2, h2] if has_halo else []
    z_rows = tm + 2 * halo if has_halo else gap + (tm // seq_len) * (seq_len + gap)
    return pl.pallas_call(
        functools.partial(_ffn_kernel, seq_len=seq_len, final_norm=final_norm, has_halo=has_halo),
        out_shape=jax.ShapeDtypeStruct((nb, n_tok, D_MODEL), F32),
        grid=(nb, n_tok // tm),
        in_specs=[
            pl.BlockSpec((None, tm, D_MODEL), tile),
            pl.BlockSpec((None, tm, D_MODEL), tile),
            *halo_specs,
            pl.BlockSpec((None, None, N_MOD, D_MODEL), lambda b, i: (layer, mod_row0 + b, 0, 0)),
            pl.BlockSpec((None, D_MODEL, 2 * D_FF), lambda b, i: (layer, 0, 0),
                         pipeline_mode=pl.Buffered(1)),
            pl.BlockSpec((None, 3, 2 * D_FF), lambda b, i: (layer, 0, 0)),
            pl.BlockSpec((None, 1, 2 * D_FF), lambda b, i: (layer, 0, 0)),
            pl.BlockSpec((None, D_FF, D_MODEL), lambda b, i: (layer, 0, 0),
                         pipeline_mode=pl.Buffered(1)),
            pl.BlockSpec((1, D_MODEL), lambda b, i: (0, 0)),
        ],
        out_specs=pl.BlockSpec((None, tm, D_MODEL), tile),
        scratch_shapes=[
            pltpu.VMEM((4 * (FFN_CHUNK // LANES), z_rows, LANES), F32),
            pltpu.VMEM((tm, D_FF), BF16),
        ],
        compiler_params=pltpu.CompilerParams(
            dimension_semantics=("arbitrary", "arbitrary"), vmem_limit_bytes=VMEM_LIMIT),
        name=f"ffn_{'lat' if nb > 1 else 'ctx'}_{layer}",
    )(x1, h2, *halo_args, mods, w_up, conv_w, conv_b, w_down, final_g)


def _rope_tables(n_tokens):
    t = jnp.arange(n_tokens)
    row = (t // GRID_W).astype(F32)
    col = (t % GRID_W).astype(F32)
    n_freq = HEAD_DIM // 4
    inv = ROPE_THETA ** (-jnp.arange(n_freq, dtype=F32) / n_freq)
    ang = jnp.stack([row[:, None] * inv, col[:, None] * inv], axis=1)
    cos = jnp.broadcast_to(jnp.cos(ang)[:, :, None, :], (n_tokens, 2, 2, n_freq))
    sin = jnp.sin(ang)[:, :, None, :] * jnp.array([-1.0, 1.0], F32)[None, None, :, None]
    cos = cos.reshape(n_tokens, HEAD_DIM)
    sin = sin.reshape(n_tokens, HEAD_DIM)
    return jnp.tile(cos, (1, 2)), jnp.tile(sin, (1, 2))


def kernel(x_prompt, x_sample, cache_k, cache_v, c, c_ctx, w_mod, b_mod, norm1_g, w_in, q_norm_g, k_norm_g, w_pool, pool_scale, w_out, norm2_g, w_up, conv_w, conv_b, w_down, final_norm_g):
    batch, seq, d = x_prompt.shape
    dec_batch, dec_seq, _ = x_sample.shape
    past = cache_k.shape[2]

    w_in_b = w_in.astype(BF16)
    w_pool_b = w_pool.astype(BF16)
    w_out_b = w_out.astype(BF16)
    w_up_b = w_up.astype(BF16)
    w_down_b = w_down.astype(BF16)

    cond8 = jnp.zeros((8, d), F32).at[0].set(c_ctx).at[1:1 + dec_batch].set(c)
    mods = _modulation(cond8, w_mod, b_mod).reshape(DEPTH, 8, N_MOD, d)

    head_id = jnp.arange(ATTN_WIDTH) // HEAD_DIM
    blockdiag = (head_id[:, None] == head_id[None, :]).astype(BF16)
    cos_t, sin_t = _rope_tables(dec_seq)
    ck = cache_k.reshape(dec_batch, DEPTH, past, KV_WIDTH)
    cv = cache_v.reshape(dec_batch, DEPTH, past, KV_WIDTH)
    final_g = final_norm_g.reshape(1, d)

    xp = x_prompt.reshape(batch * seq, d)
    xs = x_sample
    ks_out, vs_out = [], []
    for l in range(DEPTH):
        n1g = norm1_g[l].reshape(1, d)
        n2g = norm2_g[l].reshape(1, d)
        qg = jnp.tile(q_norm_g[l], N_HEADS).reshape(1, ATTN_WIDTH)
        kg = jnp.tile(k_norm_g[l], N_KV_HEADS).reshape(1, KV_WIDTH)
        ps = pool_scale[l].reshape(1, POOL_WIDTH)
        cb = conv_b.reshape(DEPTH, 1, 2 * D_FF)
        last = l == DEPTH - 1

        k_l, v_l, x1p, h2p = _ctx_front(l, xp, seq, mods, n1g, w_in_b, blockdiag, qg, kg,
                                        w_pool_b, ps, w_out_b, n2g)
        ks_out.append(k_l.reshape(batch, seq, KV_WIDTH))
        vs_out.append(v_l.reshape(batch, seq, KV_WIDTH))
        xp = _ffn(l, x1p[None], h2p[None], seq, 0, mods, w_up_b, conv_w, cb, w_down_b,
                  final_g, last)[0]

        q, k, v, mixed = _lat_proj(l, xs, mods, n1g, w_in_b, blockdiag, qg, kg, cos_t, sin_t,
                                   w_pool_b, ps)
        x1s, h2s = _lat_attn(l, q, k, v, ck, cv, mixed, xs, mods, w_out_b, n2g)
        xs = _ffn(l, x1s, h2s, dec_seq, 1, mods, w_up_b, conv_w, cb, w_down_b, final_g, last)

    y_prompt = xp.reshape(batch, seq, d)
    cache_shape = (batch, DEPTH, seq, N_KV_HEADS, HEAD_DIM)
    new_cache_k = jnp.stack(ks_out, axis=1).reshape(cache_shape)
    new_cache_v = jnp.stack(vs_out, axis=1).reshape(cache_shape)
    return (y_prompt, xs, new_cache_k, new_cache_v)
```

```python
import functools

import jax
import jax.numpy as jnp
from jax import lax
from jax.experimental import pallas as pl
from jax.experimental.pallas import tpu as pltpu

D_MODEL = 1024
DEPTH = 4
GRID_W = 64
HEAD_DIM = 64
N_HEADS = 8
N_KV_HEADS = 2
ATTN_WIDTH = N_HEADS * HEAD_DIM
KV_WIDTH = N_KV_HEADS * HEAD_DIM
POOL_WIDTH = D_MODEL - ATTN_WIDTH
POOL_WINDOWS = (2, 4, 8, 16)
POOL_GROUP_DIM = 128
IN_WIDTH = ATTN_WIDTH + 2 * KV_WIDTH + POOL_WIDTH
D_FF = 2816
ROPE_THETA = 10000.0
EPS = 1e-6
N_MOD = 6
LOG2_E = 1.4426950408889634

LANES = 128
F32_SUBLANES = 8
BF16_SUBLANES = 16
VMEM_LIMIT = 56 * 1024 * 1024

POOL_HALO = 8
FFN_CHUNK = 256
CTX_TILE = 512
LAT_TILE = 512
LAT_Q_TILE = 256
FFN_TILE = 512
SCORE_LOOKAHEAD = 2

BF16 = jnp.bfloat16
F32 = jnp.float32


def _dot(a, b):
    return jnp.dot(a, b, preferred_element_type=F32)


def _dot_nt(a, b):
    return lax.dot_general(a, b, (((1,), (1,)), ((), ())), preferred_element_type=F32)


def _rms_mod(x, g, scale1p, shift):
    ms = jnp.mean(x * x, axis=-1, keepdims=True)
    return (x * lax.rsqrt(ms + EPS) * g) * scale1p + shift


def _head_norm(t, blockdiag, g_tiled):
    ssq = _dot((t * t).astype(BF16), blockdiag)
    return t * lax.rsqrt(ssq * (1.0 / HEAD_DIM) + EPS) * g_tiled


def _rope(t, cos_t, sin_t):
    lane = lax.broadcasted_iota(jnp.int32, (t.shape[0], LANES), 1)
    first = (lane % 32) < 16
    outs = []
    for ci in range(t.shape[1] // LANES):
        tc = t[:, ci * LANES:(ci + 1) * LANES]
        partner = jnp.where(first, pltpu.roll(tc, LANES - 16, 1), pltpu.roll(tc, 16, 1))
        outs.append(tc * cos_t + partner * sin_t)
    return outs[0] if len(outs) == 1 else jnp.concatenate(outs, axis=1)


def _dup_halves(t):
    lane = lax.broadcasted_iota(jnp.int32, t.shape, 1)
    lo = lane < HEAD_DIM
    sw = pltpu.roll(t, HEAD_DIM, 1)
    return jnp.where(lo, t, sw).astype(BF16), jnp.where(lo, sw, t).astype(BF16)


def _value_rows(vt):
    ones = jnp.ones((HEAD_DIM, vt.shape[1]), F32)
    out = []
    for g in range(N_KV_HEADS):
        vg = vt[g * HEAD_DIM:(g + 1) * HEAD_DIM]
        out.append(jnp.concatenate([vg, ones], axis=0).astype(BF16))
        out.append(jnp.concatenate([ones, vg], axis=0).astype(BF16))
    return out


def _attend_keys_major(q, kdup_ref, vt_ref):
    tq = q.shape[0]
    lane = lax.broadcasted_iota(jnp.int32, (1, LANES), 1)
    masks = ((lane < HEAD_DIM).astype(BF16), (lane >= HEAD_DIM).astype(BF16))
    units = [(g, half) for g in range(N_KV_HEADS) for half in range(2)]

    def scores(g, half):
        q_a = q[:, (2 * g) * LANES:(2 * g + 1) * LANES]
        q_b = q[:, (2 * g + 1) * LANES:(2 * g + 2) * LANES]
        q_rows = jnp.concatenate([q_a * masks[half], q_b * masks[half]], axis=0)
        return _dot_nt(kdup_ref[g], q_rows)

    def values(g, half, st):
        pt = jnp.exp2(st - jnp.max(st, axis=0, keepdims=True)).astype(BF16)
        return _dot(vt_ref[2 * g + half], pt).T

    outs = []
    pending = [scores(*unit) for unit in units[:SCORE_LOOKAHEAD]]
    for i, unit in enumerate(units):
        if i + SCORE_LOOKAHEAD < len(units):
            pending.append(scores(*units[i + SCORE_LOOKAHEAD]))
        outs.append(values(*unit, pending.pop(0)))

    lo = lax.broadcasted_iota(jnp.int32, (2 * tq, LANES), 1) < HEAD_DIM
    blocks = []
    for g in range(N_KV_HEADS):
        o_even, o_odd = outs[2 * g], outs[2 * g + 1]
        num = jnp.where(lo, o_even, o_odd)
        den = jnp.where(lo, pltpu.roll(o_even, HEAD_DIM, 1), pltpu.roll(o_odd, HEAD_DIM, 1))
        out = num / den
        blocks.extend([out[0:tq], out[tq:2 * tq]])
    return jnp.concatenate(blocks, axis=1)


def _pool_mix(u, t_loc, seq_len, wpool_ref, lo_row, n_rows):
    m = u.shape[0]
    outs = []
    for gi, w in enumerate(POOL_WINDOWS):
        ug = u[:, gi * LANES:(gi + 1) * LANES]
        half = w // 2
        past, future, d = ug, ug, 1
        while d < half:
            past = past + jnp.where(t_loc >= d, pltpu.roll(past, d, 0), 0.0)
            future = future + jnp.where(t_loc + d < seq_len, pltpu.roll(future, m - d, 0), 0.0)
            d *= 2
        total = jnp.where(t_loc >= 1, pltpu.roll(past, 1, 0), 0.0) + future
        cnt = jnp.minimum(t_loc + half, seq_len) - jnp.maximum(t_loc - half, 0)
        pooled = total / cnt.astype(F32) - ug
        pooled = pooled[lo_row:lo_row + n_rows]
        outs.append(_dot(pooled.astype(BF16), wpool_ref[gi]))
    return jnp.concatenate(outs, axis=1)


def _mix_residual_norm2(x, attn, mixed, wout_ref, mod_ref, n2g_ref, x1_out, h2_out):
    gt1, sh2, sc2 = mod_ref[2:3, :], mod_ref[3:4, :], mod_ref[4:5, :]
    mix = (_dot(attn.astype(BF16), wout_ref[0:ATTN_WIDTH, :])
           + _dot(mixed.astype(BF16), wout_ref[ATTN_WIDTH:D_MODEL, :]))
    x1 = x + gt1 * mix
    x1_out[...] = x1
    h2_out[...] = _rms_mod(x1, n2g_ref[...], 1.0 + sc2, sh2).astype(BF16)


def _mod_kernel(cond_ref, w_ref, b_ref, out_ref):
    s = jax.nn.silu(cond_ref[...]).astype(BF16)
    out_ref[...] = _dot(s, w_ref[...].astype(BF16)) + b_ref[...]


def _modulation(cond8, w_mod, b_mod):
    tn = 1536
    n = N_MOD * D_MODEL
    return pl.pallas_call(
        _mod_kernel,
        out_shape=jax.ShapeDtypeStruct((DEPTH, 8, n), F32),
        grid=(DEPTH, n // tn),
        in_specs=[
            pl.BlockSpec((8, D_MODEL), lambda l, j: (0, 0)),
            pl.BlockSpec((None, D_MODEL, tn), lambda l, j: (l, 0, j)),
            pl.BlockSpec((None, 1, tn), lambda l, j: (l, 0, j)),
        ],
        out_specs=pl.BlockSpec((None, 8, tn), lambda l, j: (l, 0, j)),
        compiler_params=pltpu.CompilerParams(
            dimension_semantics=("arbitrary", "arbitrary"), vmem_limit_bytes=VMEM_LIMIT),
        name="modulation",
    )(cond8, w_mod, b_mod.reshape(DEPTH, 1, n))


def _ctx_front_kernel(x_ref, mod_ref, n1g_ref, win_ref, bd_ref, qg_ref, kg_ref, wpool_ref,
                      pscale_ref, wout_ref, n2g_ref, kt_out, vt_out, x1_out, h2_out, *, seq_len):
    x = x_ref[...]
    tm = x.shape[0]
    sh1, sc1 = mod_ref[0:1, :], mod_ref[1:2, :]
    h = _rms_mod(x, n1g_ref[...], 1.0 + sc1, sh1).astype(BF16)
    proj = _dot(h, win_ref[...])
    bd = bd_ref[...]
    q = (_head_norm(proj[:, 0:ATTN_WIDTH], bd, qg_ref[...]) * (HEAD_DIM ** -0.5 * LOG2_E)).astype(BF16)
    k = _head_norm(proj[:, ATTN_WIDTH:ATTN_WIDTH + KV_WIDTH], bd[0:KV_WIDTH, 0:KV_WIDTH], kg_ref[...])
    v = proj[:, ATTN_WIDTH + KV_WIDTH:ATTN_WIDTH + 2 * KV_WIDTH]
    u = proj[:, ATTN_WIDTH + 2 * KV_WIDTH:IN_WIDTH]

    seq_outs = []
    for s in range(tm // seq_len):
        r0 = s * seq_len
        kt_out[s] = k[r0:r0 + seq_len].T
        vt = v[r0:r0 + seq_len].T
        vt_out[s] = vt
        kdup = _dup_halves(k[r0:r0 + seq_len])
        seq_outs.append(_attend_keys_major(q[r0:r0 + seq_len], kdup, _value_rows(vt)))
    attn = jnp.concatenate(seq_outs, axis=0)

    t_loc = lax.broadcasted_iota(jnp.int32, (tm, LANES), 0) % seq_len
    mixed = _pool_mix(u, t_loc, seq_len, wpool_ref, 0, tm) * pscale_ref[...]
    _mix_residual_norm2(x, attn, mixed, wout_ref, mod_ref, n2g_ref, x1_out, h2_out)


def _ctx_front(layer, xp, seq_len, mods, norm1_g, w_in, bd, qg, kg, w_pool, pool_scale, w_out, norm2_g):
    n_tok = xp.shape[0]
    tm = CTX_TILE
    const2 = lambda i: (0, 0)
    tile = lambda i: (i, 0)
    return pl.pallas_call(
        functools.partial(_ctx_front_kernel, seq_len=seq_len),
        out_shape=(
            jax.ShapeDtypeStruct((n_tok // seq_len, KV_WIDTH, seq_len), F32),
            jax.ShapeDtypeStruct((n_tok // seq_len, KV_WIDTH, seq_len), F32),
            jax.ShapeDtypeStruct((n_tok, D_MODEL), F32),
            jax.ShapeDtypeStruct((n_tok, D_MODEL), BF16),
        ),
        grid=(n_tok // tm,),
        in_specs=[
            pl.BlockSpec((tm, D_MODEL), tile),
            pl.BlockSpec((None, None, N_MOD, D_MODEL), lambda i: (layer, 0, 0, 0)),
            pl.BlockSpec((1, D_MODEL), const2),
            pl.BlockSpec((None, D_MODEL, IN_WIDTH), lambda i: (layer, 0, 0)),
            pl.BlockSpec((ATTN_WIDTH, ATTN_WIDTH), const2),
            pl.BlockSpec((1, ATTN_WIDTH), const2),
            pl.BlockSpec((1, KV_WIDTH), const2),
            pl.BlockSpec((None, 4, POOL_GROUP_DIM, POOL_GROUP_DIM), lambda i: (layer, 0, 0, 0)),
            pl.BlockSpec((1, POOL_WIDTH), const2),
            pl.BlockSpec((None, D_MODEL, D_MODEL), lambda i: (layer, 0, 0)),
            pl.BlockSpec((1, D_MODEL), const2),
        ],
        out_specs=(
            pl.BlockSpec((tm // seq_len, KV_WIDTH, seq_len), lambda i: (i, 0, 0)),
            pl.BlockSpec((tm // seq_len, KV_WIDTH, seq_len), lambda i: (i, 0, 0)),
            pl.BlockSpec((tm, D_MODEL), tile),
            pl.BlockSpec((tm, D_MODEL), tile),
        ),
        compiler_params=pltpu.CompilerParams(
            dimension_semantics=("arbitrary",), vmem_limit_bytes=VMEM_LIMIT),
        name=f"ctx_front_{layer}",
    )(xp, mods, norm1_g, w_in, bd, qg, kg, w_pool, pool_scale, w_out, norm2_g)


def _lat_proj_kernel(xprev_ref, x_ref, xnext_ref, mod_ref, n1g_ref, win_ref, bd_ref, qg_ref, kg_ref,
                     cos_ref, sin_ref, wpool_ref, pscale_ref, q_out, k_out, v_out, mixed_out, *, seq_len):
    tm = x_ref.shape[0]
    halo = POOL_HALO
    xe = jnp.concatenate([xprev_ref[...], x_ref[...], xnext_ref[...]], axis=0)
    sh1, sc1 = mod_ref[0:1, :], mod_ref[1:2, :]
    h = _rms_mod(xe, n1g_ref[...], 1.0 + sc1, sh1).astype(BF16)
    proj = _dot(h, win_ref[...])
    main = proj[halo:halo + tm]
    bd = bd_ref[...]
    cos_t, sin_t = cos_ref[...], sin_ref[...]
    q = _head_norm(main[:, 0:ATTN_WIDTH], bd, qg_ref[...])
    q_out[...] = (_rope(q, cos_t, sin_t) * (HEAD_DIM ** -0.5 * LOG2_E)).astype(BF16)
    k = _head_norm(main[:, ATTN_WIDTH:ATTN_WIDTH + KV_WIDTH], bd[0:KV_WIDTH, 0:KV_WIDTH], kg_ref[...])
    k_out[...] = _rope(k, cos_t, sin_t).astype(BF16)
    v_out[...] = main[:, ATTN_WIDTH + KV_WIDTH:ATTN_WIDTH + 2 * KV_WIDTH].astype(BF16)
    u = proj[:, ATTN_WIDTH + 2 * KV_WIDTH:IN_WIDTH]
    t_loc = (lax.broadcasted_iota(jnp.int32, (tm + 2 * halo, LANES), 0)
             + (pl.program_id(1) * tm - halo))
    mixed = _pool_mix(u, t_loc, seq_len, wpool_ref, halo, tm) * pscale_ref[...]
    mixed_out[...] = mixed.astype(BF16)


def _lat_proj(layer, xs, mods, norm1_g, w_in, bd, qg, kg, cos_t, sin_t, w_pool, pool_scale):
    nb, seq_len, _ = xs.shape
    tm = LAT_TILE
    halo = POOL_HALO
    n_halo_blocks = seq_len // halo
    const2 = lambda b, i: (0, 0)
    tile = lambda b, i: (b, i, 0)
    return pl.pallas_call(
        functools.partial(_lat_proj_kernel, seq_len=seq_len),
        out_shape=(
            jax.ShapeDtypeStruct((nb, seq_len, ATTN_WIDTH), BF16),
            jax.ShapeDtypeStruct((nb, seq_len, KV_WIDTH), BF16),
            jax.ShapeDtypeStruct((nb, seq_len, KV_WIDTH), BF16),
            jax.ShapeDtypeStruct((nb, seq_len, POOL_WIDTH), BF16),
        ),
        grid=(nb, seq_len // tm),
        in_specs=[
            pl.BlockSpec((None, halo, D_MODEL),
                         lambda b, i: (b, jnp.maximum(i * (tm // halo) - 1, 0), 0)),
            pl.BlockSpec((None, tm, D_MODEL), tile),
            pl.BlockSpec((None, halo, D_MODEL),
                         lambda b, i: (b, jnp.minimum((i + 1) * (tm // halo), n_halo_blocks - 1), 0)),
            pl.BlockSpec((None, None, N_MOD, D_MODEL), lambda b, i: (layer, 1 + b, 0, 0)),
            pl.BlockSpec((1, D_MODEL), const2),
            pl.BlockSpec((None, D_MODEL, IN_WIDTH), lambda b, i: (layer, 0, 0)),
            pl.BlockSpec((ATTN_WIDTH, ATTN_WIDTH), const2),
            pl.BlockSpec((1, ATTN_WIDTH), const2),
            pl.BlockSpec((1, KV_WIDTH), const2),
            pl.BlockSpec((tm, LANES), lambda b, i: (i, 0)),
            pl.BlockSpec((tm, LANES), lambda b, i: (i, 0)),
            pl.BlockSpec((None, 4, POOL_GROUP_DIM, POOL_GROUP_DIM), lambda b, i: (layer, 0, 0, 0)),
            pl.BlockSpec((1, POOL_WIDTH), const2),
        ],
        out_specs=(
            pl.BlockSpec((None, tm, ATTN_WIDTH), tile),
            pl.BlockSpec((None, tm, KV_WIDTH), tile),
            pl.BlockSpec((None, tm, KV_WIDTH), tile),
            pl.BlockSpec((None, tm, POOL_WIDTH), tile),
        ),
        compiler_params=pltpu.CompilerParams(
            dimension_semantics=("arbitrary", "arbitrary"), vmem_limit_bytes=VMEM_LIMIT),
        name=f"lat_proj_{layer}",
    )(xs, xs, xs, mods, norm1_g, w_in, bd, qg, kg, cos_t, sin_t, w_pool, pool_scale)


def _lat_attn_kernel(q_ref, klat_ref, vlat_ref, ck_ref, cv_ref, mixed_ref, x_ref, mod_ref, wout_ref,
                     n2g_ref, x1_out, h2_out, kdup_scr, vt_scr):
    @pl.when(pl.program_id(1) == 0)
    def _():
        k_all = jnp.concatenate([ck_ref[...].T, klat_ref[...].astype(F32)], axis=0)
        vt_all = jnp.concatenate([cv_ref[...], vlat_ref[...].astype(F32).T], axis=1)
        k0, k1 = _dup_halves(k_all)
        kdup_scr[0] = k0
        kdup_scr[1] = k1
        for i, vt in enumerate(_value_rows(vt_all)):
            vt_scr[i] = vt

    attn = _attend_keys_major(q_ref[...], kdup_scr, vt_scr)
    _mix_residual_norm2(x_ref[...], attn, mixed_ref[...], wout_ref, mod_ref, n2g_ref, x1_out, h2_out)


def _lat_attn(layer, q, k, v, cache_k, cache_v, mixed, xs, mods, w_out, norm2_g):
    nb, seq_len, _ = xs.shape
    past = cache_k.shape[3]
    tq = LAT_Q_TILE
    n_keys = past + seq_len
    tile = lambda b, j: (b, j, 0)
    whole = lambda b, j: (b, 0, 0)
    return pl.pallas_call(
        _lat_attn_kernel,
        out_shape=(
            jax.ShapeDtypeStruct((nb, seq_len, D_MODEL), F32),
            jax.ShapeDtypeStruct((nb, seq_len, D_MODEL), BF16),
        ),
        grid=(nb, seq_len // tq),
        in_specs=[
            pl.BlockSpec((None, tq, ATTN_WIDTH), tile),
            pl.BlockSpec((None, seq_len, KV_WIDTH), whole),
            pl.BlockSpec((None, seq_len, KV_WIDTH), whole),
            pl.BlockSpec((None, None, KV_WIDTH, past), lambda b, j: (b, layer, 0, 0)),
            pl.BlockSpec((None, None, KV_WIDTH, past), lambda b, j: (b, layer, 0, 0)),
            pl.BlockSpec((None, tq, POOL_WIDTH), tile),
            pl.BlockSpec((None, tq, D_MODEL), tile),
            pl.BlockSpec((None, None, N_MOD, D_MODEL), lambda b, j: (layer, 1 + b, 0, 0)),
            pl.BlockSpec((None, D_MODEL, D_MODEL), lambda b, j: (layer, 0, 0)),
            pl.BlockSpec((1, D_MODEL), lambda b, j: (0, 0)),
        ],
        out_specs=(
            pl.BlockSpec((None, tq, D_MODEL), tile),
            pl.BlockSpec((None, tq, D_MODEL), tile),
        ),
        scratch_shapes=[
            pltpu.VMEM((N_KV_HEADS, n_keys, LANES), BF16),
            pltpu.VMEM((2 * N_KV_HEADS, LANES, n_keys), BF16),
        ],
        compiler_params=pltpu.CompilerParams(
            dimension_semantics=("arbitrary", "arbitrary"), vmem_limit_bytes=VMEM_LIMIT),
        name=f"lat_attn_{layer}",
    )(q, k, v, cache_k, cache_v, mixed, xs, mods, w_out, norm2_g)


def _ffn_kernel(*refs, seq_len, final_norm, has_halo):
    if has_halo:
        (x1_ref, h2_ref, h2prev_ref, h2next_ref, mod_ref, wup_ref, cw_ref, cb_ref, wdn_ref, fg_ref,
         out_ref, z_scr, act_scr) = refs
    else:
        (x1_ref, h2_ref, mod_ref, wup_ref, cw_ref, cb_ref, wdn_ref, fg_ref,
         out_ref, z_scr, act_scr) = refs
    tm = x1_ref.shape[0]
    gap = F32_SUBLANES
    if has_halo:
        halo = BF16_SUBLANES
        tile_start = pl.program_id(1) * tm
        h2prev = jnp.where(tile_start % seq_len != 0, h2prev_ref[...], jnp.zeros_like(h2prev_ref))
        h2next = jnp.where((tile_start + tm) % seq_len != 0, h2next_ref[...], jnp.zeros_like(h2next_ref))
        he = jnp.concatenate([h2prev, h2_ref[...], h2next], axis=0)
        segments = [(halo, tm)]
    else:
        he = h2_ref[...]
        segments = [(gap + s * (seq_len + gap), seq_len) for s in range(tm // seq_len)]
        for slab in range(z_scr.shape[0]):
            for s in range(tm // seq_len + 1):
                r = s * (seq_len + gap)
                z_scr[slab, r:r + gap, :] = jnp.zeros((gap, LANES), F32)

    def store_z(slab, z):
        if has_halo:
            z_scr[slab, 0:z.shape[0], :] = z
        else:
            for s, (row0, n) in enumerate(segments):
                z_scr[slab, row0:row0 + n, :] = z[s * n:(s + 1) * n]

    def conv_rows(slab, row, n, col0):
        cols = slice(col0, col0 + LANES)
        return (z_scr[slab, pl.ds(row - 1, n, stride=1), :] * cw_ref[0:1, cols]
                + z_scr[slab, row:row + n, :] * cw_ref[1:2, cols]
                + z_scr[slab, pl.ds(row + 1, n, stride=1), :] * cw_ref[2:3, cols]
                + cb_ref[0:1, cols])

    n_sub = FFN_CHUNK // LANES
    for ci in range(D_FF // FFN_CHUNK):
        c0 = ci * FFN_CHUNK
        slab0 = 2 * n_sub * (ci % 2)
        za = _dot(he, wup_ref[:, c0:c0 + FFN_CHUNK])
        zg = _dot(he, wup_ref[:, D_FF + c0:D_FF + c0 + FFN_CHUNK])
        for j in range(n_sub):
            lanes = slice(j * LANES, (j + 1) * LANES)
            slab_a, slab_g = slab0 + 2 * j, slab0 + 2 * j + 1
            col = c0 + j * LANES
            store_z(slab_a, za[:, lanes])
            store_z(slab_g, zg[:, lanes])
            for s, (row0, n) in enumerate(segments):
                a = conv_rows(slab_a, row0, n, col)
                g = conv_rows(slab_g, row0, n, D_FF + col)
                act_scr[s * n:(s + 1) * n, col:col + LANES] = (jax.nn.silu(a) * g).astype(BF16)
    x2 = x1_ref[...] + mod_ref[5:6, :] * _dot(act_scr[...], wdn_ref[...])
    if final_norm:
        ms = jnp.mean(x2 * x2, axis=-1, keepdims=True)
        x2 = x2 * lax.rsqrt(ms + EPS) * fg_ref[...]
    out_ref[...] = x2


def _ffn(layer, x1, h2, seq_len, mod_row0, mods, w_up, conv_w, conv_b, w_down, final_g, final_norm):
    nb, n_tok, _ = x1.shape
    tm = FFN_TILE
    has_halo = seq_len > tm
    assert seq_len % tm == 0 or tm % seq_len == 0
    halo = BF16_SUBLANES
    gap = F32_SUBLANES
    n_halo_blocks = n_tok // halo
    tile = lambda b, i: (b, i, 0)
    halo_specs = [
        pl.BlockSpec((None, halo, D_MODEL),
                     lambda b, i: (b, jnp.maximum(i * (tm // halo) - 1, 0), 0)),
        pl.BlockSpec((None, halo, D_MODEL),
                     lambda b, i: (b, jnp.minimum((i + 1) * (tm // halo), n_halo_blocks - 1), 0)),
    ] if has_halo else []
    halo_args = [h2, h2] if has_halo else []
    z_rows = tm + 2 * halo if has_halo else gap + (tm // seq_len) * (seq_len + gap)
    return pl.pallas_call(
        functools.partial(_ffn_kernel, seq_len=seq_len, final_norm=final_norm, has_halo=has_halo),
        out_shape=jax.ShapeDtypeStruct((nb, n_tok, D_MODEL), F32),
        grid=(nb, n_tok // tm),
        in_specs=[
            pl.BlockSpec((None, tm, D_MODEL), tile),
            pl.BlockSpec((None, tm, D_MODEL), tile),
            *halo_specs,
            pl.BlockSpec((None, None, N_MOD, D_MODEL), lambda b, i: (layer, mod_row0 + b, 0, 0)),
            pl.BlockSpec((None, D_MODEL, 2 * D_FF), lambda b, i: (layer, 0, 0),
                         pipeline_mode=pl.Buffered(1)),
            pl.BlockSpec((None, 3, 2 * D_FF), lambda b, i: (layer, 0, 0)),
            pl.BlockSpec((None, 1, 2 * D_FF), lambda b, i: (layer, 0, 0)),
            pl.BlockSpec((None, D_FF, D_MODEL), lambda b, i: (layer, 0, 0),
                         pipeline_mode=pl.Buffered(1)),
            pl.BlockSpec((1, D_MODEL), lambda b, i: (0, 0)),
        ],
        out_specs=pl.BlockSpec((None, tm, D_MODEL), tile),
        scratch_shapes=[
            pltpu.VMEM((4 * (FFN_CHUNK // LANES), z_rows, LANES), F32),
            pltpu.VMEM((tm, D_FF), BF16),
        ],
        compiler_params=pltpu.CompilerParams(
            dimension_semantics=("arbitrary", "arbitrary"), vmem_limit_bytes=VMEM_LIMIT),
        name=f"ffn_{'lat' if nb > 1 else 'ctx'}_{layer}",
    )(x1, h2, *halo_args, mods, w_up, conv_w, conv_b, w_down, final_g)


def _rope_tables(n_tokens):
    t = jnp.arange(n_tokens)
    row = (t // GRID_W).astype(F32)
    col = (t % GRID_W).astype(F32)
    n_freq = HEAD_DIM // 4
    inv = ROPE_THETA ** (-jnp.arange(n_freq, dtype=F32) / n_freq)
    ang = jnp.stack([row[:, None] * inv, col[:, None] * inv], axis=1)
    cos = jnp.broadcast_to(jnp.cos(ang)[:, :, None, :], (n_tokens, 2, 2, n_freq))
    sin = jnp.sin(ang)[:, :, None, :] * jnp.array([-1.0, 1.0], F32)[None, None, :, None]
    cos = cos.reshape(n_tokens, HEAD_DIM)
    sin = sin.reshape(n_tokens, HEAD_DIM)
    return jnp.tile(cos, (1, 2)), jnp.tile(sin, (1, 2))


def kernel(x_prompt, x_sample, cache_k, cache_v, c, c_ctx, w_mod, b_mod, norm1_g, w_in, q_norm_g, k_norm_g, w_pool, pool_scale, w_out, norm2_g, w_up, conv_w, conv_b, w_down, final_norm_g):
    batch, seq, d = x_prompt.shape
    dec_batch, dec_seq, _ = x_sample.shape
    past = cache_k.shape[2]

    w_in_b = w_in.astype(BF16)
    w_pool_b = w_pool.astype(BF16)
    w_out_b = w_out.astype(BF16)
    w_up_b = w_up.astype(BF16)
    w_down_b = w_down.astype(BF16)

    cond8 = jnp.zeros((8, d), F32).at[0].set(c_ctx).at[1:1 + dec_batch].set(c)
    mods = _modulation(cond8, w_mod, b_mod).reshape(DEPTH, 8, N_MOD, d)

    head_id = jnp.arange(ATTN_WIDTH) // HEAD_DIM
    blockdiag = (head_id[:, None] == head_id[None, :]).astype(BF16)
    cos_t, sin_t = _rope_tables(dec_seq)
    ck = jnp.transpose(cache_k, (0, 1, 3, 4, 2)).reshape(dec_batch, DEPTH, KV_WIDTH, past)
    cv = jnp.transpose(cache_v, (0, 1, 3, 4, 2)).reshape(dec_batch, DEPTH, KV_WIDTH, past)
    final_g = final_norm_g.reshape(1, d)

    xp = x_prompt.reshape(batch * seq, d)
    xs = x_sample
    ks_out, vs_out = [], []
    for l in range(DEPTH):
        n1g = norm1_g[l].reshape(1, d)
        n2g = norm2_g[l].reshape(1, d)
        qg = jnp.tile(q_norm_g[l], N_HEADS).reshape(1, ATTN_WIDTH)
        kg = jnp.tile(k_norm_g[l], N_KV_HEADS).reshape(1, KV_WIDTH)
        ps = pool_scale[l].reshape(1, POOL_WIDTH)
        cb = conv_b.reshape(DEPTH, 1, 2 * D_FF)
        last = l == DEPTH - 1

        k_l, v_l, x1p, h2p = _ctx_front(l, xp, seq, mods, n1g, w_in_b, blockdiag, qg, kg,
                                        w_pool_b, ps, w_out_b, n2g)
        ks_out.append(k_l)
        vs_out.append(v_l)
        xp = _ffn(l, x1p[None], h2p[None], seq, 0, mods, w_up_b, conv_w, cb, w_down_b,
                  final_g, last)[0]

        q, k, v, mixed = _lat_proj(l, xs, mods, n1g, w_in_b, blockdiag, qg, kg, cos_t, sin_t,
                                   w_pool_b, ps)
        x1s, h2s = _lat_attn(l, q, k, v, ck, cv, mixed, xs, mods, w_out_b, n2g)
        xs = _ffn(l, x1s, h2s, dec_seq, 1, mods, w_up_b, conv_w, cb, w_down_b, final_g, last)

    y_prompt = xp.reshape(batch, seq, d)

    def cache_layout(per_layer):
        t = jnp.stack(per_layer, axis=1).reshape(batch, DEPTH, N_KV_HEADS, HEAD_DIM, seq)
        return jnp.transpose(t, (0, 1, 4, 2, 3))

    new_cache_k = cache_layout(ks_out)
    new_cache_v = cache_layout(vs_out)
    return (y_prompt, xs, new_cache_k, new_cache_v)
```

```python
import functools

import jax
import jax.numpy as jnp
from jax import lax
from jax.experimental import pallas as pl
from jax.experimental.pallas import tpu as pltpu

D_MODEL = 1024
DEPTH = 4
GRID_W = 64
HEAD_DIM = 64
N_HEADS = 8
N_KV_HEADS = 2
ATTN_WIDTH = N_HEADS * HEAD_DIM
KV_WIDTH = N_KV_HEADS * HEAD_DIM
POOL_WIDTH = D_MODEL - ATTN_WIDTH
POOL_WINDOWS = (2, 4, 8, 16)
POOL_GROUP_DIM = 128
IN_WIDTH = ATTN_WIDTH + 2 * KV_WIDTH + POOL_WIDTH
D_FF = 2816
ROPE_THETA = 10000.0
EPS = 1e-6
N_MOD = 6
LOG2_E = 1.4426950408889634

LANES = 128
F32_SUBLANES = 8
BF16_SUBLANES = 16
VMEM_LIMIT = 56 * 1024 * 1024

POOL_HALO = 8
FFN_CHUNK = 256
CTX_TILE = 512
LAT_TILE = 512
LAT_Q_TILE = 256
FFN_TILE = 512
SCORE_LOOKAHEAD = 2

BF16 = jnp.bfloat16
F32 = jnp.float32


def _dot(a, b):
    return jnp.dot(a, b, preferred_element_type=F32)


def _dot_nt(a, b):
    return lax.dot_general(a, b, (((1,), (1,)), ((), ())), preferred_element_type=F32)


def _rms_mod(x, g, scale1p, shift):
    ms = jnp.mean(x * x, axis=-1, keepdims=True)
    return (x * lax.rsqrt(ms + EPS) * g) * scale1p + shift


def _head_norm(t, blockdiag, g_tiled):
    ssq = _dot((t * t).astype(BF16), blockdiag)
    return t * lax.rsqrt(ssq * (1.0 / HEAD_DIM) + EPS) * g_tiled


def _rope(t, cos_t, sin_t):
    lane = lax.broadcasted_iota(jnp.int32, (t.shape[0], LANES), 1)
    first = (lane % 32) < 16
    outs = []
    for ci in range(t.shape[1] // LANES):
        tc = t[:, ci * LANES:(ci + 1) * LANES]
        partner = jnp.where(first, pltpu.roll(tc, LANES - 16, 1), pltpu.roll(tc, 16, 1))
        outs.append(tc * cos_t + partner * sin_t)
    return outs[0] if len(outs) == 1 else jnp.concatenate(outs, axis=1)


def _dup_halves(t):
    lane = lax.broadcasted_iota(jnp.int32, t.shape, 1)
    lo = lane < HEAD_DIM
    sw = pltpu.roll(t, HEAD_DIM, 1)
    return jnp.where(lo, t, sw).astype(BF16), jnp.where(lo, sw, t).astype(BF16)


def _value_rows(vt):
    ones = jnp.ones((HEAD_DIM, vt.shape[1]), F32)
    out = []
    for g in range(N_KV_HEADS):
        vg = vt[g * HEAD_DIM:(g + 1) * HEAD_DIM]
        out.append(jnp.concatenate([vg, ones], axis=0).astype(BF16))
        out.append(jnp.concatenate([ones, vg], axis=0).astype(BF16))
    return out


def _attend_keys_major(q, kdup_ref, vt_ref):
    tq = q.shape[0]
    lane = lax.broadcasted_iota(jnp.int32, (1, LANES), 1)
    masks = ((lane < HEAD_DIM).astype(BF16), (lane >= HEAD_DIM).astype(BF16))
    units = [(g, half) for g in range(N_KV_HEADS) for half in range(2)]

    def scores(g, half):
        q_a = q[:, (2 * g) * LANES:(2 * g + 1) * LANES]
        q_b = q[:, (2 * g + 1) * LANES:(2 * g + 2) * LANES]
        q_rows = jnp.concatenate([q_a * masks[half], q_b * masks[half]], axis=0)
        return _dot_nt(kdup_ref[g], q_rows)

    def values(g, half, st):
        pt = jnp.exp2(st - jnp.max(st, axis=0, keepdims=True)).astype(BF16)
        return _dot(vt_ref[2 * g + half], pt).T

    outs = []
    pending = [scores(*unit) for unit in units[:SCORE_LOOKAHEAD]]
    for i, unit in enumerate(units):
        if i + SCORE_LOOKAHEAD < len(units):
            pending.append(scores(*units[i + SCORE_LOOKAHEAD]))
        outs.append(values(*unit, pending.pop(0)))

    lo = lax.broadcasted_iota(jnp.int32, (2 * tq, LANES), 1) < HEAD_DIM
    blocks = []
    for g in range(N_KV_HEADS):
        o_even, o_odd = outs[2 * g], outs[2 * g + 1]
        num = jnp.where(lo, o_even, o_odd)
        den = jnp.where(lo, pltpu.roll(o_even, HEAD_DIM, 1), pltpu.roll(o_odd, HEAD_DIM, 1))
        out = num / den
        blocks.extend([out[0:tq], out[tq:2 * tq]])
    return jnp.concatenate(blocks, axis=1)


def _pool_mix(u, t_loc, seq_len, wpool_ref, lo_row, n_rows):
    m = u.shape[0]
    outs = []
    for gi, w in enumerate(POOL_WINDOWS):
        ug = u[:, gi * LANES:(gi + 1) * LANES]
        half = w // 2
        past, future, d = ug, ug, 1
        while d < half:
            past = past + jnp.where(t_loc >= d, pltpu.roll(past, d, 0), 0.0)
            future = future + jnp.where(t_loc + d < seq_len, pltpu.roll(future, m - d, 0), 0.0)
            d *= 2
        total = jnp.where(t_loc >= 1, pltpu.roll(past, 1, 0), 0.0) + future
        cnt = jnp.minimum(t_loc + half, seq_len) - jnp.maximum(t_loc - half, 0)
        pooled = total / cnt.astype(F32) - ug
        pooled = pooled[lo_row:lo_row + n_rows]
        outs.append(_dot(pooled.astype(BF16), wpool_ref[gi]))
    return jnp.concatenate(outs, axis=1)


def _mix_residual_norm2(x, attn, mixed, wout_ref, mod_ref, n2g_ref, x1_out, h2_out):
    gt1, sh2, sc2 = mod_ref[2:3, :], mod_ref[3:4, :], mod_ref[4:5, :]
    mix = (_dot(attn.astype(BF16), wout_ref[0:ATTN_WIDTH, :])
           + _dot(mixed.astype(BF16), wout_ref[ATTN_WIDTH:D_MODEL, :]))
    x1 = x + gt1 * mix
    x1_out[...] = x1
    h2_out[...] = _rms_mod(x1, n2g_ref[...], 1.0 + sc2, sh2).astype(BF16)


def _cast_specs(weights, layer, n_steps, step_index):
    in_specs, out_specs, out_shapes = [], [], []
    for w in weights:
        _, rows, cols = w.shape
        blk = rows // n_steps
        assert blk * n_steps == rows and blk % BF16_SUBLANES == 0
        in_specs.append(pl.BlockSpec((None, blk, cols), lambda *g: (layer, step_index(*g), 0)))
        out_specs.append(pl.BlockSpec((blk, cols), lambda *g: (step_index(*g), 0)))
        out_shapes.append(jax.ShapeDtypeStruct((rows, cols), BF16))
    return in_specs, out_specs, out_shapes


def _cast_blocks(src_refs, dst_refs):
    for src, dst in zip(src_refs, dst_refs, strict=True):
        dst[...] = src[...].astype(BF16)


def _mod_kernel(cond_ref, w_ref, b_ref, out_ref):
    s = jax.nn.silu(cond_ref[...]).astype(BF16)
    out_ref[...] = _dot(s, w_ref[...].astype(BF16)) + b_ref[...]


def _modulation(cond8, w_mod, b_mod):
    tn = 1536
    n = N_MOD * D_MODEL
    return pl.pallas_call(
        _mod_kernel,
        out_shape=jax.ShapeDtypeStruct((DEPTH, 8, n), F32),
        grid=(DEPTH, n // tn),
        in_specs=[
            pl.BlockSpec((8, D_MODEL), lambda l, j: (0, 0)),
            pl.BlockSpec((None, D_MODEL, tn), lambda l, j: (l, 0, j)),
            pl.BlockSpec((None, 1, tn), lambda l, j: (l, 0, j)),
        ],
        out_specs=pl.BlockSpec((None, 8, tn), lambda l, j: (l, 0, j)),
        compiler_params=pltpu.CompilerParams(
            dimension_semantics=("arbitrary", "arbitrary"), vmem_limit_bytes=VMEM_LIMIT),
        name="modulation",
    )(cond8, w_mod, b_mod.reshape(DEPTH, 1, n))


def _ctx_front_kernel(*refs, seq_len, n_cast):
    (x_ref, mod_ref, n1g_ref, win_ref, bd_ref, qg_ref, kg_ref, wpool_ref, pscale_ref, wout_ref,
     n2g_ref) = refs[:11]
    cast_src = refs[11:11 + n_cast]
    kt_out, vt_out, x1_out, h2_out = refs[11 + n_cast:15 + n_cast]
    _cast_blocks(cast_src, refs[15 + n_cast:])
    x = x_ref[...]
    tm = x.shape[0]
    sh1, sc1 = mod_ref[0:1, :], mod_ref[1:2, :]
    h = _rms_mod(x, n1g_ref[...], 1.0 + sc1, sh1).astype(BF16)
    proj = _dot(h, win_ref[...])
    bd = bd_ref[...]
    q = (_head_norm(proj[:, 0:ATTN_WIDTH], bd, qg_ref[...]) * (HEAD_DIM ** -0.5 * LOG2_E)).astype(BF16)
    k = _head_norm(proj[:, ATTN_WIDTH:ATTN_WIDTH + KV_WIDTH], bd[0:KV_WIDTH, 0:KV_WIDTH], kg_ref[...])
    v = proj[:, ATTN_WIDTH + KV_WIDTH:ATTN_WIDTH + 2 * KV_WIDTH]
    u = proj[:, ATTN_WIDTH + 2 * KV_WIDTH:IN_WIDTH]

    seq_outs = []
    for s in range(tm // seq_len):
        r0 = s * seq_len
        kt_out[s] = k[r0:r0 + seq_len].T
        vt = v[r0:r0 + seq_len].T
        vt_out[s] = vt
        kdup = _dup_halves(k[r0:r0 + seq_len])
        seq_outs.append(_attend_keys_major(q[r0:r0 + seq_len], kdup, _value_rows(vt)))
    attn = jnp.concatenate(seq_outs, axis=0)

    t_loc = lax.broadcasted_iota(jnp.int32, (tm, LANES), 0) % seq_len
    mixed = _pool_mix(u, t_loc, seq_len, wpool_ref, 0, tm) * pscale_ref[...]
    _mix_residual_norm2(x, attn, mixed, wout_ref, mod_ref, n2g_ref, x1_out, h2_out)


def _ctx_front(layer, xp, seq_len, mods, norm1_g, w_in, bd, qg, kg, w_pool, pool_scale, w_out, norm2_g,
               cast_weights):
    n_tok = xp.shape[0]
    tm = CTX_TILE
    const2 = lambda i: (0, 0)
    tile = lambda i: (i, 0)
    cast_in, cast_out, cast_shapes = _cast_specs(cast_weights, layer, n_tok // tm, lambda i: i)
    return pl.pallas_call(
        functools.partial(_ctx_front_kernel, seq_len=seq_len, n_cast=len(cast_weights)),
        out_shape=(
            jax.ShapeDtypeStruct((n_tok // seq_len, KV_WIDTH, seq_len), F32),
            jax.ShapeDtypeStruct((n_tok // seq_len, KV_WIDTH, seq_len), F32),
            jax.ShapeDtypeStruct((n_tok, D_MODEL), F32),
            jax.ShapeDtypeStruct((n_tok, D_MODEL), BF16),
            *cast_shapes,
        ),
        grid=(n_tok // tm,),
        in_specs=[
            pl.BlockSpec((tm, D_MODEL), tile),
            pl.BlockSpec((None, None, N_MOD, D_MODEL), lambda i: (layer, 0, 0, 0)),
            pl.BlockSpec((1, D_MODEL), const2),
            pl.BlockSpec((D_MODEL, IN_WIDTH), const2),
            pl.BlockSpec((ATTN_WIDTH, ATTN_WIDTH), const2),
            pl.BlockSpec((1, ATTN_WIDTH), const2),
            pl.BlockSpec((1, KV_WIDTH), const2),
            pl.BlockSpec((None, 4, POOL_GROUP_DIM, POOL_GROUP_DIM), lambda i: (layer, 0, 0, 0)),
            pl.BlockSpec((1, POOL_WIDTH), const2),
            pl.BlockSpec((D_MODEL, D_MODEL), const2),
            pl.BlockSpec((1, D_MODEL), const2),
            *cast_in,
        ],
        out_specs=(
            pl.BlockSpec((tm // seq_len, KV_WIDTH, seq_len), lambda i: (i, 0, 0)),
            pl.BlockSpec((tm // seq_len, KV_WIDTH, seq_len), lambda i: (i, 0, 0)),
            pl.BlockSpec((tm, D_MODEL), tile),
            pl.BlockSpec((tm, D_MODEL), tile),
            *cast_out,
        ),
        compiler_params=pltpu.CompilerParams(
            dimension_semantics=("arbitrary",), vmem_limit_bytes=VMEM_LIMIT),
        name=f"ctx_front_{layer}",
    )(xp, mods, norm1_g, w_in, bd, qg, kg, w_pool, pool_scale, w_out, norm2_g, *cast_weights)


def _lat_proj_kernel(xprev_ref, x_ref, xnext_ref, mod_ref, n1g_ref, win_ref, bd_ref, qg_ref, kg_ref,
                     cos_ref, sin_ref, wpool_ref, pscale_ref, q_out, k_out, v_out, mixed_out, *, seq_len):
    tm = x_ref.shape[0]
    halo = POOL_HALO
    xe = jnp.concatenate([xprev_ref[...], x_ref[...], xnext_ref[...]], axis=0)
    sh1, sc1 = mod_ref[0:1, :], mod_ref[1:2, :]
    h = _rms_mod(xe, n1g_ref[...], 1.0 + sc1, sh1).astype(BF16)
    proj = _dot(h, win_ref[...])
    main = proj[halo:halo + tm]
    bd = bd_ref[...]
    cos_t, sin_t = cos_ref[...], sin_ref[...]
    q = _head_norm(main[:, 0:ATTN_WIDTH], bd, qg_ref[...])
    q_out[...] = (_rope(q, cos_t, sin_t) * (HEAD_DIM ** -0.5 * LOG2_E)).astype(BF16)
    k = _head_norm(main[:, ATTN_WIDTH:ATTN_WIDTH + KV_WIDTH], bd[0:KV_WIDTH, 0:KV_WIDTH], kg_ref[...])
    k_out[...] = _rope(k, cos_t, sin_t).astype(BF16)
    v_out[...] = main[:, ATTN_WIDTH + KV_WIDTH:ATTN_WIDTH + 2 * KV_WIDTH].astype(BF16)
    u = proj[:, ATTN_WIDTH + 2 * KV_WIDTH:IN_WIDTH]
    t_loc = (lax.broadcasted_iota(jnp.int32, (tm + 2 * halo, LANES), 0)
             + (pl.program_id(1) * tm - halo))
    mixed = _pool_mix(u, t_loc, seq_len, wpool_ref, halo, tm) * pscale_ref[...]
    mixed_out[...] = mixed.astype(BF16)


def _lat_proj(layer, xs, mods, norm1_g, w_in, bd, qg, kg, cos_t, sin_t, w_pool, pool_scale):
    nb, seq_len, _ = xs.shape
    tm = LAT_TILE
    halo = POOL_HALO
    n_halo_blocks = seq_len // halo
    const2 = lambda b, i: (0, 0)
    tile = lambda b, i: (b, i, 0)
    return pl.pallas_call(
        functools.partial(_lat_proj_kernel, seq_len=seq_len),
        out_shape=(
            jax.ShapeDtypeStruct((nb, seq_len, ATTN_WIDTH), BF16),
            jax.ShapeDtypeStruct((nb, seq_len, KV_WIDTH), BF16),
            jax.ShapeDtypeStruct((nb, seq_len, KV_WIDTH), BF16),
            jax.ShapeDtypeStruct((nb, seq_len, POOL_WIDTH), BF16),
        ),
        grid=(nb, seq_len // tm),
        in_specs=[
            pl.BlockSpec((None, halo, D_MODEL),
                         lambda b, i: (b, jnp.maximum(i * (tm // halo) - 1, 0), 0)),
            pl.BlockSpec((None, tm, D_MODEL), tile),
            pl.BlockSpec((None, halo, D_MODEL),
                         lambda b, i: (b, jnp.minimum((i + 1) * (tm // halo), n_halo_blocks - 1), 0)),
            pl.BlockSpec((None, None, N_MOD, D_MODEL), lambda b, i: (layer, 1 + b, 0, 0)),
            pl.BlockSpec((1, D_MODEL), const2),
            pl.BlockSpec((D_MODEL, IN_WIDTH), const2),
            pl.BlockSpec((ATTN_WIDTH, ATTN_WIDTH), const2),
            pl.BlockSpec((1, ATTN_WIDTH), const2),
            pl.BlockSpec((1, KV_WIDTH), const2),
            pl.BlockSpec((tm, LANES), lambda b, i: (i, 0)),
            pl.BlockSpec((tm, LANES), lambda b, i: (i, 0)),
            pl.BlockSpec((None, 4, POOL_GROUP_DIM, POOL_GROUP_DIM), lambda b, i: (layer, 0, 0, 0)),
            pl.BlockSpec((1, POOL_WIDTH), const2),
        ],
        out_specs=(
            pl.BlockSpec((None, tm, ATTN_WIDTH), tile),
            pl.BlockSpec((None, tm, KV_WIDTH), tile),
            pl.BlockSpec((None, tm, KV_WIDTH), tile),
            pl.BlockSpec((None, tm, POOL_WIDTH), tile),
        ),
        compiler_params=pltpu.CompilerParams(
            dimension_semantics=("arbitrary", "arbitrary"), vmem_limit_bytes=VMEM_LIMIT),
        name=f"lat_proj_{layer}",
    )(xs, xs, xs, mods, norm1_g, w_in, bd, qg, kg, cos_t, sin_t, w_pool, pool_scale)


def _lat_attn_kernel(*refs, n_cast):
    (q_ref, klat_ref, vlat_ref, ck_ref, cv_ref, mixed_ref, x_ref, mod_ref, wout_ref,
     n2g_ref) = refs[:10]
    cast_src = refs[10:10 + n_cast]
    x1_out, h2_out = refs[10 + n_cast:12 + n_cast]
    cast_dst = refs[12 + n_cast:12 + 2 * n_cast]
    kdup_scr, vt_scr = refs[12 + 2 * n_cast:]
    _cast_blocks(cast_src, cast_dst)

    @pl.when(pl.program_id(1) == 0)
    def _():
        k_all = jnp.concatenate([ck_ref[...].T, klat_ref[...].astype(F32)], axis=0)
        vt_all = jnp.concatenate([cv_ref[...], vlat_ref[...].astype(F32).T], axis=1)
        k0, k1 = _dup_halves(k_all)
        kdup_scr[0] = k0
        kdup_scr[1] = k1
        for i, vt in enumerate(_value_rows(vt_all)):
            vt_scr[i] = vt

    attn = _attend_keys_major(q_ref[...], kdup_scr, vt_scr)
    _mix_residual_norm2(x_ref[...], attn, mixed_ref[...], wout_ref, mod_ref, n2g_ref, x1_out, h2_out)


def _lat_attn(layer, q, k, v, cache_k, cache_v, mixed, xs, mods, w_out, norm2_g, cast_weights, cast_layer):
    nb, seq_len, _ = xs.shape
    past = cache_k.shape[3]
    tq = LAT_Q_TILE
    n_keys = past + seq_len
    n_tiles = seq_len // tq
    tile = lambda b, j: (b, j, 0)
    whole = lambda b, j: (b, 0, 0)
    cast_in, cast_out, cast_shapes = _cast_specs(cast_weights, cast_layer, nb * n_tiles,
                                                 lambda b, j: b * n_tiles + j)
    return pl.pallas_call(
        functools.partial(_lat_attn_kernel, n_cast=len(cast_weights)),
        out_shape=(
            jax.ShapeDtypeStruct((nb, seq_len, D_MODEL), F32),
            jax.ShapeDtypeStruct((nb, seq_len, D_MODEL), BF16),
            *cast_shapes,
        ),
        grid=(nb, seq_len // tq),
        in_specs=[
            pl.BlockSpec((None, tq, ATTN_WIDTH), tile),
            pl.BlockSpec((None, seq_len, KV_WIDTH), whole),
            pl.BlockSpec((None, seq_len, KV_WIDTH), whole),
            pl.BlockSpec((None, None, KV_WIDTH, past), lambda b, j: (b, layer, 0, 0)),
            pl.BlockSpec((None, None, KV_WIDTH, past), lambda b, j: (b, layer, 0, 0)),
            pl.BlockSpec((None, tq, POOL_WIDTH), tile),
            pl.BlockSpec((None, tq, D_MODEL), tile),
            pl.BlockSpec((None, None, N_MOD, D_MODEL), lambda b, j: (layer, 1 + b, 0, 0)),
            pl.BlockSpec((D_MODEL, D_MODEL), lambda b, j: (0, 0)),
            pl.BlockSpec((1, D_MODEL), lambda b, j: (0, 0)),
            *cast_in,
        ],
        out_specs=(
            pl.BlockSpec((None, tq, D_MODEL), tile),
            pl.BlockSpec((None, tq, D_MODEL), tile),
            *cast_out,
        ),
        scratch_shapes=[
            pltpu.VMEM((N_KV_HEADS, n_keys, LANES), BF16),
            pltpu.VMEM((2 * N_KV_HEADS, LANES, n_keys), BF16),
        ],
        compiler_params=pltpu.CompilerParams(
            dimension_semantics=("arbitrary", "arbitrary"), vmem_limit_bytes=VMEM_LIMIT),
        name=f"lat_attn_{layer}",
    )(q, k, v, cache_k, cache_v, mixed, xs, mods, w_out, norm2_g, *cast_weights)


def _ffn_kernel(*refs, seq_len, final_norm, has_halo):
    if has_halo:
        (x1_ref, h2_ref, h2prev_ref, h2next_ref, mod_ref, wup_ref, cw_ref, cb_ref, wdn_ref, fg_ref,
         out_ref, z_scr, act_scr) = refs
    else:
        (x1_ref, h2_ref, mod_ref, wup_ref, cw_ref, cb_ref, wdn_ref, fg_ref,
         out_ref, z_scr, act_scr) = refs
    tm = x1_ref.shape[0]
    gap = F32_SUBLANES
    if has_halo:
        halo = BF16_SUBLANES
        tile_start = pl.program_id(1) * tm
        h2prev = jnp.where(tile_start % seq_len != 0, h2prev_ref[...], jnp.zeros_like(h2prev_ref))
        h2next = jnp.where((tile_start + tm) % seq_len != 0, h2next_ref[...], jnp.zeros_like(h2next_ref))
        he = jnp.concatenate([h2prev, h2_ref[...], h2next], axis=0)
        segments = [(halo, tm)]
    else:
        he = h2_ref[...]
        segments = [(gap + s * (seq_len + gap), seq_len) for s in range(tm // seq_len)]
        for slab in range(z_scr.shape[0]):
            for s in range(tm // seq_len + 1):
                r = s * (seq_len + gap)
                z_scr[slab, r:r + gap, :] = jnp.zeros((gap, LANES), F32)

    def store_z(slab, z):
        if has_halo:
            z_scr[slab, 0:z.shape[0], :] = z
        else:
            for s, (row0, n) in enumerate(segments):
                z_scr[slab, row0:row0 + n, :] = z[s * n:(s + 1) * n]

    def conv_rows(slab, row, n, col0):
        cols = slice(col0, col0 + LANES)
        return (z_scr[slab, pl.ds(row - 1, n, stride=1), :] * cw_ref[0:1, cols]
                + z_scr[slab, row:row + n, :] * cw_ref[1:2, cols]
                + z_scr[slab, pl.ds(row + 1, n, stride=1), :] * cw_ref[2:3, cols]
                + cb_ref[0:1, cols])

    n_sub = FFN_CHUNK // LANES
    for ci in range(D_FF // FFN_CHUNK):
        c0 = ci * FFN_CHUNK
        slab0 = 2 * n_sub * (ci % 2)
        za = _dot(he, wup_ref[:, c0:c0 + FFN_CHUNK])
        zg = _dot(he, wup_ref[:, D_FF + c0:D_FF + c0 + FFN_CHUNK])
        for j in range(n_sub):
            lanes = slice(j * LANES, (j + 1) * LANES)
            slab_a, slab_g = slab0 + 2 * j, slab0 + 2 * j + 1
            col = c0 + j * LANES
            store_z(slab_a, za[:, lanes])
            store_z(slab_g, zg[:, lanes])
            for s, (row0, n) in enumerate(segments):
                a = conv_rows(slab_a, row0, n, col)
                g = conv_rows(slab_g, row0, n, D_FF + col)
                act_scr[s * n:(s + 1) * n, col:col + LANES] = (jax.nn.silu(a) * g).astype(BF16)
    x2 = x1_ref[...] + mod_ref[5:6, :] * _dot(act_scr[...], wdn_ref[...])
    if final_norm:
        ms = jnp.mean(x2 * x2, axis=-1, keepdims=True)
        x2 = x2 * lax.rsqrt(ms + EPS) * fg_ref[...]
    out_ref[...] = x2


def _ffn(layer, x1, h2, seq_len, mod_row0, mods, w_up, conv_w, conv_b, w_down, final_g, final_norm):
    nb, n_tok, _ = x1.shape
    tm = FFN_TILE
    has_halo = seq_len > tm
    assert seq_len % tm == 0 or tm % seq_len == 0
    halo = BF16_SUBLANES
    gap = F32_SUBLANES
    n_halo_blocks = n_tok // halo
    tile = lambda b, i: (b, i, 0)
    halo_specs = [
        pl.BlockSpec((None, halo, D_MODEL),
                     lambda b, i: (b, jnp.maximum(i * (tm // halo) - 1, 0), 0)),
        pl.BlockSpec((None, halo, D_MODEL),
                     lambda b, i: (b, jnp.minimum((i + 1) * (tm // halo), n_halo_blocks - 1), 0)),
    ] if has_halo else []
    halo_args = [h2, h2] if has_halo else []
    z_rows = tm + 2 * halo if has_halo else gap + (tm // seq_len) * (seq_len + gap)
    return pl.pallas_call(
        functools.partial(_ffn_kernel, seq_len=seq_len, final_norm=final_norm, has_halo=has_halo),
        out_shape=jax.ShapeDtypeStruct((nb, n_tok, D_MODEL), F32),
        grid=(nb, n_tok // tm),
        in_specs=[
            pl.BlockSpec((None, tm, D_MODEL), tile),
            pl.BlockSpec((None, tm, D_MODEL), tile),
            *halo_specs,
            pl.BlockSpec((None, None, N_MOD, D_MODEL), lambda b, i: (layer, mod_row0 + b, 0, 0)),
            pl.BlockSpec((D_MODEL, 2 * D_FF), lambda b, i: (0, 0), pipeline_mode=pl.Buffered(1)),
            pl.BlockSpec((None, 3, 2 * D_FF), lambda b, i: (layer, 0, 0)),
            pl.BlockSpec((None, 1, 2 * D_FF), lambda b, i: (layer, 0, 0)),
            pl.BlockSpec((D_FF, D_MODEL), lambda b, i: (0, 0), pipeline_mode=pl.Buffered(1)),
            pl.BlockSpec((1, D_MODEL), lambda b, i: (0, 0)),
        ],
        out_specs=pl.BlockSpec((None, tm, D_MODEL), tile),
        scratch_shapes=[
            pltpu.VMEM((4 * (FFN_CHUNK // LANES), z_rows, LANES), F32),
            pltpu.VMEM((tm, D_FF), BF16),
        ],
        compiler_params=pltpu.CompilerParams(
            dimension_semantics=("arbitrary", "arbitrary"), vmem_limit_bytes=VMEM_LIMIT),
        name=f"ffn_{'lat' if nb > 1 else 'ctx'}_{layer}",
    )(x1, h2, *halo_args, mods, w_up, conv_w, conv_b, w_down, final_g)


def _rope_tables(n_tokens):
    t = jnp.arange(n_tokens)
    row = (t // GRID_W).astype(F32)
    col = (t % GRID_W).astype(F32)
    n_freq = HEAD_DIM // 4
    inv = ROPE_THETA ** (-jnp.arange(n_freq, dtype=F32) / n_freq)
    ang = jnp.stack([row[:, None] * inv, col[:, None] * inv], axis=1)
    cos = jnp.broadcast_to(jnp.cos(ang)[:, :, None, :], (n_tokens, 2, 2, n_freq))
    sin = jnp.sin(ang)[:, :, None, :] * jnp.array([-1.0, 1.0], F32)[None, None, :, None]
    cos = cos.reshape(n_tokens, HEAD_DIM)
    sin = sin.reshape(n_tokens, HEAD_DIM)
    return jnp.tile(cos, (1, 2)), jnp.tile(sin, (1, 2))


def kernel(x_prompt, x_sample, cache_k, cache_v, c, c_ctx, w_mod, b_mod, norm1_g, w_in, q_norm_g, k_norm_g, w_pool, pool_scale, w_out, norm2_g, w_up, conv_w, conv_b, w_down, final_norm_g):
    batch, seq, d = x_prompt.shape
    dec_batch, dec_seq, _ = x_sample.shape
    past = cache_k.shape[2]

    w_in_l = w_in[0].astype(BF16)
    w_out_l = w_out[0].astype(BF16)
    w_pool_b = w_pool.astype(BF16)

    cond8 = jnp.zeros((8, d), F32).at[0].set(c_ctx).at[1:1 + dec_batch].set(c)
    mods = _modulation(cond8, w_mod, b_mod).reshape(DEPTH, 8, N_MOD, d)

    head_id = jnp.arange(ATTN_WIDTH) // HEAD_DIM
    blockdiag = (head_id[:, None] == head_id[None, :]).astype(BF16)
    cos_t, sin_t = _rope_tables(dec_seq)
    ck = jnp.transpose(cache_k, (0, 1, 3, 4, 2)).reshape(dec_batch, DEPTH, KV_WIDTH, past)
    cv = jnp.transpose(cache_v, (0, 1, 3, 4, 2)).reshape(dec_batch, DEPTH, KV_WIDTH, past)
    final_g = final_norm_g.reshape(1, d)

    xp = x_prompt.reshape(batch * seq, d)
    xs = x_sample
    ks_out, vs_out = [], []
    for l in range(DEPTH):
        n1g = norm1_g[l].reshape(1, d)
        n2g = norm2_g[l].reshape(1, d)
        qg = jnp.tile(q_norm_g[l], N_HEADS).reshape(1, ATTN_WIDTH)
        kg = jnp.tile(k_norm_g[l], N_KV_HEADS).reshape(1, KV_WIDTH)
        ps = pool_scale[l].reshape(1, POOL_WIDTH)
        cb = conv_b.reshape(DEPTH, 1, 2 * D_FF)
        last = l == DEPTH - 1

        k_l, v_l, x1p, h2p, w_up_l, w_down_l = _ctx_front(
            l, xp, seq, mods, n1g, w_in_l, blockdiag, qg, kg, w_pool_b, ps, w_out_l, n2g,
            (w_up, w_down))
        ks_out.append(k_l)
        vs_out.append(v_l)
        xp = _ffn(l, x1p[None], h2p[None], seq, 0, mods, w_up_l, conv_w, cb, w_down_l,
                  final_g, last)[0]

        q, k, v, mixed = _lat_proj(l, xs, mods, n1g, w_in_l, blockdiag, qg, kg, cos_t, sin_t,
                                   w_pool_b, ps)
        x1s, h2s, *next_weights = _lat_attn(l, q, k, v, ck, cv, mixed, xs, mods, w_out_l, n2g,
                                            () if last else (w_in, w_out), l + 1)
        xs = _ffn(l, x1s, h2s, dec_seq, 1, mods, w_up_l, conv_w, cb, w_down_l, final_g, last)
        if not last:
            w_in_l, w_out_l = next_weights

    y_prompt = xp.reshape(batch, seq, d)

    def cache_layout(per_layer):
        t = jnp.stack(per_layer, axis=1).reshape(batch, DEPTH, N_KV_HEADS, HEAD_DIM, seq)
        return jnp.transpose(t, (0, 1, 4, 2, 3))

    new_cache_k = cache_layout(ks_out)
    new_cache_v = cache_layout(vs_out)
    return (y_prompt, xs, new_cache_k, new_cache_v)
```

```python
import functools

import jax
import jax.numpy as jnp
from jax import lax
from jax.experimental import pallas as pl
from jax.experimental.pallas import tpu as pltpu

D_MODEL = 1024
DEPTH = 4
GRID_W = 64
HEAD_DIM = 64
N_HEADS = 8
N_KV_HEADS = 2
ATTN_WIDTH = N_HEADS * HEAD_DIM
KV_WIDTH = N_KV_HEADS * HEAD_DIM
POOL_WIDTH = D_MODEL - ATTN_WIDTH
POOL_WINDOWS = (2, 4, 8, 16)
POOL_GROUP_DIM = 128
IN_WIDTH = ATTN_WIDTH + 2 * KV_WIDTH + POOL_WIDTH
D_FF = 2816
ROPE_THETA = 10000.0
EPS = 1e-6
N_MOD = 6
LOG2_E = 1.4426950408889634

LANES = 128
F32_SUBLANES = 8
BF16_SUBLANES = 16
VMEM_LIMIT = 56 * 1024 * 1024

POOL_HALO = 8
FFN_CHUNK = 256
CTX_TILE = 512
LAT_TILE = 512
LAT_Q_TILE = 512
LAT_Q_BLOCK = 128
FFN_TILE = 512
SCORE_LOOKAHEAD = 4

BF16 = jnp.bfloat16
F32 = jnp.float32


def _dot(a, b):
    return jnp.dot(a, b, preferred_element_type=F32)


def _dot_nt(a, b):
    return lax.dot_general(a, b, (((1,), (1,)), ((), ())), preferred_element_type=F32)


def _rms_mod(x, g, scale1p, shift):
    ms = jnp.mean(x * x, axis=-1, keepdims=True)
    return (x * lax.rsqrt(ms + EPS) * g) * scale1p + shift


def _head_norm(t, blockdiag, g_tiled):
    ssq = _dot((t * t).astype(BF16), blockdiag)
    return t * lax.rsqrt(ssq * (1.0 / HEAD_DIM) + EPS) * g_tiled


def _rope(t, cos_t, sin_t):
    lane = lax.broadcasted_iota(jnp.int32, (t.shape[0], LANES), 1)
    first = (lane % 32) < 16
    outs = []
    for ci in range(t.shape[1] // LANES):
        tc = t[:, ci * LANES:(ci + 1) * LANES]
        partner = jnp.where(first, pltpu.roll(tc, LANES - 16, 1), pltpu.roll(tc, 16, 1))
        outs.append(tc * cos_t + partner * sin_t)
    return outs[0] if len(outs) == 1 else jnp.concatenate(outs, axis=1)


def _dup_halves(t):
    lane = lax.broadcasted_iota(jnp.int32, t.shape, 1)
    lo = lane < HEAD_DIM
    sw = pltpu.roll(t, HEAD_DIM, 1)
    return jnp.where(lo, t, sw).astype(BF16), jnp.where(lo, sw, t).astype(BF16)


def _value_rows(vt):
    ones = jnp.ones((HEAD_DIM, vt.shape[1]), F32)
    out = []
    for g in range(N_KV_HEADS):
        vg = vt[g * HEAD_DIM:(g + 1) * HEAD_DIM]
        out.append(jnp.concatenate([vg, ones], axis=0).astype(BF16))
        out.append(jnp.concatenate([ones, vg], axis=0).astype(BF16))
    return out


def _attend_keys_major(q, kdup_ref, vt_ref, q_block):
    tq = q.shape[0]
    lane = lax.broadcasted_iota(jnp.int32, (1, LANES), 1)
    masks = ((lane < HEAD_DIM).astype(BF16), (lane >= HEAD_DIM).astype(BF16))
    units = [(r, g, half) for r in range(0, tq, q_block) for g in range(N_KV_HEADS) for half in range(2)]

    def scores(r, g, half):
        q_a = q[r:r + q_block, (2 * g) * LANES:(2 * g + 1) * LANES]
        q_b = q[r:r + q_block, (2 * g + 1) * LANES:(2 * g + 2) * LANES]
        q_rows = jnp.concatenate([q_a * masks[half], q_b * masks[half]], axis=0)
        return _dot_nt(kdup_ref[g], q_rows)

    def values(r, g, half, st):
        pt = jnp.exp2(st - jnp.max(st, axis=0, keepdims=True)).astype(BF16)
        return _dot(vt_ref[2 * g + half], pt).T

    outs = {}
    pending = [scores(*unit) for unit in units[:SCORE_LOOKAHEAD]]
    for i, unit in enumerate(units):
        if i + SCORE_LOOKAHEAD < len(units):
            pending.append(scores(*units[i + SCORE_LOOKAHEAD]))
        outs[unit] = values(*unit, pending.pop(0))

    lo = lax.broadcasted_iota(jnp.int32, (2 * q_block, LANES), 1) < HEAD_DIM
    rows = []
    for r in range(0, tq, q_block):
        blocks = []
        for g in range(N_KV_HEADS):
            o_even, o_odd = outs[(r, g, 0)], outs[(r, g, 1)]
            num = jnp.where(lo, o_even, o_odd)
            den = jnp.where(lo, pltpu.roll(o_even, HEAD_DIM, 1), pltpu.roll(o_odd, HEAD_DIM, 1))
            out = num / den
            blocks.extend([out[0:q_block], out[q_block:2 * q_block]])
        rows.append(jnp.concatenate(blocks, axis=1))
    return rows[0] if len(rows) == 1 else jnp.concatenate(rows, axis=0)


def _pool_mix(u, t_loc, seq_len, wpool_ref, lo_row, n_rows):
    m = u.shape[0]
    outs = []
    for gi, w in enumerate(POOL_WINDOWS):
        ug = u[:, gi * LANES:(gi + 1) * LANES]
        half = w // 2
        past, future, d = ug, ug, 1
        while d < half:
            past = past + jnp.where(t_loc >= d, pltpu.roll(past, d, 0), 0.0)
            future = future + jnp.where(t_loc + d < seq_len, pltpu.roll(future, m - d, 0), 0.0)
            d *= 2
        total = jnp.where(t_loc >= 1, pltpu.roll(past, 1, 0), 0.0) + future
        cnt = jnp.minimum(t_loc + half, seq_len) - jnp.maximum(t_loc - half, 0)
        pooled = total / cnt.astype(F32) - ug
        pooled = pooled[lo_row:lo_row + n_rows]
        outs.append(_dot(pooled.astype(BF16), wpool_ref[gi]))
    return jnp.concatenate(outs, axis=1)


def _mix_residual_norm2(x, attn, mixed, wout_ref, mod_ref, n2g_ref, x1_out, h2_out):
    gt1, sh2, sc2 = mod_ref[2:3, :], mod_ref[3:4, :], mod_ref[4:5, :]
    mix = (_dot(attn.astype(BF16), wout_ref[0:ATTN_WIDTH, :])
           + _dot(mixed.astype(BF16), wout_ref[ATTN_WIDTH:D_MODEL, :]))
    x1 = x + gt1 * mix
    x1_out[...] = x1
    h2_out[...] = _rms_mod(x1, n2g_ref[...], 1.0 + sc2, sh2).astype(BF16)


def _cast_specs(weights, layer, n_steps, step_index):
    in_specs, out_specs, out_shapes = [], [], []
    for w in weights:
        _, rows, cols = w.shape
        blk = rows // n_steps
        assert blk * n_steps == rows and blk % BF16_SUBLANES == 0
        in_specs.append(pl.BlockSpec((None, blk, cols), lambda *g: (layer, step_index(*g), 0)))
        out_specs.append(pl.BlockSpec((blk, cols), lambda *g: (step_index(*g), 0)))
        out_shapes.append(jax.ShapeDtypeStruct((rows, cols), BF16))
    return in_specs, out_specs, out_shapes


def _cast_blocks(src_refs, dst_refs):
    for src, dst in zip(src_refs, dst_refs, strict=True):
        dst[...] = src[...].astype(BF16)


def _mod_kernel(cond_ref, w_ref, b_ref, out_ref):
    s = jax.nn.silu(cond_ref[...]).astype(BF16)
    out_ref[...] = _dot(s, w_ref[...].astype(BF16)) + b_ref[...]


def _modulation(cond8, w_mod, b_mod):
    tn = 1536
    n = N_MOD * D_MODEL
    return pl.pallas_call(
        _mod_kernel,
        out_shape=jax.ShapeDtypeStruct((DEPTH, 8, n), F32),
        grid=(DEPTH, n // tn),
        in_specs=[
            pl.BlockSpec((8, D_MODEL), lambda l, j: (0, 0)),
            pl.BlockSpec((None, D_MODEL, tn), lambda l, j: (l, 0, j)),
            pl.BlockSpec((None, 1, tn), lambda l, j: (l, 0, j)),
        ],
        out_specs=pl.BlockSpec((None, 8, tn), lambda l, j: (l, 0, j)),
        compiler_params=pltpu.CompilerParams(
            dimension_semantics=("arbitrary", "arbitrary"), vmem_limit_bytes=VMEM_LIMIT),
        name="modulation",
    )(cond8, w_mod, b_mod.reshape(DEPTH, 1, n))


def _ctx_front_kernel(*refs, seq_len, n_cast):
    (x_ref, mod_ref, n1g_ref, win_ref, bd_ref, qg_ref, kg_ref, wpool_ref, pscale_ref, wout_ref,
     n2g_ref) = refs[:11]
    cast_src = refs[11:11 + n_cast]
    kt_out, vt_out, x1_out, h2_out = refs[11 + n_cast:15 + n_cast]
    _cast_blocks(cast_src, refs[15 + n_cast:])
    x = x_ref[...]
    tm = x.shape[0]
    sh1, sc1 = mod_ref[0:1, :], mod_ref[1:2, :]
    h = _rms_mod(x, n1g_ref[...], 1.0 + sc1, sh1).astype(BF16)
    proj = _dot(h, win_ref[...])
    bd = bd_ref[...]
    q = (_head_norm(proj[:, 0:ATTN_WIDTH], bd, qg_ref[...]) * (HEAD_DIM ** -0.5 * LOG2_E)).astype(BF16)
    k = _head_norm(proj[:, ATTN_WIDTH:ATTN_WIDTH + KV_WIDTH], bd[0:KV_WIDTH, 0:KV_WIDTH], kg_ref[...])
    v = proj[:, ATTN_WIDTH + KV_WIDTH:ATTN_WIDTH + 2 * KV_WIDTH]
    u = proj[:, ATTN_WIDTH + 2 * KV_WIDTH:IN_WIDTH]

    seq_outs = []
    for s in range(tm // seq_len):
        r0 = s * seq_len
        kt_out[s] = k[r0:r0 + seq_len].T
        vt = v[r0:r0 + seq_len].T
        vt_out[s] = vt
        kdup = _dup_halves(k[r0:r0 + seq_len])
        seq_outs.append(_attend_keys_major(q[r0:r0 + seq_len], kdup, _value_rows(vt), seq_len))
    attn = jnp.concatenate(seq_outs, axis=0)

    t_loc = lax.broadcasted_iota(jnp.int32, (tm, LANES), 0) % seq_len
    mixed = _pool_mix(u, t_loc, seq_len, wpool_ref, 0, tm) * pscale_ref[...]
    _mix_residual_norm2(x, attn, mixed, wout_ref, mod_ref, n2g_ref, x1_out, h2_out)


def _ctx_front(layer, xp, seq_len, mods, norm1_g, w_in, bd, qg, kg, w_pool, pool_scale, w_out, norm2_g,
               cast_weights):
    n_tok = xp.shape[0]
    tm = CTX_TILE
    const2 = lambda i: (0, 0)
    tile = lambda i: (i, 0)
    cast_in, cast_out, cast_shapes = _cast_specs(cast_weights, layer, n_tok // tm, lambda i: i)
    return pl.pallas_call(
        functools.partial(_ctx_front_kernel, seq_len=seq_len, n_cast=len(cast_weights)),
        out_shape=(
            jax.ShapeDtypeStruct((n_tok // seq_len, KV_WIDTH, seq_len), F32),
            jax.ShapeDtypeStruct((n_tok // seq_len, KV_WIDTH, seq_len), F32),
            jax.ShapeDtypeStruct((n_tok, D_MODEL), F32),
            jax.ShapeDtypeStruct((n_tok, D_MODEL), BF16),
            *cast_shapes,
        ),
        grid=(n_tok // tm,),
        in_specs=[
            pl.BlockSpec((tm, D_MODEL), tile),
            pl.BlockSpec((None, None, N_MOD, D_MODEL), lambda i: (layer, 0, 0, 0)),
            pl.BlockSpec((1, D_MODEL), const2),
            pl.BlockSpec((D_MODEL, IN_WIDTH), const2),
            pl.BlockSpec((ATTN_WIDTH, ATTN_WIDTH), const2),
            pl.BlockSpec((1, ATTN_WIDTH), const2),
            pl.BlockSpec((1, KV_WIDTH), const2),
            pl.BlockSpec((None, 4, POOL_GROUP_DIM, POOL_GROUP_DIM), lambda i: (layer, 0, 0, 0)),
            pl.BlockSpec((1, POOL_WIDTH), const2),
            pl.BlockSpec((D_MODEL, D_MODEL), const2),
            pl.BlockSpec((1, D_MODEL), const2),
            *cast_in,
        ],
        out_specs=(
            pl.BlockSpec((tm // seq_len, KV_WIDTH, seq_len), lambda i: (i, 0, 0)),
            pl.BlockSpec((tm // seq_len, KV_WIDTH, seq_len), lambda i: (i, 0, 0)),
            pl.BlockSpec((tm, D_MODEL), tile),
            pl.BlockSpec((tm, D_MODEL), tile),
            *cast_out,
        ),
        compiler_params=pltpu.CompilerParams(
            dimension_semantics=("arbitrary",), vmem_limit_bytes=VMEM_LIMIT),
        name=f"ctx_front_{layer}",
    )(xp, mods, norm1_g, w_in, bd, qg, kg, w_pool, pool_scale, w_out, norm2_g, *cast_weights)


def _lat_proj_kernel(xprev_ref, x_ref, xnext_ref, mod_ref, n1g_ref, win_ref, bd_ref, qg_ref, kg_ref,
                     cos_ref, sin_ref, wpool_ref, pscale_ref, q_out, k_out, v_out, mixed_out, *, seq_len):
    tm = x_ref.shape[0]
    halo = POOL_HALO
    xe = jnp.concatenate([xprev_ref[...], x_ref[...], xnext_ref[...]], axis=0)
    sh1, sc1 = mod_ref[0:1, :], mod_ref[1:2, :]
    h = _rms_mod(xe, n1g_ref[...], 1.0 + sc1, sh1).astype(BF16)
    proj = _dot(h, win_ref[...])
    main = proj[halo:halo + tm]
    bd = bd_ref[...]
    cos_t, sin_t = cos_ref[...], sin_ref[...]
    q = _head_norm(main[:, 0:ATTN_WIDTH], bd, qg_ref[...])
    q_out[...] = (_rope(q, cos_t, sin_t) * (HEAD_DIM ** -0.5 * LOG2_E)).astype(BF16)
    k = _head_norm(main[:, ATTN_WIDTH:ATTN_WIDTH + KV_WIDTH], bd[0:KV_WIDTH, 0:KV_WIDTH], kg_ref[...])
    k_out[...] = _rope(k, cos_t, sin_t).astype(BF16)
    v_out[...] = main[:, ATTN_WIDTH + KV_WIDTH:ATTN_WIDTH + 2 * KV_WIDTH].astype(BF16)
    u = proj[:, ATTN_WIDTH + 2 * KV_WIDTH:IN_WIDTH]
    t_loc = (lax.broadcasted_iota(jnp.int32, (tm + 2 * halo, LANES), 0)
             + (pl.program_id(1) * tm - halo))
    mixed = _pool_mix(u, t_loc, seq_len, wpool_ref, halo, tm) * pscale_ref[...]
    mixed_out[...] = mixed.astype(BF16)


def _lat_proj(layer, xs, mods, norm1_g, w_in, bd, qg, kg, cos_t, sin_t, w_pool, pool_scale):
    nb, seq_len, _ = xs.shape
    tm = LAT_TILE
    halo = POOL_HALO
    n_halo_blocks = seq_len // halo
    const2 = lambda b, i: (0, 0)
    tile = lambda b, i: (b, i, 0)
    return pl.pallas_call(
        functools.partial(_lat_proj_kernel, seq_len=seq_len),
        out_shape=(
            jax.ShapeDtypeStruct((nb, seq_len, ATTN_WIDTH), BF16),
            jax.ShapeDtypeStruct((nb, seq_len, KV_WIDTH), BF16),
            jax.ShapeDtypeStruct((nb, seq_len, KV_WIDTH), BF16),
            jax.ShapeDtypeStruct((nb, seq_len, POOL_WIDTH), BF16),
        ),
        grid=(nb, seq_len // tm),
        in_specs=[
            pl.BlockSpec((None, halo, D_MODEL),
                         lambda b, i: (b, jnp.maximum(i * (tm // halo) - 1, 0), 0)),
            pl.BlockSpec((None, tm, D_MODEL), tile),
            pl.BlockSpec((None, halo, D_MODEL),
                         lambda b, i: (b, jnp.minimum((i + 1) * (tm // halo), n_halo_blocks - 1), 0)),
            pl.BlockSpec((None, None, N_MOD, D_MODEL), lambda b, i: (layer, 1 + b, 0, 0)),
            pl.BlockSpec((1, D_MODEL), const2),
            pl.BlockSpec((D_MODEL, IN_WIDTH), const2),
            pl.BlockSpec((ATTN_WIDTH, ATTN_WIDTH), const2),
            pl.BlockSpec((1, ATTN_WIDTH), const2),
            pl.BlockSpec((1, KV_WIDTH), const2),
            pl.BlockSpec((tm, LANES), lambda b, i: (i, 0)),
            pl.BlockSpec((tm, LANES), lambda b, i: (i, 0)),
            pl.BlockSpec((None, 4, POOL_GROUP_DIM, POOL_GROUP_DIM), lambda b, i: (layer, 0, 0, 0)),
            pl.BlockSpec((1, POOL_WIDTH), const2),
        ],
        out_specs=(
            pl.BlockSpec((None, tm, ATTN_WIDTH), tile),
            pl.BlockSpec((None, tm, KV_WIDTH), tile),
            pl.BlockSpec((None, tm, KV_WIDTH), tile),
            pl.BlockSpec((None, tm, POOL_WIDTH), tile),
        ),
        compiler_params=pltpu.CompilerParams(
            dimension_semantics=("arbitrary", "arbitrary"), vmem_limit_bytes=VMEM_LIMIT),
        name=f"lat_proj_{layer}",
    )(xs, xs, xs, mods, norm1_g, w_in, bd, qg, kg, cos_t, sin_t, w_pool, pool_scale)


def _lat_attn_kernel(*refs, n_cast):
    (q_ref, klat_ref, vlat_ref, ck_ref, cv_ref, mixed_ref, x_ref, mod_ref, wout_ref,
     n2g_ref) = refs[:10]
    cast_src = refs[10:10 + n_cast]
    x1_out, h2_out = refs[10 + n_cast:12 + n_cast]
    cast_dst = refs[12 + n_cast:12 + 2 * n_cast]
    kdup_scr, vt_scr = refs[12 + 2 * n_cast:]
    _cast_blocks(cast_src, cast_dst)

    @pl.when(pl.program_id(1) == 0)
    def _():
        k_all = jnp.concatenate([ck_ref[...].T, klat_ref[...].astype(F32)], axis=0)
        vt_all = jnp.concatenate([cv_ref[...], vlat_ref[...].astype(F32).T], axis=1)
        k0, k1 = _dup_halves(k_all)
        kdup_scr[0] = k0
        kdup_scr[1] = k1
        for i, vt in enumerate(_value_rows(vt_all)):
            vt_scr[i] = vt

    attn = _attend_keys_major(q_ref[...], kdup_scr, vt_scr, LAT_Q_BLOCK)
    _mix_residual_norm2(x_ref[...], attn, mixed_ref[...], wout_ref, mod_ref, n2g_ref, x1_out, h2_out)


def _lat_attn(layer, q, k, v, cache_k, cache_v, mixed, xs, mods, w_out, norm2_g, cast_weights, cast_layer):
    nb, seq_len, _ = xs.shape
    past = cache_k.shape[3]
    tq = LAT_Q_TILE
    n_keys = past + seq_len
    n_tiles = seq_len // tq
    tile = lambda b, j: (b, j, 0)
    whole = lambda b, j: (b, 0, 0)
    cast_in, cast_out, cast_shapes = _cast_specs(cast_weights, cast_layer, nb * n_tiles,
                                                 lambda b, j: b * n_tiles + j)
    return pl.pallas_call(
        functools.partial(_lat_attn_kernel, n_cast=len(cast_weights)),
        out_shape=(
            jax.ShapeDtypeStruct((nb, seq_len, D_MODEL), F32),
            jax.ShapeDtypeStruct((nb, seq_len, D_MODEL), BF16),
            *cast_shapes,
        ),
        grid=(nb, seq_len // tq),
        in_specs=[
            pl.BlockSpec((None, tq, ATTN_WIDTH), tile),
            pl.BlockSpec((None, seq_len, KV_WIDTH), whole),
            pl.BlockSpec((None, seq_len, KV_WIDTH), whole),
            pl.BlockSpec((None, None, KV_WIDTH, past), lambda b, j: (b, layer, 0, 0)),
            pl.BlockSpec((None, None, KV_WIDTH, past), lambda b, j: (b, layer, 0, 0)),
            pl.BlockSpec((None, tq, POOL_WIDTH), tile),
            pl.BlockSpec((None, tq, D_MODEL), tile),
            pl.BlockSpec((None, None, N_MOD, D_MODEL), lambda b, j: (layer, 1 + b, 0, 0)),
            pl.BlockSpec((D_MODEL, D_MODEL), lambda b, j: (0, 0)),
            pl.BlockSpec((1, D_MODEL), lambda b, j: (0, 0)),
            *cast_in,
        ],
        out_specs=(
            pl.BlockSpec((None, tq, D_MODEL), tile),
            pl.BlockSpec((None, tq, D_MODEL), tile),
            *cast_out,
        ),
        scratch_shapes=[
            pltpu.VMEM((N_KV_HEADS, n_keys, LANES), BF16),
            pltpu.VMEM((2 * N_KV_HEADS, LANES, n_keys), BF16),
        ],
        compiler_params=pltpu.CompilerParams(
            dimension_semantics=("arbitrary", "arbitrary"), vmem_limit_bytes=VMEM_LIMIT),
        name=f"lat_attn_{layer}",
    )(q, k, v, cache_k, cache_v, mixed, xs, mods, w_out, norm2_g, *cast_weights)


def _ffn_kernel(*refs, seq_len, final_norm, has_halo):
    if has_halo:
        (x1_ref, h2_ref, h2prev_ref, h2next_ref, mod_ref, wup_ref, cw_ref, cb_ref, wdn_ref, fg_ref,
         out_ref, z_scr, act_scr) = refs
    else:
        (x1_ref, h2_ref, mod_ref, wup_ref, cw_ref, cb_ref, wdn_ref, fg_ref,
         out_ref, z_scr, act_scr) = refs
    tm = x1_ref.shape[0]
    gap = F32_SUBLANES
    if has_halo:
        halo = BF16_SUBLANES
        tile_start = pl.program_id(1) * tm
        h2prev = jnp.where(tile_start % seq_len != 0, h2prev_ref[...], jnp.zeros_like(h2prev_ref))
        h2next = jnp.where((tile_start + tm) % seq_len != 0, h2next_ref[...], jnp.zeros_like(h2next_ref))
        he = jnp.concatenate([h2prev, h2_ref[...], h2next], axis=0)
        segments = [(halo, tm)]
    else:
        he = h2_ref[...]
        segments = [(gap + s * (seq_len + gap), seq_len) for s in range(tm // seq_len)]
        for slab in range(z_scr.shape[0]):
            for s in range(tm // seq_len + 1):
                r = s * (seq_len + gap)
                z_scr[slab, r:r + gap, :] = jnp.zeros((gap, LANES), F32)

    def store_z(slab, z):
        if has_halo:
            z_scr[slab, 0:z.shape[0], :] = z
        else:
            for s, (row0, n) in enumerate(segments):
                z_scr[slab, row0:row0 + n, :] = z[s * n:(s + 1) * n]

    def conv_rows(slab, row, n, col0):
        cols = slice(col0, col0 + LANES)
        return (z_scr[slab, pl.ds(row - 1, n, stride=1), :] * cw_ref[0:1, cols]
                + z_scr[slab, row:row + n, :] * cw_ref[1:2, cols]
                + z_scr[slab, pl.ds(row + 1, n, stride=1), :] * cw_ref[2:3, cols]
                + cb_ref[0:1, cols])

    n_sub = FFN_CHUNK // LANES
    for ci in range(D_FF // FFN_CHUNK):
        c0 = ci * FFN_CHUNK
        slab0 = 2 * n_sub * (ci % 2)
        za = _dot(he, wup_ref[:, c0:c0 + FFN_CHUNK])
        zg = _dot(he, wup_ref[:, D_FF + c0:D_FF + c0 + FFN_CHUNK])
        for j in range(n_sub):
            lanes = slice(j * LANES, (j + 1) * LANES)
            slab_a, slab_g = slab0 + 2 * j, slab0 + 2 * j + 1
            col = c0 + j * LANES
            store_z(slab_a, za[:, lanes])
            store_z(slab_g, zg[:, lanes])
            for s, (row0, n) in enumerate(segments):
                a = conv_rows(slab_a, row0, n, col)
                g = conv_rows(slab_g, row0, n, D_FF + col)
                act_scr[s * n:(s + 1) * n, col:col + LANES] = (jax.nn.silu(a) * g).astype(BF16)
    x2 = x1_ref[...] + mod_ref[5:6, :] * _dot(act_scr[...], wdn_ref[...])
    if final_norm:
        ms = jnp.mean(x2 * x2, axis=-1, keepdims=True)
        x2 = x2 * lax.rsqrt(ms + EPS) * fg_ref[...]
    out_ref[...] = x2


def _ffn(layer, x1, h2, seq_len, mod_row0, mods, w_up, conv_w, conv_b, w_down, final_g, final_norm):
    nb, n_tok, _ = x1.shape
    tm = FFN_TILE
    has_halo = seq_len > tm
    assert seq_len % tm == 0 or tm % seq_len == 0
    halo = BF16_SUBLANES
    gap = F32_SUBLANES
    n_halo_blocks = n_tok // halo
    tile = lambda b, i: (b, i, 0)
    halo_specs = [
        pl.BlockSpec((None, halo, D_MODEL),
                     lambda b, i: (b, jnp.maximum(i * (tm // halo) - 1, 0), 0)),
        pl.BlockSpec((None, halo, D_MODEL),
                     lambda b, i: (b, jnp.minimum((i + 1) * (tm // halo), n_halo_blocks - 1), 0)),
    ] if has_halo else []
    halo_args = [h2, h2] if has_halo else []
    z_rows = tm + 2 * halo if has_halo else gap + (tm // seq_len) * (seq_len + gap)
    return pl.pallas_call(
        functools.partial(_ffn_kernel, seq_len=seq_len, final_norm=final_norm, has_halo=has_halo),
        out_shape=jax.ShapeDtypeStruct((nb, n_tok, D_MODEL), F32),
        grid=(nb, n_tok // tm),
        in_specs=[
            pl.BlockSpec((None, tm, D_MODEL), tile),
            pl.BlockSpec((None, tm, D_MODEL), tile),
            *halo_specs,
            pl.BlockSpec((None, None, N_MOD, D_MODEL), lambda b, i: (layer, mod_row0 + b, 0, 0)),
            pl.BlockSpec((D_MODEL, 2 * D_FF), lambda b, i: (0, 0), pipeline_mode=pl.Buffered(1)),
            pl.BlockSpec((None, 3, 2 * D_FF), lambda b, i: (layer, 0, 0)),
            pl.BlockSpec((None, 1, 2 * D_FF), lambda b, i: (layer, 0, 0)),
            pl.BlockSpec((D_FF, D_MODEL), lambda b, i: (0, 0), pipeline_mode=pl.Buffered(1)),
            pl.BlockSpec((1, D_MODEL), lambda b, i: (0, 0)),
        ],
        out_specs=pl.BlockSpec((None, tm, D_MODEL), tile),
        scratch_shapes=[
            pltpu.VMEM((4 * (FFN_CHUNK // LANES), z_rows, LANES), F32),
            pltpu.VMEM((tm, D_FF), BF16),
        ],
        compiler_params=pltpu.CompilerParams(
            dimension_semantics=("arbitrary", "arbitrary"), vmem_limit_bytes=VMEM_LIMIT),
        name=f"ffn_{'lat' if nb > 1 else 'ctx'}_{layer}",
    )(x1, h2, *halo_args, mods, w_up, conv_w, conv_b, w_down, final_g)


def _rope_tables(n_tokens):
    t = jnp.arange(n_tokens)
    row = (t // GRID_W).astype(F32)
    col = (t % GRID_W).astype(F32)
    n_freq = HEAD_DIM // 4
    inv = ROPE_THETA ** (-jnp.arange(n_freq, dtype=F32) / n_freq)
    ang = jnp.stack([row[:, None] * inv, col[:, None] * inv], axis=1)
    cos = jnp.broadcast_to(jnp.cos(ang)[:, :, None, :], (n_tokens, 2, 2, n_freq))
    sin = jnp.sin(ang)[:, :, None, :] * jnp.array([-1.0, 1.0], F32)[None, None, :, None]
    cos = cos.reshape(n_tokens, HEAD_DIM)
    sin = sin.reshape(n_tokens, HEAD_DIM)
    return jnp.tile(cos, (1, 2)), jnp.tile(sin, (1, 2))


def kernel(x_prompt, x_sample, cache_k, cache_v, c, c_ctx, w_mod, b_mod, norm1_g, w_in, q_norm_g, k_norm_g, w_pool, pool_scale, w_out, norm2_g, w_up, conv_w, conv_b, w_down, final_norm_g):
    batch, seq, d = x_prompt.shape
    dec_batch, dec_seq, _ = x_sample.shape
    past = cache_k.shape[2]

    w_in_l = w_in[0].astype(BF16)
    w_out_l = w_out[0].astype(BF16)
    w_pool_b = w_pool.astype(BF16)

    cond8 = jnp.zeros((8, d), F32).at[0].set(c_ctx).at[1:1 + dec_batch].set(c)
    mods = _modulation(cond8, w_mod, b_mod).reshape(DEPTH, 8, N_MOD, d)

    head_id = jnp.arange(ATTN_WIDTH) // HEAD_DIM
    blockdiag = (head_id[:, None] == head_id[None, :]).astype(BF16)
    cos_t, sin_t = _rope_tables(dec_seq)
    ck = jnp.transpose(cache_k, (0, 1, 3, 4, 2)).reshape(dec_batch, DEPTH, KV_WIDTH, past)
    cv = jnp.transpose(cache_v, (0, 1, 3, 4, 2)).reshape(dec_batch, DEPTH, KV_WIDTH, past)
    final_g = final_norm_g.reshape(1, d)

    xp = x_prompt.reshape(batch * seq, d)
    xs = x_sample
    ks_out, vs_out = [], []
    for l in range(DEPTH):
        n1g = norm1_g[l].reshape(1, d)
        n2g = norm2_g[l].reshape(1, d)
        qg = jnp.tile(q_norm_g[l], N_HEADS).reshape(1, ATTN_WIDTH)
        kg = jnp.tile(k_norm_g[l], N_KV_HEADS).reshape(1, KV_WIDTH)
        ps = pool_scale[l].reshape(1, POOL_WIDTH)
        cb = conv_b.reshape(DEPTH, 1, 2 * D_FF)
        last = l == DEPTH - 1

        k_l, v_l, x1p, h2p, w_up_l, w_down_l = _ctx_front(
            l, xp, seq, mods, n1g, w_in_l, blockdiag, qg, kg, w_pool_b, ps, w_out_l, n2g,
            (w_up, w_down))
        ks_out.append(k_l)
        vs_out.append(v_l)
        xp = _ffn(l, x1p[None], h2p[None], seq, 0, mods, w_up_l, conv_w, cb, w_down_l,
                  final_g, last)[0]

        q, k, v, mixed = _lat_proj(l, xs, mods, n1g, w_in_l, blockdiag, qg, kg, cos_t, sin_t,
                                   w_pool_b, ps)
        x1s, h2s, *next_weights = _lat_attn(l, q, k, v, ck, cv, mixed, xs, mods, w_out_l, n2g,
                                            () if last else (w_in, w_out), l + 1)
        xs = _ffn(l, x1s, h2s, dec_seq, 1, mods, w_up_l, conv_w, cb, w_down_l, final_g, last)
        if not last:
            w_in_l, w_out_l = next_weights

    y_prompt = xp.reshape(batch, seq, d)

    def cache_layout(per_layer):
        t = jnp.stack(per_layer, axis=1).reshape(batch, DEPTH, N_KV_HEADS, HEAD_DIM, seq)
        return jnp.transpose(t, (0, 1, 4, 2, 3))

    new_cache_k = cache_layout(ks_out)
    new_cache_v = cache_layout(vs_out)
    return (y_prompt, xs, new_cache_k, new_cache_v)
```

```python
import functools

import jax
import jax.numpy as jnp
from jax import lax
from jax.experimental import pallas as pl
from jax.experimental.pallas import tpu as pltpu

D_MODEL = 1024
DEPTH = 4
GRID_W = 64
HEAD_DIM = 64
N_HEADS = 8
N_KV_HEADS = 2
ATTN_WIDTH = N_HEADS * HEAD_DIM
KV_WIDTH = N_KV_HEADS * HEAD_DIM
POOL_WIDTH = D_MODEL - ATTN_WIDTH
POOL_WINDOWS = (2, 4, 8, 16)
POOL_GROUP_DIM = 128
IN_WIDTH = ATTN_WIDTH + 2 * KV_WIDTH + POOL_WIDTH
D_FF = 2816
ROPE_THETA = 10000.0
EPS = 1e-6
N_MOD = 6
LOG2_E = 1.4426950408889634

LANES = 128
F32_SUBLANES = 8
BF16_SUBLANES = 16
VMEM_LIMIT = 56 * 1024 * 1024

POOL_HALO = 8
FFN_CHUNK = 256
CTX_TILE = 1024
LAT_TILE = 512
LAT_Q_TILE = 512
LAT_Q_BLOCK = 128
FFN_TILE = 512
LAT_SCORE_LOOKAHEAD = 4
CTX_SCORE_LOOKAHEAD = 2
CTX_SEQ_SKEW = 1

BF16 = jnp.bfloat16
F32 = jnp.float32


def _dot(a, b):
    return jnp.dot(a, b, preferred_element_type=F32)


def _dot_nt(a, b):
    return lax.dot_general(a, b, (((1,), (1,)), ((), ())), preferred_element_type=F32)


def _rms_mod(x, g, scale1p, shift):
    ms = jnp.mean(x * x, axis=-1, keepdims=True)
    return (x * lax.rsqrt(ms + EPS) * g) * scale1p + shift


def _head_norm(t, blockdiag, g_tiled):
    ssq = _dot((t * t).astype(BF16), blockdiag)
    return t * lax.rsqrt(ssq * (1.0 / HEAD_DIM) + EPS) * g_tiled


def _rope(t, cos_t, sin_t):
    lane = lax.broadcasted_iota(jnp.int32, (t.shape[0], LANES), 1)
    first = (lane % 32) < 16
    outs = []
    for ci in range(t.shape[1] // LANES):
        tc = t[:, ci * LANES:(ci + 1) * LANES]
        partner = jnp.where(first, pltpu.roll(tc, LANES - 16, 1), pltpu.roll(tc, 16, 1))
        outs.append(tc * cos_t + partner * sin_t)
    return outs[0] if len(outs) == 1 else jnp.concatenate(outs, axis=1)


def _dup_halves(t):
    lane = lax.broadcasted_iota(jnp.int32, t.shape, 1)
    lo = lane < HEAD_DIM
    sw = pltpu.roll(t, HEAD_DIM, 1)
    return jnp.where(lo, t, sw).astype(BF16), jnp.where(lo, sw, t).astype(BF16)


def _value_rows(vt):
    ones = jnp.ones((HEAD_DIM, vt.shape[1]), F32)
    out = []
    for g in range(N_KV_HEADS):
        vg = vt[g * HEAD_DIM:(g + 1) * HEAD_DIM]
        out.append(jnp.concatenate([vg, ones], axis=0).astype(BF16))
        out.append(jnp.concatenate([ones, vg], axis=0).astype(BF16))
    return out


def _interleave(chains, skew):
    live = list(enumerate(chains))
    t = 0
    while live:
        for i, chain in list(live):
            if t >= i * skew and next(chain, StopIteration) is StopIteration:
                live.remove((i, chain))
        t += 1


def _run(chain):
    while True:
        try:
            next(chain)
        except StopIteration as stop:
            return stop.value


def _attend_keys_major(q, kdup_ref, vt_ref, q_block, lookahead):
    tq = q.shape[0]
    lane = lax.broadcasted_iota(jnp.int32, (1, LANES), 1)
    masks = ((lane < HEAD_DIM).astype(BF16), (lane >= HEAD_DIM).astype(BF16))
    units = [(r, g, half) for r in range(0, tq, q_block) for g in range(N_KV_HEADS) for half in range(2)]

    def scores(r, g, half):
        q_a = q[r:r + q_block, (2 * g) * LANES:(2 * g + 1) * LANES]
        q_b = q[r:r + q_block, (2 * g + 1) * LANES:(2 * g + 2) * LANES]
        q_rows = jnp.concatenate([q_a * masks[half], q_b * masks[half]], axis=0)
        return _dot_nt(kdup_ref[g], q_rows)

    def values(r, g, half, st):
        pt = jnp.exp2(st - jnp.max(st, axis=0, keepdims=True)).astype(BF16)
        return _dot(vt_ref[2 * g + half], pt).T

    outs = {}
    pending = []
    for unit in units[:lookahead]:
        pending.append(scores(*unit))
        yield
    for i, unit in enumerate(units):
        if i + lookahead < len(units):
            pending.append(scores(*units[i + lookahead]))
            yield
        outs[unit] = values(*unit, pending.pop(0))
        yield

    lo = lax.broadcasted_iota(jnp.int32, (2 * q_block, LANES), 1) < HEAD_DIM
    rows = []
    for r in range(0, tq, q_block):
        blocks = []
        for g in range(N_KV_HEADS):
            o_even, o_odd = outs[(r, g, 0)], outs[(r, g, 1)]
            num = jnp.where(lo, o_even, o_odd)
            den = jnp.where(lo, pltpu.roll(o_even, HEAD_DIM, 1), pltpu.roll(o_odd, HEAD_DIM, 1))
            out = num / den
            blocks.extend([out[0:q_block], out[q_block:2 * q_block]])
        rows.append(jnp.concatenate(blocks, axis=1))
    return rows[0] if len(rows) == 1 else jnp.concatenate(rows, axis=0)


def _pool_mix(u, t_loc, seq_len, wpool_ref, lo_row, n_rows):
    m = u.shape[0]
    outs = []
    for gi, w in enumerate(POOL_WINDOWS):
        ug = u[:, gi * LANES:(gi + 1) * LANES]
        half = w // 2
        past, future, d = ug, ug, 1
        while d < half:
            past = past + jnp.where(t_loc >= d, pltpu.roll(past, d, 0), 0.0)
            future = future + jnp.where(t_loc + d < seq_len, pltpu.roll(future, m - d, 0), 0.0)
            d *= 2
        total = jnp.where(t_loc >= 1, pltpu.roll(past, 1, 0), 0.0) + future
        cnt = jnp.minimum(t_loc + half, seq_len) - jnp.maximum(t_loc - half, 0)
        pooled = total / cnt.astype(F32) - ug
        pooled = pooled[lo_row:lo_row + n_rows]
        outs.append(_dot(pooled.astype(BF16), wpool_ref[gi]))
    return jnp.concatenate(outs, axis=1)


def _mix_residual_norm2(x, attn, mixed, wout_ref, mod_ref, n2g_ref, x1_out, h2_out, rows=slice(None)):
    gt1, sh2, sc2 = mod_ref[2:3, :], mod_ref[3:4, :], mod_ref[4:5, :]
    mix = (_dot(attn.astype(BF16), wout_ref[0:ATTN_WIDTH, :])
           + _dot(mixed.astype(BF16), wout_ref[ATTN_WIDTH:D_MODEL, :]))
    x1 = x + gt1 * mix
    x1_out[rows, :] = x1
    h2_out[rows, :] = _rms_mod(x1, n2g_ref[...], 1.0 + sc2, sh2).astype(BF16)


def _cast_specs(weights, layer, n_steps, step_index):
    in_specs, out_specs, out_shapes = [], [], []
    for w in weights:
        _, rows, cols = w.shape
        blk = rows // n_steps
        assert blk * n_steps == rows and blk % BF16_SUBLANES == 0
        in_specs.append(pl.BlockSpec((None, blk, cols), lambda *g: (layer, step_index(*g), 0)))
        out_specs.append(pl.BlockSpec((blk, cols), lambda *g: (step_index(*g), 0)))
        out_shapes.append(jax.ShapeDtypeStruct((rows, cols), BF16))
    return in_specs, out_specs, out_shapes


def _cast_blocks(src_refs, dst_refs):
    for src, dst in zip(src_refs, dst_refs, strict=True):
        dst[...] = src[...].astype(BF16)


def _mod_kernel(cond_ref, w_ref, b_ref, out_ref):
    s = jax.nn.silu(cond_ref[...]).astype(BF16)
    out_ref[...] = _dot(s, w_ref[...].astype(BF16)) + b_ref[...]


def _modulation(cond8, w_mod, b_mod):
    tn = 1536
    n = N_MOD * D_MODEL
    return pl.pallas_call(
        _mod_kernel,
        out_shape=jax.ShapeDtypeStruct((DEPTH, 8, n), F32),
        grid=(DEPTH, n // tn),
        in_specs=[
            pl.BlockSpec((8, D_MODEL), lambda l, j: (0, 0)),
            pl.BlockSpec((None, D_MODEL, tn), lambda l, j: (l, 0, j)),
            pl.BlockSpec((None, 1, tn), lambda l, j: (l, 0, j)),
        ],
        out_specs=pl.BlockSpec((None, 8, tn), lambda l, j: (l, 0, j)),
        compiler_params=pltpu.CompilerParams(
            dimension_semantics=("arbitrary", "arbitrary"), vmem_limit_bytes=VMEM_LIMIT),
        name="modulation",
    )(cond8, w_mod, b_mod.reshape(DEPTH, 1, n))


def _ctx_front_kernel(*refs, seq_len, n_cast):
    (x_ref, mod_ref, n1g_ref, win_ref, bd_ref, qg_ref, kg_ref, wpool_ref, pscale_ref, wout_ref,
     n2g_ref) = refs[:11]
    cast_src = refs[11:11 + n_cast]
    kt_out, vt_out, x1_out, h2_out = refs[11 + n_cast:15 + n_cast]
    _cast_blocks(cast_src, refs[15 + n_cast:])
    tm = x_ref.shape[0]
    sh1, sc1 = mod_ref[0:1, :], mod_ref[1:2, :]
    bd = bd_ref[...]
    t_loc = lax.broadcasted_iota(jnp.int32, (seq_len, LANES), 0)

    def sequence(s):
        rows = slice(s * seq_len, (s + 1) * seq_len)
        x = x_ref[rows, :]
        h = _rms_mod(x, n1g_ref[...], 1.0 + sc1, sh1).astype(BF16)
        proj = _dot(h, win_ref[...])
        yield
        q = (_head_norm(proj[:, 0:ATTN_WIDTH], bd, qg_ref[...])
             * (HEAD_DIM ** -0.5 * LOG2_E)).astype(BF16)
        k = _head_norm(proj[:, ATTN_WIDTH:ATTN_WIDTH + KV_WIDTH], bd[0:KV_WIDTH, 0:KV_WIDTH],
                       kg_ref[...])
        yield
        v = proj[:, ATTN_WIDTH + KV_WIDTH:ATTN_WIDTH + 2 * KV_WIDTH]
        u = proj[:, ATTN_WIDTH + 2 * KV_WIDTH:IN_WIDTH]
        kt_out[s] = k.T
        vt = v.T
        vt_out[s] = vt
        attn = yield from _attend_keys_major(q, _dup_halves(k), _value_rows(vt), seq_len,
                                             CTX_SCORE_LOOKAHEAD)
        mixed = _pool_mix(u, t_loc, seq_len, wpool_ref, 0, seq_len) * pscale_ref[...]
        yield
        _mix_residual_norm2(x, attn, mixed, wout_ref, mod_ref, n2g_ref, x1_out, h2_out, rows)

    _interleave([sequence(s) for s in range(tm // seq_len)], CTX_SEQ_SKEW)


def _ctx_front(layer, xp, seq_len, mods, norm1_g, w_in, bd, qg, kg, w_pool, pool_scale, w_out, norm2_g,
               cast_weights):
    n_tok = xp.shape[0]
    tm = CTX_TILE
    const2 = lambda i: (0, 0)
    tile = lambda i: (i, 0)
    cast_in, cast_out, cast_shapes = _cast_specs(cast_weights, layer, n_tok // tm, lambda i: i)
    return pl.pallas_call(
        functools.partial(_ctx_front_kernel, seq_len=seq_len, n_cast=len(cast_weights)),
        out_shape=(
            jax.ShapeDtypeStruct((n_tok // seq_len, KV_WIDTH, seq_len), F32),
            jax.ShapeDtypeStruct((n_tok // seq_len, KV_WIDTH, seq_len), F32),
            jax.ShapeDtypeStruct((n_tok, D_MODEL), F32),
            jax.ShapeDtypeStruct((n_tok, D_MODEL), BF16),
            *cast_shapes,
        ),
        grid=(n_tok // tm,),
        in_specs=[
            pl.BlockSpec((tm, D_MODEL), tile),
            pl.BlockSpec((None, None, N_MOD, D_MODEL), lambda i: (layer, 0, 0, 0)),
            pl.BlockSpec((1, D_MODEL), const2),
            pl.BlockSpec((D_MODEL, IN_WIDTH), const2),
            pl.BlockSpec((ATTN_WIDTH, ATTN_WIDTH), const2),
            pl.BlockSpec((1, ATTN_WIDTH), const2),
            pl.BlockSpec((1, KV_WIDTH), const2),
            pl.BlockSpec((None, 4, POOL_GROUP_DIM, POOL_GROUP_DIM), lambda i: (layer, 0, 0, 0)),
            pl.BlockSpec((1, POOL_WIDTH), const2),
            pl.BlockSpec((D_MODEL, D_MODEL), const2),
            pl.BlockSpec((1, D_MODEL), const2),
            *cast_in,
        ],
        out_specs=(
            pl.BlockSpec((tm // seq_len, KV_WIDTH, seq_len), lambda i: (i, 0, 0)),
            pl.BlockSpec((tm // seq_len, KV_WIDTH, seq_len), lambda i: (i, 0, 0)),
            pl.BlockSpec((tm, D_MODEL), tile),
            pl.BlockSpec((tm, D_MODEL), tile),
            *cast_out,
        ),
        compiler_params=pltpu.CompilerParams(
            dimension_semantics=("arbitrary",), vmem_limit_bytes=VMEM_LIMIT),
        name=f"ctx_front_{layer}",
    )(xp, mods, norm1_g, w_in, bd, qg, kg, w_pool, pool_scale, w_out, norm2_g, *cast_weights)


def _lat_proj_kernel(xprev_ref, x_ref, xnext_ref, mod_ref, n1g_ref, win_ref, bd_ref, qg_ref, kg_ref,
                     cos_ref, sin_ref, wpool_ref, pscale_ref, q_out, k_out, v_out, mixed_out, *, seq_len):
    tm = x_ref.shape[0]
    halo = POOL_HALO
    xe = jnp.concatenate([xprev_ref[...], x_ref[...], xnext_ref[...]], axis=0)
    sh1, sc1 = mod_ref[0:1, :], mod_ref[1:2, :]
    h = _rms_mod(xe, n1g_ref[...], 1.0 + sc1, sh1).astype(BF16)
    proj = _dot(h, win_ref[...])
    main = proj[halo:halo + tm]
    bd = bd_ref[...]
    cos_t, sin_t = cos_ref[...], sin_ref[...]
    q = _head_norm(main[:, 0:ATTN_WIDTH], bd, qg_ref[...])
    q_out[...] = (_rope(q, cos_t, sin_t) * (HEAD_DIM ** -0.5 * LOG2_E)).astype(BF16)
    k = _head_norm(main[:, ATTN_WIDTH:ATTN_WIDTH + KV_WIDTH], bd[0:KV_WIDTH, 0:KV_WIDTH], kg_ref[...])
    k_out[...] = _rope(k, cos_t, sin_t).astype(BF16)
    v_out[...] = main[:, ATTN_WIDTH + KV_WIDTH:ATTN_WIDTH + 2 * KV_WIDTH].astype(BF16)
    u = proj[:, ATTN_WIDTH + 2 * KV_WIDTH:IN_WIDTH]
    t_loc = (lax.broadcasted_iota(jnp.int32, (tm + 2 * halo, LANES), 0)
             + (pl.program_id(1) * tm - halo))
    mixed = _pool_mix(u, t_loc, seq_len, wpool_ref, halo, tm) * pscale_ref[...]
    mixed_out[...] = mixed.astype(BF16)


def _lat_proj(layer, xs, mods, norm1_g, w_in, bd, qg, kg, cos_t, sin_t, w_pool, pool_scale):
    nb, seq_len, _ = xs.shape
    tm = LAT_TILE
    halo = POOL_HALO
    n_halo_blocks = seq_len // halo
    const2 = lambda b, i: (0, 0)
    tile = lambda b, i: (b, i, 0)
    return pl.pallas_call(
        functools.partial(_lat_proj_kernel, seq_len=seq_len),
        out_shape=(
            jax.ShapeDtypeStruct((nb, seq_len, ATTN_WIDTH), BF16),
            jax.ShapeDtypeStruct((nb, seq_len, KV_WIDTH), BF16),
            jax.ShapeDtypeStruct((nb, seq_len, KV_WIDTH), BF16),
            jax.ShapeDtypeStruct((nb, seq_len, POOL_WIDTH), BF16),
        ),
        grid=(nb, seq_len // tm),
        in_specs=[
            pl.BlockSpec((None, halo, D_MODEL),
                         lambda b, i: (b, jnp.maximum(i * (tm // halo) - 1, 0), 0)),
            pl.BlockSpec((None, tm, D_MODEL), tile),
            pl.BlockSpec((None, halo, D_MODEL),
                         lambda b, i: (b, jnp.minimum((i + 1) * (tm // halo), n_halo_blocks - 1), 0)),
            pl.BlockSpec((None, None, N_MOD, D_MODEL), lambda b, i: (layer, 1 + b, 0, 0)),
            pl.BlockSpec((1, D_MODEL), const2),
            pl.BlockSpec((D_MODEL, IN_WIDTH), const2),
            pl.BlockSpec((ATTN_WIDTH, ATTN_WIDTH), const2),
            pl.BlockSpec((1, ATTN_WIDTH), const2),
            pl.BlockSpec((1, KV_WIDTH), const2),
            pl.BlockSpec((tm, LANES), lambda b, i: (i, 0)),
            pl.BlockSpec((tm, LANES), lambda b, i: (i, 0)),
            pl.BlockSpec((None, 4, POOL_GROUP_DIM, POOL_GROUP_DIM), lambda b, i: (layer, 0, 0, 0)),
            pl.BlockSpec((1, POOL_WIDTH), const2),
        ],
        out_specs=(
            pl.BlockSpec((None, tm, ATTN_WIDTH), tile),
            pl.BlockSpec((None, tm, KV_WIDTH), tile),
            pl.BlockSpec((None, tm, KV_WIDTH), tile),
            pl.BlockSpec((None, tm, POOL_WIDTH), tile),
        ),
        compiler_params=pltpu.CompilerParams(
            dimension_semantics=("arbitrary", "arbitrary"), vmem_limit_bytes=VMEM_LIMIT),
        name=f"lat_proj_{layer}",
    )(xs, xs, xs, mods, norm1_g, w_in, bd, qg, kg, cos_t, sin_t, w_pool, pool_scale)


def _lat_attn_kernel(*refs, n_cast):
    (q_ref, klat_ref, vlat_ref, ck_ref, cv_ref, mixed_ref, x_ref, mod_ref, wout_ref,
     n2g_ref) = refs[:10]
    cast_src = refs[10:10 + n_cast]
    x1_out, h2_out = refs[10 + n_cast:12 + n_cast]
    cast_dst = refs[12 + n_cast:12 + 2 * n_cast]
    kdup_scr, vt_scr = refs[12 + 2 * n_cast:]
    _cast_blocks(cast_src, cast_dst)

    @pl.when(pl.program_id(1) == 0)
    def _():
        k_all = jnp.concatenate([ck_ref[...].T, klat_ref[...].astype(F32)], axis=0)
        vt_all = jnp.concatenate([cv_ref[...], vlat_ref[...].astype(F32).T], axis=1)
        k0, k1 = _dup_halves(k_all)
        kdup_scr[0] = k0
        kdup_scr[1] = k1
        for i, vt in enumerate(_value_rows(vt_all)):
            vt_scr[i] = vt

    attn = _run(_attend_keys_major(q_ref[...], kdup_scr, vt_scr, LAT_Q_BLOCK, LAT_SCORE_LOOKAHEAD))
    _mix_residual_norm2(x_ref[...], attn, mixed_ref[...], wout_ref, mod_ref, n2g_ref, x1_out, h2_out)


def _lat_attn(layer, q, k, v, cache_k, cache_v, mixed, xs, mods, w_out, norm2_g, cast_weights, cast_layer):
    nb, seq_len, _ = xs.shape
    past = cache_k.shape[3]
    tq = LAT_Q_TILE
    n_keys = past + seq_len
    n_tiles = seq_len // tq
    tile = lambda b, j: (b, j, 0)
    whole = lambda b, j: (b, 0, 0)
    cast_in, cast_out, cast_shapes = _cast_specs(cast_weights, cast_layer, nb * n_tiles,
                                                 lambda b, j: b * n_tiles + j)
    return pl.pallas_call(
        functools.partial(_lat_attn_kernel, n_cast=len(cast_weights)),
        out_shape=(
            jax.ShapeDtypeStruct((nb, seq_len, D_MODEL), F32),
            jax.ShapeDtypeStruct((nb, seq_len, D_MODEL), BF16),
            *cast_shapes,
        ),
        grid=(nb, seq_len // tq),
        in_specs=[
            pl.BlockSpec((None, tq, ATTN_WIDTH), tile),
            pl.BlockSpec((None, seq_len, KV_WIDTH), whole),
            pl.BlockSpec((None, seq_len, KV_WIDTH), whole),
            pl.BlockSpec((None, None, KV_WIDTH, past), lambda b, j: (b, layer, 0, 0)),
            pl.BlockSpec((None, None, KV_WIDTH, past), lambda b, j: (b, layer, 0, 0)),
            pl.BlockSpec((None, tq, POOL_WIDTH), tile),
            pl.BlockSpec((None, tq, D_MODEL), tile),
            pl.BlockSpec((None, None, N_MOD, D_MODEL), lambda b, j: (layer, 1 + b, 0, 0)),
            pl.BlockSpec((D_MODEL, D_MODEL), lambda b, j: (0, 0)),
            pl.BlockSpec((1, D_MODEL), lambda b, j: (0, 0)),
            *cast_in,
        ],
        out_specs=(
            pl.BlockSpec((None, tq, D_MODEL), tile),
            pl.BlockSpec((None, tq, D_MODEL), tile),
            *cast_out,
        ),
        scratch_shapes=[
            pltpu.VMEM((N_KV_HEADS, n_keys, LANES), BF16),
            pltpu.VMEM((2 * N_KV_HEADS, LANES, n_keys), BF16),
        ],
        compiler_params=pltpu.CompilerParams(
            dimension_semantics=("arbitrary", "arbitrary"), vmem_limit_bytes=VMEM_LIMIT),
        name=f"lat_attn_{layer}",
    )(q, k, v, cache_k, cache_v, mixed, xs, mods, w_out, norm2_g, *cast_weights)


def _ffn_kernel(*refs, seq_len, final_norm, has_halo):
    if has_halo:
        (x1_ref, h2_ref, h2prev_ref, h2next_ref, mod_ref, wup_ref, cw_ref, cb_ref, wdn_ref, fg_ref,
         out_ref, z_scr, act_scr) = refs
    else:
        (x1_ref, h2_ref, mod_ref, wup_ref, cw_ref, cb_ref, wdn_ref, fg_ref,
         out_ref, z_scr, act_scr) = refs
    tm = x1_ref.shape[0]
    gap = F32_SUBLANES
    if has_halo:
        halo = BF16_SUBLANES
        tile_start = pl.program_id(1) * tm
        h2prev = jnp.where(tile_start % seq_len != 0, h2prev_ref[...], jnp.zeros_like(h2prev_ref))
        h2next = jnp.where((tile_start + tm) % seq_len != 0, h2next_ref[...], jnp.zeros_like(h2next_ref))
        he = jnp.concatenate([h2prev, h2_ref[...], h2next], axis=0)
        segments = [(halo, tm)]
    else:
        he = h2_ref[...]
        segments = [(gap + s * (seq_len + gap), seq_len) for s in range(tm // seq_len)]
        for slab in range(z_scr.shape[0]):
            for s in range(tm // seq_len + 1):
                r = s * (seq_len + gap)
                z_scr[slab, r:r + gap, :] = jnp.zeros((gap, LANES), F32)

    def store_z(slab, z):
        if has_halo:
            z_scr[slab, 0:z.shape[0], :] = z
        else:
            for s, (row0, n) in enumerate(segments):
                z_scr[slab, row0:row0 + n, :] = z[s * n:(s + 1) * n]

    def conv_rows(slab, row, n, col0):
        cols = slice(col0, col0 + LANES)
        return (z_scr[slab, pl.ds(row - 1, n, stride=1), :] * cw_ref[0:1, cols]
                + z_scr[slab, row:row + n, :] * cw_ref[1:2, cols]
                + z_scr[slab, pl.ds(row + 1, n, stride=1), :] * cw_ref[2:3, cols]
                + cb_ref[0:1, cols])

    n_sub = FFN_CHUNK // LANES
    for ci in range(D_FF // FFN_CHUNK):
        c0 = ci * FFN_CHUNK
        slab0 = 2 * n_sub * (ci % 2)
        za = _dot(he, wup_ref[:, c0:c0 + FFN_CHUNK])
        zg = _dot(he, wup_ref[:, D_FF + c0:D_FF + c0 + FFN_CHUNK])
        for j in range(n_sub):
            lanes = slice(j * LANES, (j + 1) * LANES)
            slab_a, slab_g = slab0 + 2 * j, slab0 + 2 * j + 1
            col = c0 + j * LANES
            store_z(slab_a, za[:, lanes])
            store_z(slab_g, zg[:, lanes])
            for s, (row0, n) in enumerate(segments):
                a = conv_rows(slab_a, row0, n, col)
                g = conv_rows(slab_g, row0, n, D_FF + col)
                act_scr[s * n:(s + 1) * n, col:col + LANES] = (jax.nn.silu(a) * g).astype(BF16)
    x2 = x1_ref[...] + mod_ref[5:6, :] * _dot(act_scr[...], wdn_ref[...])
    if final_norm:
        ms = jnp.mean(x2 * x2, axis=-1, keepdims=True)
        x2 = x2 * lax.rsqrt(ms + EPS) * fg_ref[...]
    out_ref[...] = x2


def _ffn(layer, x1, h2, seq_len, mod_row0, mods, w_up, conv_w, conv_b, w_down, final_g, final_norm):
    nb, n_tok, _ = x1.shape
    tm = FFN_TILE
    has_halo = seq_len > tm
    assert seq_len % tm == 0 or tm % seq_len == 0
    halo = BF16_SUBLANES
    gap = F32_SUBLANES
    n_halo_blocks = n_tok // halo
    tile = lambda b, i: (b, i, 0)
    halo_specs = [
        pl.BlockSpec((None, halo, D_MODEL),
                     lambda b, i: (b, jnp.maximum(i * (tm // halo) - 1, 0), 0)),
        pl.BlockSpec((None, halo, D_MODEL),
                     lambda b, i: (b, jnp.minimum((i + 1) * (tm // halo), n_halo_blocks - 1), 0)),
    ] if has_halo else []
    halo_args = [h2, h2] if has_halo else []
    z_rows = tm + 2 * halo if has_halo else gap + (tm // seq_len) * (seq_len + gap)
    return pl.pallas_call(
        functools.partial(_ffn_kernel, seq_len=seq_len, final_norm=final_norm, has_halo=has_halo),
        out_shape=jax.ShapeDtypeStruct((nb, n_tok, D_MODEL), F32),
        grid=(nb, n_tok // tm),
        in_specs=[
            pl.BlockSpec((None, tm, D_MODEL), tile),
            pl.BlockSpec((None, tm, D_MODEL), tile),
            *halo_specs,
            pl.BlockSpec((None, None, N_MOD, D_MODEL), lambda b, i: (layer, mod_row0 + b, 0, 0)),
            pl.BlockSpec((D_MODEL, 2 * D_FF), lambda b, i: (0, 0), pipeline_mode=pl.Buffered(1)),
            pl.BlockSpec((None, 3, 2 * D_FF), lambda b, i: (layer, 0, 0)),
            pl.BlockSpec((None, 1, 2 * D_FF), lambda b, i: (layer, 0, 0)),
            pl.BlockSpec((D_FF, D_MODEL), lambda b, i: (0, 0), pipeline_mode=pl.Buffered(1)),
            pl.BlockSpec((1, D_MODEL), lambda b, i: (0, 0)),
        ],
        out_specs=pl.BlockSpec((None, tm, D_MODEL), tile),
        scratch_shapes=[
            pltpu.VMEM((4 * (FFN_CHUNK // LANES), z_rows, LANES), F32),
            pltpu.VMEM((tm, D_FF), BF16),
        ],
        compiler_params=pltpu.CompilerParams(
            dimension_semantics=("arbitrary", "arbitrary"), vmem_limit_bytes=VMEM_LIMIT),
        name=f"ffn_{'lat' if nb > 1 else 'ctx'}_{layer}",
    )(x1, h2, *halo_args, mods, w_up, conv_w, conv_b, w_down, final_g)


def _rope_tables(n_tokens):
    t = jnp.arange(n_tokens)
    row = (t // GRID_W).astype(F32)
    col = (t % GRID_W).astype(F32)
    n_freq = HEAD_DIM // 4
    inv = ROPE_THETA ** (-jnp.arange(n_freq, dtype=F32) / n_freq)
    ang = jnp.stack([row[:, None] * inv, col[:, None] * inv], axis=1)
    cos = jnp.broadcast_to(jnp.cos(ang)[:, :, None, :], (n_tokens, 2, 2, n_freq))
    sin = jnp.sin(ang)[:, :, None, :] * jnp.array([-1.0, 1.0], F32)[None, None, :, None]
    cos = cos.reshape(n_tokens, HEAD_DIM)
    sin = sin.reshape(n_tokens, HEAD_DIM)
    return jnp.tile(cos, (1, 2)), jnp.tile(sin, (1, 2))


def kernel(x_prompt, x_sample, cache_k, cache_v, c, c_ctx, w_mod, b_mod, norm1_g, w_in, q_norm_g, k_norm_g, w_pool, pool_scale, w_out, norm2_g, w_up, conv_w, conv_b, w_down, final_norm_g):
    batch, seq, d = x_prompt.shape
    dec_batch, dec_seq, _ = x_sample.shape
    past = cache_k.shape[2]

    w_in_l = w_in[0].astype(BF16)
    w_out_l = w_out[0].astype(BF16)
    w_pool_b = w_pool.astype(BF16)

    cond8 = jnp.zeros((8, d), F32).at[0].set(c_ctx).at[1:1 + dec_batch].set(c)
    mods = _modulation(cond8, w_mod, b_mod).reshape(DEPTH, 8, N_MOD, d)

    head_id = jnp.arange(ATTN_WIDTH) // HEAD_DIM
    blockdiag = (head_id[:, None] == head_id[None, :]).astype(BF16)
    cos_t, sin_t = _rope_tables(dec_seq)
    ck = jnp.transpose(cache_k, (0, 1, 3, 4, 2)).reshape(dec_batch, DEPTH, KV_WIDTH, past)
    cv = jnp.transpose(cache_v, (0, 1, 3, 4, 2)).reshape(dec_batch, DEPTH, KV_WIDTH, past)
    final_g = final_norm_g.reshape(1, d)

    xp = x_prompt.reshape(batch * seq, d)
    xs = x_sample
    ks_out, vs_out = [], []
    for l in range(DEPTH):
        n1g = norm1_g[l].reshape(1, d)
        n2g = norm2_g[l].reshape(1, d)
        qg = jnp.tile(q_norm_g[l], N_HEADS).reshape(1, ATTN_WIDTH)
        kg = jnp.tile(k_norm_g[l], N_KV_HEADS).reshape(1, KV_WIDTH)
        ps = pool_scale[l].reshape(1, POOL_WIDTH)
        cb = conv_b.reshape(DEPTH, 1, 2 * D_FF)
        last = l == DEPTH - 1

        k_l, v_l, x1p, h2p, w_up_l, w_down_l = _ctx_front(
            l, xp, seq, mods, n1g, w_in_l, blockdiag, qg, kg, w_pool_b, ps, w_out_l, n2g,
            (w_up, w_down))
        ks_out.append(k_l)
        vs_out.append(v_l)
        xp = _ffn(l, x1p[None], h2p[None], seq, 0, mods, w_up_l, conv_w, cb, w_down_l,
                  final_g, last)[0]

        q, k, v, mixed = _lat_proj(l, xs, mods, n1g, w_in_l, blockdiag, qg, kg, cos_t, sin_t,
                                   w_pool_b, ps)
        x1s, h2s, *next_weights = _lat_attn(l, q, k, v, ck, cv, mixed, xs, mods, w_out_l, n2g,
                                            () if last else (w_in, w_out), l + 1)
        xs = _ffn(l, x1s, h2s, dec_seq, 1, mods, w_up_l, conv_w, cb, w_down_l, final_g, last)
        if not last:
            w_in_l, w_out_l = next_weights

    y_prompt = xp.reshape(batch, seq, d)

    def cache_layout(per_layer):
        t = jnp.stack(per_layer, axis=1).reshape(batch, DEPTH, N_KV_HEADS, HEAD_DIM, seq)
        return jnp.transpose(t, (0, 1, 4, 2, 3))

    new_cache_k = cache_layout(ks_out)
    new_cache_v = cache_layout(vs_out)
    return (y_prompt, xs, new_cache_k, new_cache_v)
```

```python
import functools

import jax
import jax.numpy as jnp
from jax import lax
from jax.experimental import pallas as pl
from jax.experimental.pallas import tpu as pltpu

D_MODEL = 1024
DEPTH = 4
GRID_W = 64
HEAD_DIM = 64
N_HEADS = 8
N_KV_HEADS = 2
ATTN_WIDTH = N_HEADS * HEAD_DIM
KV_WIDTH = N_KV_HEADS * HEAD_DIM
POOL_WIDTH = D_MODEL - ATTN_WIDTH
POOL_WINDOWS = (2, 4, 8, 16)
POOL_GROUP_DIM = 128
IN_WIDTH = ATTN_WIDTH + 2 * KV_WIDTH + POOL_WIDTH
D_FF = 2816
ROPE_THETA = 10000.0
EPS = 1e-6
N_MOD = 6
LOG2_E = 1.4426950408889634

LANES = 128
F32_SUBLANES = 8
BF16_SUBLANES = 16
VMEM_LIMIT = 56 * 1024 * 1024

POOL_HALO = 8
FFN_CHUNK = 256
CTX_TILE = 1024
LAT_TILE = 1024
LAT_SUB_TILE = 512
LAT_SUB_SKEW = 1
LAT_Q_TILE = 512
LAT_Q_BLOCK = 128
FFN_TILE = 512
LAT_SCORE_LOOKAHEAD = 4
CTX_SCORE_LOOKAHEAD = 2
CTX_SEQ_SKEW = 1

BF16 = jnp.bfloat16
F32 = jnp.float32


def _dot(a, b):
    return jnp.dot(a, b, preferred_element_type=F32)


def _dot_nt(a, b):
    return lax.dot_general(a, b, (((1,), (1,)), ((), ())), preferred_element_type=F32)


def _rms_mod(x, g, scale1p, shift):
    ms = jnp.mean(x * x, axis=-1, keepdims=True)
    return (x * lax.rsqrt(ms + EPS) * g) * scale1p + shift


def _head_norm(t, blockdiag, g_tiled):
    ssq = _dot((t * t).astype(BF16), blockdiag)
    return t * lax.rsqrt(ssq * (1.0 / HEAD_DIM) + EPS) * g_tiled


def _rope(t, cos_t, sin_t):
    lane = lax.broadcasted_iota(jnp.int32, (t.shape[0], LANES), 1)
    first = (lane % 32) < 16
    outs = []
    for ci in range(t.shape[1] // LANES):
        tc = t[:, ci * LANES:(ci + 1) * LANES]
        partner = jnp.where(first, pltpu.roll(tc, LANES - 16, 1), pltpu.roll(tc, 16, 1))
        outs.append(tc * cos_t + partner * sin_t)
    return outs[0] if len(outs) == 1 else jnp.concatenate(outs, axis=1)


def _dup_halves(t):
    lane = lax.broadcasted_iota(jnp.int32, t.shape, 1)
    lo = lane < HEAD_DIM
    sw = pltpu.roll(t, HEAD_DIM, 1)
    return jnp.where(lo, t, sw).astype(BF16), jnp.where(lo, sw, t).astype(BF16)


def _value_rows(vt):
    ones = jnp.ones((HEAD_DIM, vt.shape[1]), F32)
    out = []
    for g in range(N_KV_HEADS):
        vg = vt[g * HEAD_DIM:(g + 1) * HEAD_DIM]
        out.append(jnp.concatenate([vg, ones], axis=0).astype(BF16))
        out.append(jnp.concatenate([ones, vg], axis=0).astype(BF16))
    return out


def _interleave(chains, skew):
    live = list(enumerate(chains))
    t = 0
    while live:
        for i, chain in list(live):
            if t >= i * skew and next(chain, StopIteration) is StopIteration:
                live.remove((i, chain))
        t += 1


def _run(chain):
    while True:
        try:
            next(chain)
        except StopIteration as stop:
            return stop.value


def _attend_keys_major(q, kdup_ref, vt_ref, q_block, lookahead):
    tq = q.shape[0]
    lane = lax.broadcasted_iota(jnp.int32, (1, LANES), 1)
    masks = ((lane < HEAD_DIM).astype(BF16), (lane >= HEAD_DIM).astype(BF16))
    units = [(r, g, half) for r in range(0, tq, q_block) for g in range(N_KV_HEADS) for half in range(2)]

    def scores(r, g, half):
        q_a = q[r:r + q_block, (2 * g) * LANES:(2 * g + 1) * LANES]
        q_b = q[r:r + q_block, (2 * g + 1) * LANES:(2 * g + 2) * LANES]
        q_rows = jnp.concatenate([q_a * masks[half], q_b * masks[half]], axis=0)
        return _dot_nt(kdup_ref[g], q_rows)

    def values(r, g, half, st):
        pt = jnp.exp2(st - jnp.max(st, axis=0, keepdims=True)).astype(BF16)
        return _dot(vt_ref[2 * g + half], pt).T

    outs = {}
    pending = []
    for unit in units[:lookahead]:
        pending.append(scores(*unit))
        yield
    for i, unit in enumerate(units):
        if i + lookahead < len(units):
            pending.append(scores(*units[i + lookahead]))
            yield
        outs[unit] = values(*unit, pending.pop(0))
        yield

    lo = lax.broadcasted_iota(jnp.int32, (2 * q_block, LANES), 1) < HEAD_DIM
    rows = []
    for r in range(0, tq, q_block):
        blocks = []
        for g in range(N_KV_HEADS):
            o_even, o_odd = outs[(r, g, 0)], outs[(r, g, 1)]
            num = jnp.where(lo, o_even, o_odd)
            den = jnp.where(lo, pltpu.roll(o_even, HEAD_DIM, 1), pltpu.roll(o_odd, HEAD_DIM, 1))
            out = num / den
            blocks.extend([out[0:q_block], out[q_block:2 * q_block]])
        rows.append(jnp.concatenate(blocks, axis=1))
    return rows[0] if len(rows) == 1 else jnp.concatenate(rows, axis=0)


def _pool_mix(u, t_loc, seq_len, wpool_ref, lo_row, n_rows):
    m = u.shape[0]
    outs = []
    for gi, w in enumerate(POOL_WINDOWS):
        ug = u[:, gi * LANES:(gi + 1) * LANES]
        half = w // 2
        past, future, d = ug, ug, 1
        while d < half:
            past = past + jnp.where(t_loc >= d, pltpu.roll(past, d, 0), 0.0)
            future = future + jnp.where(t_loc + d < seq_len, pltpu.roll(future, m - d, 0), 0.0)
            d *= 2
        total = jnp.where(t_loc >= 1, pltpu.roll(past, 1, 0), 0.0) + future
        cnt = jnp.minimum(t_loc + half, seq_len) - jnp.maximum(t_loc - half, 0)
        pooled = total / cnt.astype(F32) - ug
        pooled = pooled[lo_row:lo_row + n_rows]
        outs.append(_dot(pooled.astype(BF16), wpool_ref[gi]))
    return jnp.concatenate(outs, axis=1)


def _mix_residual_norm2(x, attn, mixed, wout_ref, mod_ref, n2g_ref, x1_out, h2_out, rows=slice(None)):
    gt1, sh2, sc2 = mod_ref[2:3, :], mod_ref[3:4, :], mod_ref[4:5, :]
    mix = (_dot(attn.astype(BF16), wout_ref[0:ATTN_WIDTH, :])
           + _dot(mixed.astype(BF16), wout_ref[ATTN_WIDTH:D_MODEL, :]))
    x1 = x + gt1 * mix
    x1_out[rows, :] = x1
    h2_out[rows, :] = _rms_mod(x1, n2g_ref[...], 1.0 + sc2, sh2).astype(BF16)


def _cast_specs(weights, layer, n_steps, step_index):
    in_specs, out_specs, out_shapes = [], [], []
    for w in weights:
        _, rows, cols = w.shape
        blk = rows // n_steps
        assert blk * n_steps == rows and blk % BF16_SUBLANES == 0
        in_specs.append(pl.BlockSpec((None, blk, cols), lambda *g: (layer, step_index(*g), 0)))
        out_specs.append(pl.BlockSpec((blk, cols), lambda *g: (step_index(*g), 0)))
        out_shapes.append(jax.ShapeDtypeStruct((rows, cols), BF16))
    return in_specs, out_specs, out_shapes


def _cast_blocks(src_refs, dst_refs):
    for src, dst in zip(src_refs, dst_refs, strict=True):
        dst[...] = src[...].astype(BF16)


def _mod_kernel(cond_ref, w_ref, b_ref, out_ref):
    s = jax.nn.silu(cond_ref[...]).astype(BF16)
    out_ref[...] = _dot(s, w_ref[...].astype(BF16)) + b_ref[...]


def _modulation(cond8, w_mod, b_mod):
    tn = 1536
    n = N_MOD * D_MODEL
    return pl.pallas_call(
        _mod_kernel,
        out_shape=jax.ShapeDtypeStruct((DEPTH, 8, n), F32),
        grid=(DEPTH, n // tn),
        in_specs=[
            pl.BlockSpec((8, D_MODEL), lambda l, j: (0, 0)),
            pl.BlockSpec((None, D_MODEL, tn), lambda l, j: (l, 0, j)),
            pl.BlockSpec((None, 1, tn), lambda l, j: (l, 0, j)),
        ],
        out_specs=pl.BlockSpec((None, 8, tn), lambda l, j: (l, 0, j)),
        compiler_params=pltpu.CompilerParams(
            dimension_semantics=("arbitrary", "arbitrary"), vmem_limit_bytes=VMEM_LIMIT),
        name="modulation",
    )(cond8, w_mod, b_mod.reshape(DEPTH, 1, n))


def _ctx_front_kernel(*refs, seq_len, n_cast):
    (x_ref, mod_ref, n1g_ref, win_ref, bd_ref, qg_ref, kg_ref, wpool_ref, pscale_ref, wout_ref,
     n2g_ref) = refs[:11]
    cast_src = refs[11:11 + n_cast]
    kt_out, vt_out, x1_out, h2_out = refs[11 + n_cast:15 + n_cast]
    _cast_blocks(cast_src, refs[15 + n_cast:])
    tm = x_ref.shape[0]
    sh1, sc1 = mod_ref[0:1, :], mod_ref[1:2, :]
    bd = bd_ref[...]
    t_loc = lax.broadcasted_iota(jnp.int32, (seq_len, LANES), 0)

    def sequence(s):
        rows = slice(s * seq_len, (s + 1) * seq_len)
        x = x_ref[rows, :]
        h = _rms_mod(x, n1g_ref[...], 1.0 + sc1, sh1).astype(BF16)
        proj = _dot(h, win_ref[...])
        yield
        q = (_head_norm(proj[:, 0:ATTN_WIDTH], bd, qg_ref[...])
             * (HEAD_DIM ** -0.5 * LOG2_E)).astype(BF16)
        k = _head_norm(proj[:, ATTN_WIDTH:ATTN_WIDTH + KV_WIDTH], bd[0:KV_WIDTH, 0:KV_WIDTH],
                       kg_ref[...])
        yield
        v = proj[:, ATTN_WIDTH + KV_WIDTH:ATTN_WIDTH + 2 * KV_WIDTH]
        u = proj[:, ATTN_WIDTH + 2 * KV_WIDTH:IN_WIDTH]
        kt_out[s] = k.T
        vt = v.T
        vt_out[s] = vt
        attn = yield from _attend_keys_major(q, _dup_halves(k), _value_rows(vt), seq_len,
                                             CTX_SCORE_LOOKAHEAD)
        mixed = _pool_mix(u, t_loc, seq_len, wpool_ref, 0, seq_len) * pscale_ref[...]
        yield
        _mix_residual_norm2(x, attn, mixed, wout_ref, mod_ref, n2g_ref, x1_out, h2_out, rows)

    _interleave([sequence(s) for s in range(tm // seq_len)], CTX_SEQ_SKEW)


def _ctx_front(layer, xp, seq_len, mods, norm1_g, w_in, bd, qg, kg, w_pool, pool_scale, w_out, norm2_g,
               cast_weights):
    n_tok = xp.shape[0]
    tm = CTX_TILE
    const2 = lambda i: (0, 0)
    tile = lambda i: (i, 0)
    cast_in, cast_out, cast_shapes = _cast_specs(cast_weights, layer, n_tok // tm, lambda i: i)
    return pl.pallas_call(
        functools.partial(_ctx_front_kernel, seq_len=seq_len, n_cast=len(cast_weights)),
        out_shape=(
            jax.ShapeDtypeStruct((n_tok // seq_len, KV_WIDTH, seq_len), F32),
            jax.ShapeDtypeStruct((n_tok // seq_len, KV_WIDTH, seq_len), F32),
            jax.ShapeDtypeStruct((n_tok, D_MODEL), F32),
            jax.ShapeDtypeStruct((n_tok, D_MODEL), BF16),
            *cast_shapes,
        ),
        grid=(n_tok // tm,),
        in_specs=[
            pl.BlockSpec((tm, D_MODEL), tile),
            pl.BlockSpec((None, None, N_MOD, D_MODEL), lambda i: (layer, 0, 0, 0)),
            pl.BlockSpec((1, D_MODEL), const2),
            pl.BlockSpec((D_MODEL, IN_WIDTH), const2),
            pl.BlockSpec((ATTN_WIDTH, ATTN_WIDTH), const2),
            pl.BlockSpec((1, ATTN_WIDTH), const2),
            pl.BlockSpec((1, KV_WIDTH), const2),
            pl.BlockSpec((None, 4, POOL_GROUP_DIM, POOL_GROUP_DIM), lambda i: (layer, 0, 0, 0)),
            pl.BlockSpec((1, POOL_WIDTH), const2),
            pl.BlockSpec((D_MODEL, D_MODEL), const2),
            pl.BlockSpec((1, D_MODEL), const2),
            *cast_in,
        ],
        out_specs=(
            pl.BlockSpec((tm // seq_len, KV_WIDTH, seq_len), lambda i: (i, 0, 0)),
            pl.BlockSpec((tm // seq_len, KV_WIDTH, seq_len), lambda i: (i, 0, 0)),
            pl.BlockSpec((tm, D_MODEL), tile),
            pl.BlockSpec((tm, D_MODEL), tile),
            *cast_out,
        ),
        compiler_params=pltpu.CompilerParams(
            dimension_semantics=("arbitrary",), vmem_limit_bytes=VMEM_LIMIT),
        name=f"ctx_front_{layer}",
    )(xp, mods, norm1_g, w_in, bd, qg, kg, w_pool, pool_scale, w_out, norm2_g, *cast_weights)


def _lat_proj_kernel(xprev_ref, x_ref, xnext_ref, mod_ref, n1g_ref, win_ref, bd_ref, qg_ref, kg_ref,
                     cos_ref, sin_ref, wpool_ref, pscale_ref, q_out, k_out, v_out, mixed_out, *, seq_len):
    tm = x_ref.shape[0]
    halo = POOL_HALO
    sub = LAT_SUB_TILE
    sh1, sc1 = mod_ref[0:1, :], mod_ref[1:2, :]
    bd = bd_ref[...]
    tile_start = pl.program_id(1) * tm

    def sub_tile(r0):
        rows = slice(r0, r0 + sub)
        before = xprev_ref[...] if r0 == 0 else x_ref[r0 - halo:r0, :]
        after = xnext_ref[...] if r0 + sub == tm else x_ref[r0 + sub:r0 + sub + halo, :]
        xe = jnp.concatenate([before, x_ref[rows, :], after], axis=0)
        h = _rms_mod(xe, n1g_ref[...], 1.0 + sc1, sh1).astype(BF16)
        proj = _dot(h, win_ref[...])
        yield
        main = proj[halo:halo + sub]
        cos_t, sin_t = cos_ref[rows, :], sin_ref[rows, :]
        q = _head_norm(main[:, 0:ATTN_WIDTH], bd, qg_ref[...])
        q_out[rows, :] = (_rope(q, cos_t, sin_t) * (HEAD_DIM ** -0.5 * LOG2_E)).astype(BF16)
        yield
        k = _head_norm(main[:, ATTN_WIDTH:ATTN_WIDTH + KV_WIDTH], bd[0:KV_WIDTH, 0:KV_WIDTH],
                       kg_ref[...])
        k_out[rows, :] = _rope(k, cos_t, sin_t).astype(BF16)
        v_out[rows, :] = main[:, ATTN_WIDTH + KV_WIDTH:ATTN_WIDTH + 2 * KV_WIDTH].astype(BF16)
        yield
        u = proj[:, ATTN_WIDTH + 2 * KV_WIDTH:IN_WIDTH]
        t_loc = (lax.broadcasted_iota(jnp.int32, (sub + 2 * halo, LANES), 0)
                 + (tile_start + r0 - halo))
        mixed = _pool_mix(u, t_loc, seq_len, wpool_ref, halo, sub) * pscale_ref[...]
        mixed_out[rows, :] = mixed.astype(BF16)

    _interleave([sub_tile(r0) for r0 in range(0, tm, sub)], LAT_SUB_SKEW)


def _lat_proj(layer, xs, mods, norm1_g, w_in, bd, qg, kg, cos_t, sin_t, w_pool, pool_scale):
    nb, seq_len, _ = xs.shape
    tm = LAT_TILE
    halo = POOL_HALO
    n_halo_blocks = seq_len // halo
    const2 = lambda b, i: (0, 0)
    tile = lambda b, i: (b, i, 0)
    return pl.pallas_call(
        functools.partial(_lat_proj_kernel, seq_len=seq_len),
        out_shape=(
            jax.ShapeDtypeStruct((nb, seq_len, ATTN_WIDTH), BF16),
            jax.ShapeDtypeStruct((nb, seq_len, KV_WIDTH), BF16),
            jax.ShapeDtypeStruct((nb, seq_len, KV_WIDTH), BF16),
            jax.ShapeDtypeStruct((nb, seq_len, POOL_WIDTH), BF16),
        ),
        grid=(nb, seq_len // tm),
        in_specs=[
            pl.BlockSpec((None, halo, D_MODEL),
                         lambda b, i: (b, jnp.maximum(i * (tm // halo) - 1, 0), 0)),
            pl.BlockSpec((None, tm, D_MODEL), tile),
            pl.BlockSpec((None, halo, D_MODEL),
                         lambda b, i: (b, jnp.minimum((i + 1) * (tm // halo), n_halo_blocks - 1), 0)),
            pl.BlockSpec((None, None, N_MOD, D_MODEL), lambda b, i: (layer, 1 + b, 0, 0)),
            pl.BlockSpec((1, D_MODEL), const2),
            pl.BlockSpec((D_MODEL, IN_WIDTH), const2),
            pl.BlockSpec((ATTN_WIDTH, ATTN_WIDTH), const2),
            pl.BlockSpec((1, ATTN_WIDTH), const2),
            pl.BlockSpec((1, KV_WIDTH), const2),
            pl.BlockSpec((tm, LANES), lambda b, i: (i, 0)),
            pl.BlockSpec((tm, LANES), lambda b, i: (i, 0)),
            pl.BlockSpec((None, 4, POOL_GROUP_DIM, POOL_GROUP_DIM), lambda b, i: (layer, 0, 0, 0)),
            pl.BlockSpec((1, POOL_WIDTH), const2),
        ],
        out_specs=(
            pl.BlockSpec((None, tm, ATTN_WIDTH), tile),
            pl.BlockSpec((None, tm, KV_WIDTH), tile),
            pl.BlockSpec((None, tm, KV_WIDTH), tile),
            pl.BlockSpec((None, tm, POOL_WIDTH), tile),
        ),
        compiler_params=pltpu.CompilerParams(
            dimension_semantics=("arbitrary", "arbitrary"), vmem_limit_bytes=VMEM_LIMIT),
        name=f"lat_proj_{layer}",
    )(xs, xs, xs, mods, norm1_g, w_in, bd, qg, kg, cos_t, sin_t, w_pool, pool_scale)


def _lat_attn_kernel(*refs, n_cast):
    (q_ref, klat_ref, vlat_ref, ck_ref, cv_ref, mixed_ref, x_ref, mod_ref, wout_ref,
     n2g_ref) = refs[:10]
    cast_src = refs[10:10 + n_cast]
    x1_out, h2_out = refs[10 + n_cast:12 + n_cast]
    cast_dst = refs[12 + n_cast:12 + 2 * n_cast]
    kdup_scr, vt_scr = refs[12 + 2 * n_cast:]
    _cast_blocks(cast_src, cast_dst)

    @pl.when(pl.program_id(1) == 0)
    def _():
        k_all = jnp.concatenate([ck_ref[...].T, klat_ref[...].astype(F32)], axis=0)
        vt_all = jnp.concatenate([cv_ref[...], vlat_ref[...].astype(F32).T], axis=1)
        k0, k1 = _dup_halves(k_all)
        kdup_scr[0] = k0
        kdup_scr[1] = k1
        for i, vt in enumerate(_value_rows(vt_all)):
            vt_scr[i] = vt

    attn = _run(_attend_keys_major(q_ref[...], kdup_scr, vt_scr, LAT_Q_BLOCK, LAT_SCORE_LOOKAHEAD))
    _mix_residual_norm2(x_ref[...], attn, mixed_ref[...], wout_ref, mod_ref, n2g_ref, x1_out, h2_out)


def _lat_attn(layer, q, k, v, cache_k, cache_v, mixed, xs, mods, w_out, norm2_g, cast_weights, cast_layer):
    nb, seq_len, _ = xs.shape
    past = cache_k.shape[3]
    tq = LAT_Q_TILE
    n_keys = past + seq_len
    n_tiles = seq_len // tq
    tile = lambda b, j: (b, j, 0)
    whole = lambda b, j: (b, 0, 0)
    cast_in, cast_out, cast_shapes = _cast_specs(cast_weights, cast_layer, nb * n_tiles,
                                                 lambda b, j: b * n_tiles + j)
    return pl.pallas_call(
        functools.partial(_lat_attn_kernel, n_cast=len(cast_weights)),
        out_shape=(
            jax.ShapeDtypeStruct((nb, seq_len, D_MODEL), F32),
            jax.ShapeDtypeStruct((nb, seq_len, D_MODEL), BF16),
            *cast_shapes,
        ),
        grid=(nb, seq_len // tq),
        in_specs=[
            pl.BlockSpec((None, tq, ATTN_WIDTH), tile),
            pl.BlockSpec((None, seq_len, KV_WIDTH), whole),
            pl.BlockSpec((None, seq_len, KV_WIDTH), whole),
            pl.BlockSpec((None, None, KV_WIDTH, past), lambda b, j: (b, layer, 0, 0)),
            pl.BlockSpec((None, None, KV_WIDTH, past), lambda b, j: (b, layer, 0, 0)),
            pl.BlockSpec((None, tq, POOL_WIDTH), tile),
            pl.BlockSpec((None, tq, D_MODEL), tile),
            pl.BlockSpec((None, None, N_MOD, D_MODEL), lambda b, j: (layer, 1 + b, 0, 0)),
            pl.BlockSpec((D_MODEL, D_MODEL), lambda b, j: (0, 0)),
            pl.BlockSpec((1, D_MODEL), lambda b, j: (0, 0)),
            *cast_in,
        ],
        out_specs=(
            pl.BlockSpec((None, tq, D_MODEL), tile),
            pl.BlockSpec((None, tq, D_MODEL), tile),
            *cast_out,
        ),
        scratch_shapes=[
            pltpu.VMEM((N_KV_HEADS, n_keys, LANES), BF16),
            pltpu.VMEM((2 * N_KV_HEADS, LANES, n_keys), BF16),
        ],
        compiler_params=pltpu.CompilerParams(
            dimension_semantics=("arbitrary", "arbitrary"), vmem_limit_bytes=VMEM_LIMIT),
        name=f"lat_attn_{layer}",
    )(q, k, v, cache_k, cache_v, mixed, xs, mods, w_out, norm2_g, *cast_weights)


def _ffn_kernel(*refs, seq_len, final_norm, has_halo):
    if has_halo:
        (x1_ref, h2_ref, h2prev_ref, h2next_ref, mod_ref, wup_ref, cw_ref, cb_ref, wdn_ref, fg_ref,
         out_ref, z_scr, act_scr) = refs
    else:
        (x1_ref, h2_ref, mod_ref, wup_ref, cw_ref, cb_ref, wdn_ref, fg_ref,
         out_ref, z_scr, act_scr) = refs
    tm = x1_ref.shape[0]
    gap = F32_SUBLANES
    if has_halo:
        halo = BF16_SUBLANES
        tile_start = pl.program_id(1) * tm
        h2prev = jnp.where(tile_start % seq_len != 0, h2prev_ref[...], jnp.zeros_like(h2prev_ref))
        h2next = jnp.where((tile_start + tm) % seq_len != 0, h2next_ref[...], jnp.zeros_like(h2next_ref))
        he = jnp.concatenate([h2prev, h2_ref[...], h2next], axis=0)
        segments = [(halo, tm)]
    else:
        he = h2_ref[...]
        segments = [(gap + s * (seq_len + gap), seq_len) for s in range(tm // seq_len)]
        for slab in range(z_scr.shape[0]):
            for s in range(tm // seq_len + 1):
                r = s * (seq_len + gap)
                z_scr[slab, r:r + gap, :] = jnp.zeros((gap, LANES), F32)

    def store_z(slab, z):
        if has_halo:
            z_scr[slab, 0:z.shape[0], :] = z
        else:
            for s, (row0, n) in enumerate(segments):
                z_scr[slab, row0:row0 + n, :] = z[s * n:(s + 1) * n]

    def conv_rows(slab, row, n, col0):
        cols = slice(col0, col0 + LANES)
        return (z_scr[slab, pl.ds(row - 1, n, stride=1), :] * cw_ref[0:1, cols]
                + z_scr[slab, row:row + n, :] * cw_ref[1:2, cols]
                + z_scr[slab, pl.ds(row + 1, n, stride=1), :] * cw_ref[2:3, cols]
                + cb_ref[0:1, cols])

    n_sub = FFN_CHUNK // LANES
    for ci in range(D_FF // FFN_CHUNK):
        c0 = ci * FFN_CHUNK
        slab0 = 2 * n_sub * (ci % 2)
        za = _dot(he, wup_ref[:, c0:c0 + FFN_CHUNK])
        zg = _dot(he, wup_ref[:, D_FF + c0:D_FF + c0 + FFN_CHUNK])
        for j in range(n_sub):
            lanes = slice(j * LANES, (j + 1) * LANES)
            slab_a, slab_g = slab0 + 2 * j, slab0 + 2 * j + 1
            col = c0 + j * LANES
            store_z(slab_a, za[:, lanes])
            store_z(slab_g, zg[:, lanes])
            for s, (row0, n) in enumerate(segments):
                a = conv_rows(slab_a, row0, n, col)
                g = conv_rows(slab_g, row0, n, D_FF + col)
                act_scr[s * n:(s + 1) * n, col:col + LANES] = (jax.nn.silu(a) * g).astype(BF16)
    x2 = x1_ref[...] + mod_ref[5:6, :] * _dot(act_scr[...], wdn_ref[...])
    if final_norm:
        ms = jnp.mean(x2 * x2, axis=-1, keepdims=True)
        x2 = x2 * lax.rsqrt(ms + EPS) * fg_ref[...]
    out_ref[...] = x2


def _ffn(layer, x1, h2, seq_len, mod_row0, mods, w_up, conv_w, conv_b, w_down, final_g, final_norm):
    nb, n_tok, _ = x1.shape
    tm = FFN_TILE
    has_halo = seq_len > tm
    assert seq_len % tm == 0 or tm % seq_len == 0
    halo = BF16_SUBLANES
    gap = F32_SUBLANES
    n_halo_blocks = n_tok // halo
    tile = lambda b, i: (b, i, 0)
    halo_specs = [
        pl.BlockSpec((None, halo, D_MODEL),
                     lambda b, i: (b, jnp.maximum(i * (tm // halo) - 1, 0), 0)),
        pl.BlockSpec((None, halo, D_MODEL),
                     lambda b, i: (b, jnp.minimum((i + 1) * (tm // halo), n_halo_blocks - 1), 0)),
    ] if has_halo else []
    halo_args = [h2, h2] if has_halo else []
    z_rows = tm + 2 * halo if has_halo else gap + (tm // seq_len) * (seq_len + gap)
    return pl.pallas_call(
        functools.partial(_ffn_kernel, seq_len=seq_len, final_norm=final_norm, has_halo=has_halo),
        out_shape=jax.ShapeDtypeStruct((nb, n_tok, D_MODEL), F32),
        grid=(nb, n_tok // tm),
        in_specs=[
            pl.BlockSpec((None, tm, D_MODEL), tile),
            pl.BlockSpec((None, tm, D_MODEL), tile),
            *halo_specs,
            pl.BlockSpec((None, None, N_MOD, D_MODEL), lambda b, i: (layer, mod_row0 + b, 0, 0)),
            pl.BlockSpec((D_MODEL, 2 * D_FF), lambda b, i: (0, 0), pipeline_mode=pl.Buffered(1)),
            pl.BlockSpec((None, 3, 2 * D_FF), lambda b, i: (layer, 0, 0)),
            pl.BlockSpec((None, 1, 2 * D_FF), lambda b, i: (layer, 0, 0)),
            pl.BlockSpec((D_FF, D_MODEL), lambda b, i: (0, 0), pipeline_mode=pl.Buffered(1)),
            pl.BlockSpec((1, D_MODEL), lambda b, i: (0, 0)),
        ],
        out_specs=pl.BlockSpec((None, tm, D_MODEL), tile),
        scratch_shapes=[
            pltpu.VMEM((4 * (FFN_CHUNK // LANES), z_rows, LANES), F32),
            pltpu.VMEM((tm, D_FF), BF16),
        ],
        compiler_params=pltpu.CompilerParams(
            dimension_semantics=("arbitrary", "arbitrary"), vmem_limit_bytes=VMEM_LIMIT),
        name=f"ffn_{'lat' if nb > 1 else 'ctx'}_{layer}",
    )(x1, h2, *halo_args, mods, w_up, conv_w, conv_b, w_down, final_g)


def _rope_tables(n_tokens):
    t = jnp.arange(n_tokens)
    row = (t // GRID_W).astype(F32)
    col = (t % GRID_W).astype(F32)
    n_freq = HEAD_DIM // 4
    inv = ROPE_THETA ** (-jnp.arange(n_freq, dtype=F32) / n_freq)
    ang = jnp.stack([row[:, None] * inv, col[:, None] * inv], axis=1)
    cos = jnp.broadcast_to(jnp.cos(ang)[:, :, None, :], (n_tokens, 2, 2, n_freq))
    sin = jnp.sin(ang)[:, :, None, :] * jnp.array([-1.0, 1.0], F32)[None, None, :, None]
    cos = cos.reshape(n_tokens, HEAD_DIM)
    sin = sin.reshape(n_tokens, HEAD_DIM)
    return jnp.tile(cos, (1, 2)), jnp.tile(sin, (1, 2))


def kernel(x_prompt, x_sample, cache_k, cache_v, c, c_ctx, w_mod, b_mod, norm1_g, w_in, q_norm_g, k_norm_g, w_pool, pool_scale, w_out, norm2_g, w_up, conv_w, conv_b, w_down, final_norm_g):
    batch, seq, d = x_prompt.shape
    dec_batch, dec_seq, _ = x_sample.shape
    past = cache_k.shape[2]

    w_in_l = w_in[0].astype(BF16)
    w_out_l = w_out[0].astype(BF16)
    w_pool_b = w_pool.astype(BF16)

    cond8 = jnp.zeros((8, d), F32).at[0].set(c_ctx).at[1:1 + dec_batch].set(c)
    mods = _modulation(cond8, w_mod, b_mod).reshape(DEPTH, 8, N_MOD, d)

    head_id = jnp.arange(ATTN_WIDTH) // HEAD_DIM
    blockdiag = (head_id[:, None] == head_id[None, :]).astype(BF16)
    cos_t, sin_t = _rope_tables(dec_seq)
    ck = jnp.transpose(cache_k, (0, 1, 3, 4, 2)).reshape(dec_batch, DEPTH, KV_WIDTH, past)
    cv = jnp.transpose(cache_v, (0, 1, 3, 4, 2)).reshape(dec_batch, DEPTH, KV_WIDTH, past)
    final_g = final_norm_g.reshape(1, d)

    xp = x_prompt.reshape(batch * seq, d)
    xs = x_sample
    ks_out, vs_out = [], []
    for l in range(DEPTH):
        n1g = norm1_g[l].reshape(1, d)
        n2g = norm2_g[l].reshape(1, d)
        qg = jnp.tile(q_norm_g[l], N_HEADS).reshape(1, ATTN_WIDTH)
        kg = jnp.tile(k_norm_g[l], N_KV_HEADS).reshape(1, KV_WIDTH)
        ps = pool_scale[l].reshape(1, POOL_WIDTH)
        cb = conv_b.reshape(DEPTH, 1, 2 * D_FF)
        last = l == DEPTH - 1

        k_l, v_l, x1p, h2p, w_up_l, w_down_l = _ctx_front(
            l, xp, seq, mods, n1g, w_in_l, blockdiag, qg, kg, w_pool_b, ps, w_out_l, n2g,
            (w_up, w_down))
        ks_out.append(k_l)
        vs_out.append(v_l)
        xp = _ffn(l, x1p[None], h2p[None], seq, 0, mods, w_up_l, conv_w, cb, w_down_l,
                  final_g, last)[0]

        q, k, v, mixed = _lat_proj(l, xs, mods, n1g, w_in_l, blockdiag, qg, kg, cos_t, sin_t,
                                   w_pool_b, ps)
        x1s, h2s, *next_weights = _lat_attn(l, q, k, v, ck, cv, mixed, xs, mods, w_out_l, n2g,
                                            () if last else (w_in, w_out), l + 1)
        xs = _ffn(l, x1s, h2s, dec_seq, 1, mods, w_up_l, conv_w, cb, w_down_l, final_g, last)
        if not last:
            w_in_l, w_out_l = next_weights

    y_prompt = xp.reshape(batch, seq, d)

    def cache_layout(per_layer):
        t = jnp.stack(per_layer, axis=1).reshape(batch, DEPTH, N_KV_HEADS, HEAD_DIM, seq)
        return jnp.transpose(t, (0, 1, 4, 2, 3))

    new_cache_k = cache_layout(ks_out)
    new_cache_v = cache_layout(vs_out)
    return (y_prompt, xs, new_cache_k, new_cache_v)
```

```python
import functools

import jax
import jax.numpy as jnp
from jax import lax
from jax.experimental import pallas as pl
from jax.experimental.pallas import tpu as pltpu

D_MODEL = 1024
DEPTH = 4
GRID_W = 64
HEAD_DIM = 64
N_HEADS = 8
N_KV_HEADS = 2
ATTN_WIDTH = N_HEADS * HEAD_DIM
KV_WIDTH = N_KV_HEADS * HEAD_DIM
POOL_WIDTH = D_MODEL - ATTN_WIDTH
POOL_WINDOWS = (2, 4, 8, 16)
POOL_GROUP_DIM = 128
IN_WIDTH = ATTN_WIDTH + 2 * KV_WIDTH + POOL_WIDTH
D_FF = 2816
ROPE_THETA = 10000.0
EPS = 1e-6
N_MOD = 6
LOG2_E = 1.4426950408889634

LANES = 128
F32_SUBLANES = 8
BF16_SUBLANES = 16
VMEM_LIMIT = 56 * 1024 * 1024

POOL_HALO = 8
FFN_CHUNK = 256
CTX_TILE = 1024
LAT_TILE = 1024
LAT_SUB_TILE = 512
LAT_SUB_SKEW = 1
LAT_Q_TILE = 512
LAT_Q_BLOCK = 128
FFN_TILE = 512
FFN_DOWN_GROUP = 7
LAT_SCORE_LOOKAHEAD = 4
CTX_SCORE_LOOKAHEAD = 2
CTX_SEQ_SKEW = 1

BF16 = jnp.bfloat16
F32 = jnp.float32


def _dot(a, b):
    return jnp.dot(a, b, preferred_element_type=F32)


def _dot_nt(a, b):
    return lax.dot_general(a, b, (((1,), (1,)), ((), ())), preferred_element_type=F32)


def _rms_mod(x, g, scale1p, shift):
    ms = jnp.mean(x * x, axis=-1, keepdims=True)
    return (x * lax.rsqrt(ms + EPS) * g) * scale1p + shift


def _head_norm(t, blockdiag, g_tiled):
    ssq = _dot((t * t).astype(BF16), blockdiag)
    return t * lax.rsqrt(ssq * (1.0 / HEAD_DIM) + EPS) * g_tiled


def _rope(t, cos_t, sin_t):
    lane = lax.broadcasted_iota(jnp.int32, (t.shape[0], LANES), 1)
    first = (lane % 32) < 16
    outs = []
    for ci in range(t.shape[1] // LANES):
        tc = t[:, ci * LANES:(ci + 1) * LANES]
        partner = jnp.where(first, pltpu.roll(tc, LANES - 16, 1), pltpu.roll(tc, 16, 1))
        outs.append(tc * cos_t + partner * sin_t)
    return outs[0] if len(outs) == 1 else jnp.concatenate(outs, axis=1)


def _dup_halves(t):
    lane = lax.broadcasted_iota(jnp.int32, t.shape, 1)
    lo = lane < HEAD_DIM
    sw = pltpu.roll(t, HEAD_DIM, 1)
    return jnp.where(lo, t, sw).astype(BF16), jnp.where(lo, sw, t).astype(BF16)


def _value_rows(vt):
    ones = jnp.ones((HEAD_DIM, vt.shape[1]), F32)
    out = []
    for g in range(N_KV_HEADS):
        vg = vt[g * HEAD_DIM:(g + 1) * HEAD_DIM]
        out.append(jnp.concatenate([vg, ones], axis=0).astype(BF16))
        out.append(jnp.concatenate([ones, vg], axis=0).astype(BF16))
    return out


def _interleave(chains, skew):
    live = list(enumerate(chains))
    t = 0
    while live:
        for i, chain in list(live):
            if t >= i * skew and next(chain, StopIteration) is StopIteration:
                live.remove((i, chain))
        t += 1


def _run(chain):
    while True:
        try:
            next(chain)
        except StopIteration as stop:
            return stop.value


def _attend_keys_major(q, kdup_ref, vt_ref, q_block, lookahead):
    tq = q.shape[0]
    lane = lax.broadcasted_iota(jnp.int32, (1, LANES), 1)
    masks = ((lane < HEAD_DIM).astype(BF16), (lane >= HEAD_DIM).astype(BF16))
    units = [(r, g, half) for r in range(0, tq, q_block) for g in range(N_KV_HEADS) for half in range(2)]

    def scores(r, g, half):
        q_a = q[r:r + q_block, (2 * g) * LANES:(2 * g + 1) * LANES]
        q_b = q[r:r + q_block, (2 * g + 1) * LANES:(2 * g + 2) * LANES]
        q_rows = jnp.concatenate([q_a * masks[half], q_b * masks[half]], axis=0)
        return _dot_nt(kdup_ref[g], q_rows)

    def values(r, g, half, st):
        pt = jnp.exp2(st - jnp.max(st, axis=0, keepdims=True)).astype(BF16)
        return _dot(vt_ref[2 * g + half], pt).T

    outs = {}
    pending = []
    for unit in units[:lookahead]:
        pending.append(scores(*unit))
        yield
    for i, unit in enumerate(units):
        if i + lookahead < len(units):
            pending.append(scores(*units[i + lookahead]))
            yield
        outs[unit] = values(*unit, pending.pop(0))
        yield

    lo = lax.broadcasted_iota(jnp.int32, (2 * q_block, LANES), 1) < HEAD_DIM
    rows = []
    for r in range(0, tq, q_block):
        blocks = []
        for g in range(N_KV_HEADS):
            o_even, o_odd = outs[(r, g, 0)], outs[(r, g, 1)]
            num = jnp.where(lo, o_even, o_odd)
            den = jnp.where(lo, pltpu.roll(o_even, HEAD_DIM, 1), pltpu.roll(o_odd, HEAD_DIM, 1))
            out = num / den
            blocks.extend([out[0:q_block], out[q_block:2 * q_block]])
        rows.append(jnp.concatenate(blocks, axis=1))
    return rows[0] if len(rows) == 1 else jnp.concatenate(rows, axis=0)


def _pool_mix(u, t_loc, seq_len, wpool_ref, lo_row, n_rows):
    m = u.shape[0]
    outs = []
    for gi, w in enumerate(POOL_WINDOWS):
        ug = u[:, gi * LANES:(gi + 1) * LANES]
        half = w // 2
        past, future, d = ug, ug, 1
        while d < half:
            past = past + jnp.where(t_loc >= d, pltpu.roll(past, d, 0), 0.0)
            future = future + jnp.where(t_loc + d < seq_len, pltpu.roll(future, m - d, 0), 0.0)
            d *= 2
        total = jnp.where(t_loc >= 1, pltpu.roll(past, 1, 0), 0.0) + future
        cnt = jnp.minimum(t_loc + half, seq_len) - jnp.maximum(t_loc - half, 0)
        pooled = total / cnt.astype(F32) - ug
        pooled = pooled[lo_row:lo_row + n_rows]
        outs.append(_dot(pooled.astype(BF16), wpool_ref[gi]))
    return jnp.concatenate(outs, axis=1)


def _mix_residual_norm2(x, attn, mixed, wout_ref, mod_ref, n2g_ref, x1_out, h2_out, rows=slice(None)):
    gt1, sh2, sc2 = mod_ref[2:3, :], mod_ref[3:4, :], mod_ref[4:5, :]
    mix = (_dot(attn.astype(BF16), wout_ref[0:ATTN_WIDTH, :])
           + _dot(mixed.astype(BF16), wout_ref[ATTN_WIDTH:D_MODEL, :]))
    x1 = x + gt1 * mix
    x1_out[rows, :] = x1
    h2_out[rows, :] = _rms_mod(x1, n2g_ref[...], 1.0 + sc2, sh2).astype(BF16)


def _cast_specs(weights, layer, n_steps, step_index):
    in_specs, out_specs, out_shapes = [], [], []
    for w in weights:
        _, rows, cols = w.shape
        blk = rows // n_steps
        assert blk * n_steps == rows and blk % BF16_SUBLANES == 0
        in_specs.append(pl.BlockSpec((None, blk, cols), lambda *g: (layer, step_index(*g), 0)))
        out_specs.append(pl.BlockSpec((blk, cols), lambda *g: (step_index(*g), 0)))
        out_shapes.append(jax.ShapeDtypeStruct((rows, cols), BF16))
    return in_specs, out_specs, out_shapes


def _cast_blocks(src_refs, dst_refs):
    for src, dst in zip(src_refs, dst_refs, strict=True):
        dst[...] = src[...].astype(BF16)


def _mod_kernel(cond_ref, w_ref, b_ref, out_ref):
    s = jax.nn.silu(cond_ref[...]).astype(BF16)
    out_ref[...] = _dot(s, w_ref[...].astype(BF16)) + b_ref[...]


def _modulation(cond8, w_mod, b_mod):
    tn = 1536
    n = N_MOD * D_MODEL
    return pl.pallas_call(
        _mod_kernel,
        out_shape=jax.ShapeDtypeStruct((DEPTH, 8, n), F32),
        grid=(DEPTH, n // tn),
        in_specs=[
            pl.BlockSpec((8, D_MODEL), lambda l, j: (0, 0)),
            pl.BlockSpec((None, D_MODEL, tn), lambda l, j: (l, 0, j)),
            pl.BlockSpec((None, 1, tn), lambda l, j: (l, 0, j)),
        ],
        out_specs=pl.BlockSpec((None, 8, tn), lambda l, j: (l, 0, j)),
        compiler_params=pltpu.CompilerParams(
            dimension_semantics=("arbitrary", "arbitrary"), vmem_limit_bytes=VMEM_LIMIT),
        name="modulation",
    )(cond8, w_mod, b_mod.reshape(DEPTH, 1, n))


def _ctx_front_kernel(*refs, seq_len, n_cast):
    (x_ref, mod_ref, n1g_ref, win_ref, bd_ref, qg_ref, kg_ref, wpool_ref, pscale_ref, wout_ref,
     n2g_ref) = refs[:11]
    cast_src = refs[11:11 + n_cast]
    kt_out, vt_out, x1_out, h2_out = refs[11 + n_cast:15 + n_cast]
    _cast_blocks(cast_src, refs[15 + n_cast:])
    tm = x_ref.shape[0]
    sh1, sc1 = mod_ref[0:1, :], mod_ref[1:2, :]
    bd = bd_ref[...]
    t_loc = lax.broadcasted_iota(jnp.int32, (seq_len, LANES), 0)

    def sequence(s):
        rows = slice(s * seq_len, (s + 1) * seq_len)
        x = x_ref[rows, :]
        h = _rms_mod(x, n1g_ref[...], 1.0 + sc1, sh1).astype(BF16)
        proj = _dot(h, win_ref[...])
        yield
        q = (_head_norm(proj[:, 0:ATTN_WIDTH], bd, qg_ref[...])
             * (HEAD_DIM ** -0.5 * LOG2_E)).astype(BF16)
        k = _head_norm(proj[:, ATTN_WIDTH:ATTN_WIDTH + KV_WIDTH], bd[0:KV_WIDTH, 0:KV_WIDTH],
                       kg_ref[...])
        yield
        v = proj[:, ATTN_WIDTH + KV_WIDTH:ATTN_WIDTH + 2 * KV_WIDTH]
        u = proj[:, ATTN_WIDTH + 2 * KV_WIDTH:IN_WIDTH]
        kt_out[s] = k.T
        vt = v.T
        vt_out[s] = vt
        attn = yield from _attend_keys_major(q, _dup_halves(k), _value_rows(vt), seq_len,
                                             CTX_SCORE_LOOKAHEAD)
        mixed = _pool_mix(u, t_loc, seq_len, wpool_ref, 0, seq_len) * pscale_ref[...]
        yield
        _mix_residual_norm2(x, attn, mixed, wout_ref, mod_ref, n2g_ref, x1_out, h2_out, rows)

    _interleave([sequence(s) for s in range(tm // seq_len)], CTX_SEQ_SKEW)


def _ctx_front(layer, xp, seq_len, mods, norm1_g, w_in, bd, qg, kg, w_pool, pool_scale, w_out, norm2_g,
               cast_weights):
    n_tok = xp.shape[0]
    tm = CTX_TILE
    const2 = lambda i: (0, 0)
    tile = lambda i: (i, 0)
    cast_in, cast_out, cast_shapes = _cast_specs(cast_weights, layer, n_tok // tm, lambda i: i)
    return pl.pallas_call(
        functools.partial(_ctx_front_kernel, seq_len=seq_len, n_cast=len(cast_weights)),
        out_shape=(
            jax.ShapeDtypeStruct((n_tok // seq_len, KV_WIDTH, seq_len), F32),
            jax.ShapeDtypeStruct((n_tok // seq_len, KV_WIDTH, seq_len), F32),
            jax.ShapeDtypeStruct((n_tok, D_MODEL), F32),
            jax.ShapeDtypeStruct((n_tok, D_MODEL), BF16),
            *cast_shapes,
        ),
        grid=(n_tok // tm,),
        in_specs=[
            pl.BlockSpec((tm, D_MODEL), tile),
            pl.BlockSpec((None, None, N_MOD, D_MODEL), lambda i: (layer, 0, 0, 0)),
            pl.BlockSpec((1, D_MODEL), const2),
            pl.BlockSpec((D_MODEL, IN_WIDTH), const2),
            pl.BlockSpec((ATTN_WIDTH, ATTN_WIDTH), const2),
            pl.BlockSpec((1, ATTN_WIDTH), const2),
            pl.BlockSpec((1, KV_WIDTH), const2),
            pl.BlockSpec((None, 4, POOL_GROUP_DIM, POOL_GROUP_DIM), lambda i: (layer, 0, 0, 0)),
            pl.BlockSpec((1, POOL_WIDTH), const2),
            pl.BlockSpec((D_MODEL, D_MODEL), const2),
            pl.BlockSpec((1, D_MODEL), const2),
            *cast_in,
        ],
        out_specs=(
            pl.BlockSpec((tm // seq_len, KV_WIDTH, seq_len), lambda i: (i, 0, 0)),
            pl.BlockSpec((tm // seq_len, KV_WIDTH, seq_len), lambda i: (i, 0, 0)),
            pl.BlockSpec((tm, D_MODEL), tile),
            pl.BlockSpec((tm, D_MODEL), tile),
            *cast_out,
        ),
        compiler_params=pltpu.CompilerParams(
            dimension_semantics=("arbitrary",), vmem_limit_bytes=VMEM_LIMIT),
        name=f"ctx_front_{layer}",
    )(xp, mods, norm1_g, w_in, bd, qg, kg, w_pool, pool_scale, w_out, norm2_g, *cast_weights)


def _lat_proj_kernel(xprev_ref, x_ref, xnext_ref, mod_ref, n1g_ref, win_ref, bd_ref, qg_ref, kg_ref,
                     cos_ref, sin_ref, wpool_ref, pscale_ref, q_out, k_out, v_out, mixed_out, *, seq_len):
    tm = x_ref.shape[0]
    halo = POOL_HALO
    sub = LAT_SUB_TILE
    sh1, sc1 = mod_ref[0:1, :], mod_ref[1:2, :]
    bd = bd_ref[...]
    tile_start = pl.program_id(1) * tm

    def sub_tile(r0):
        rows = slice(r0, r0 + sub)
        before = xprev_ref[...] if r0 == 0 else x_ref[r0 - halo:r0, :]
        after = xnext_ref[...] if r0 + sub == tm else x_ref[r0 + sub:r0 + sub + halo, :]
        xe = jnp.concatenate([before, x_ref[rows, :], after], axis=0)
        h = _rms_mod(xe, n1g_ref[...], 1.0 + sc1, sh1).astype(BF16)
        proj = _dot(h, win_ref[...])
        yield
        main = proj[halo:halo + sub]
        cos_t, sin_t = cos_ref[rows, :], sin_ref[rows, :]
        q = _head_norm(main[:, 0:ATTN_WIDTH], bd, qg_ref[...])
        q_out[rows, :] = (_rope(q, cos_t, sin_t) * (HEAD_DIM ** -0.5 * LOG2_E)).astype(BF16)
        yield
        k = _head_norm(main[:, ATTN_WIDTH:ATTN_WIDTH + KV_WIDTH], bd[0:KV_WIDTH, 0:KV_WIDTH],
                       kg_ref[...])
        k_out[rows, :] = _rope(k, cos_t, sin_t).astype(BF16)
        v_out[rows, :] = main[:, ATTN_WIDTH + KV_WIDTH:ATTN_WIDTH + 2 * KV_WIDTH].astype(BF16)
        yield
        u = proj[:, ATTN_WIDTH + 2 * KV_WIDTH:IN_WIDTH]
        t_loc = (lax.broadcasted_iota(jnp.int32, (sub + 2 * halo, LANES), 0)
                 + (tile_start + r0 - halo))
        mixed = _pool_mix(u, t_loc, seq_len, wpool_ref, halo, sub) * pscale_ref[...]
        mixed_out[rows, :] = mixed.astype(BF16)

    _interleave([sub_tile(r0) for r0 in range(0, tm, sub)], LAT_SUB_SKEW)


def _lat_proj(layer, xs, mods, norm1_g, w_in, bd, qg, kg, cos_t, sin_t, w_pool, pool_scale):
    nb, seq_len, _ = xs.shape
    tm = LAT_TILE
    halo = POOL_HALO
    n_halo_blocks = seq_len // halo
    const2 = lambda b, i: (0, 0)
    tile = lambda b, i: (b, i, 0)
    return pl.pallas_call(
        functools.partial(_lat_proj_kernel, seq_len=seq_len),
        out_shape=(
            jax.ShapeDtypeStruct((nb, seq_len, ATTN_WIDTH), BF16),
            jax.ShapeDtypeStruct((nb, seq_len, KV_WIDTH), BF16),
            jax.ShapeDtypeStruct((nb, seq_len, KV_WIDTH), BF16),
            jax.ShapeDtypeStruct((nb, seq_len, POOL_WIDTH), BF16),
        ),
        grid=(nb, seq_len // tm),
        in_specs=[
            pl.BlockSpec((None, halo, D_MODEL),
                         lambda b, i: (b, jnp.maximum(i * (tm // halo) - 1, 0), 0)),
            pl.BlockSpec((None, tm, D_MODEL), tile),
            pl.BlockSpec((None, halo, D_MODEL),
                         lambda b, i: (b, jnp.minimum((i + 1) * (tm // halo), n_halo_blocks - 1), 0)),
            pl.BlockSpec((None, None, N_MOD, D_MODEL), lambda b, i: (layer, 1 + b, 0, 0)),
            pl.BlockSpec((1, D_MODEL), const2),
            pl.BlockSpec((D_MODEL, IN_WIDTH), const2),
            pl.BlockSpec((ATTN_WIDTH, ATTN_WIDTH), const2),
            pl.BlockSpec((1, ATTN_WIDTH), const2),
            pl.BlockSpec((1, KV_WIDTH), const2),
            pl.BlockSpec((tm, LANES), lambda b, i: (i, 0)),
            pl.BlockSpec((tm, LANES), lambda b, i: (i, 0)),
            pl.BlockSpec((None, 4, POOL_GROUP_DIM, POOL_GROUP_DIM), lambda b, i: (layer, 0, 0, 0)),
            pl.BlockSpec((1, POOL_WIDTH), const2),
        ],
        out_specs=(
            pl.BlockSpec((None, tm, ATTN_WIDTH), tile),
            pl.BlockSpec((None, tm, KV_WIDTH), tile),
            pl.BlockSpec((None, tm, KV_WIDTH), tile),
            pl.BlockSpec((None, tm, POOL_WIDTH), tile),
        ),
        compiler_params=pltpu.CompilerParams(
            dimension_semantics=("arbitrary", "arbitrary"), vmem_limit_bytes=VMEM_LIMIT),
        name=f"lat_proj_{layer}",
    )(xs, xs, xs, mods, norm1_g, w_in, bd, qg, kg, cos_t, sin_t, w_pool, pool_scale)


def _lat_attn_kernel(*refs, n_cast):
    (q_ref, klat_ref, vlat_ref, ck_ref, cv_ref, mixed_ref, x_ref, mod_ref, wout_ref,
     n2g_ref) = refs[:10]
    cast_src = refs[10:10 + n_cast]
    x1_out, h2_out = refs[10 + n_cast:12 + n_cast]
    cast_dst = refs[12 + n_cast:12 + 2 * n_cast]
    kdup_scr, vt_scr = refs[12 + 2 * n_cast:]
    _cast_blocks(cast_src, cast_dst)

    @pl.when(pl.program_id(1) == 0)
    def _():
        k_all = jnp.concatenate([ck_ref[...].T, klat_ref[...].astype(F32)], axis=0)
        vt_all = jnp.concatenate([cv_ref[...], vlat_ref[...].astype(F32).T], axis=1)
        k0, k1 = _dup_halves(k_all)
        kdup_scr[0] = k0
        kdup_scr[1] = k1
        for i, vt in enumerate(_value_rows(vt_all)):
            vt_scr[i] = vt

    attn = _run(_attend_keys_major(q_ref[...], kdup_scr, vt_scr, LAT_Q_BLOCK, LAT_SCORE_LOOKAHEAD))
    _mix_residual_norm2(x_ref[...], attn, mixed_ref[...], wout_ref, mod_ref, n2g_ref, x1_out, h2_out)


def _lat_attn(layer, q, k, v, cache_k, cache_v, mixed, xs, mods, w_out, norm2_g, cast_weights, cast_layer):
    nb, seq_len, _ = xs.shape
    past = cache_k.shape[3]
    tq = LAT_Q_TILE
    n_keys = past + seq_len
    n_tiles = seq_len // tq
    tile = lambda b, j: (b, j, 0)
    whole = lambda b, j: (b, 0, 0)
    cast_in, cast_out, cast_shapes = _cast_specs(cast_weights, cast_layer, nb * n_tiles,
                                                 lambda b, j: b * n_tiles + j)
    return pl.pallas_call(
        functools.partial(_lat_attn_kernel, n_cast=len(cast_weights)),
        out_shape=(
            jax.ShapeDtypeStruct((nb, seq_len, D_MODEL), F32),
            jax.ShapeDtypeStruct((nb, seq_len, D_MODEL), BF16),
            *cast_shapes,
        ),
        grid=(nb, seq_len // tq),
        in_specs=[
            pl.BlockSpec((None, tq, ATTN_WIDTH), tile),
            pl.BlockSpec((None, seq_len, KV_WIDTH), whole),
            pl.BlockSpec((None, seq_len, KV_WIDTH), whole),
            pl.BlockSpec((None, None, KV_WIDTH, past), lambda b, j: (b, layer, 0, 0)),
            pl.BlockSpec((None, None, KV_WIDTH, past), lambda b, j: (b, layer, 0, 0)),
            pl.BlockSpec((None, tq, POOL_WIDTH), tile),
            pl.BlockSpec((None, tq, D_MODEL), tile),
            pl.BlockSpec((None, None, N_MOD, D_MODEL), lambda b, j: (layer, 1 + b, 0, 0)),
            pl.BlockSpec((D_MODEL, D_MODEL), lambda b, j: (0, 0)),
            pl.BlockSpec((1, D_MODEL), lambda b, j: (0, 0)),
            *cast_in,
        ],
        out_specs=(
            pl.BlockSpec((None, tq, D_MODEL), tile),
            pl.BlockSpec((None, tq, D_MODEL), tile),
            *cast_out,
        ),
        scratch_shapes=[
            pltpu.VMEM((N_KV_HEADS, n_keys, LANES), BF16),
            pltpu.VMEM((2 * N_KV_HEADS, LANES, n_keys), BF16),
        ],
        compiler_params=pltpu.CompilerParams(
            dimension_semantics=("arbitrary", "arbitrary"), vmem_limit_bytes=VMEM_LIMIT),
        name=f"lat_attn_{layer}",
    )(q, k, v, cache_k, cache_v, mixed, xs, mods, w_out, norm2_g, *cast_weights)


def _ffn_kernel(*refs, seq_len, final_norm, has_halo):
    if has_halo:
        (x1_ref, h2_ref, h2prev_ref, h2next_ref, mod_ref, wup_ref, cw_ref, cb_ref, wdn_ref, fg_ref,
         out_ref, z_scr, act_scr) = refs
    else:
        (x1_ref, h2_ref, mod_ref, wup_ref, cw_ref, cb_ref, wdn_ref, fg_ref,
         out_ref, z_scr, act_scr) = refs
    tm = x1_ref.shape[0]
    gap = F32_SUBLANES
    if has_halo:
        halo = BF16_SUBLANES
        tile_start = pl.program_id(1) * tm
        h2prev = jnp.where(tile_start % seq_len != 0, h2prev_ref[...], jnp.zeros_like(h2prev_ref))
        h2next = jnp.where((tile_start + tm) % seq_len != 0, h2next_ref[...], jnp.zeros_like(h2next_ref))
        he = jnp.concatenate([h2prev, h2_ref[...], h2next], axis=0)
        segments = [(halo, tm)]
    else:
        he = h2_ref[...]
        segments = [(gap + s * (seq_len + gap), seq_len) for s in range(tm // seq_len)]
        for slab in range(z_scr.shape[0]):
            for s in range(tm // seq_len + 1):
                r = s * (seq_len + gap)
                z_scr[slab, r:r + gap, :] = jnp.zeros((gap, LANES), F32)

    def store_z(slab, z):
        if has_halo:
            z_scr[slab, 0:z.shape[0], :] = z
        else:
            for s, (row0, n) in enumerate(segments):
                z_scr[slab, row0:row0 + n, :] = z[s * n:(s + 1) * n]

    def conv_rows(slab, row, n, col0):
        cols = slice(col0, col0 + LANES)
        return (z_scr[slab, pl.ds(row - 1, n, stride=1), :] * cw_ref[0:1, cols]
                + z_scr[slab, row:row + n, :] * cw_ref[1:2, cols]
                + z_scr[slab, pl.ds(row + 1, n, stride=1), :] * cw_ref[2:3, cols]
                + cb_ref[0:1, cols])

    n_sub = FFN_CHUNK // LANES
    n_chunks = D_FF // FFN_CHUNK
    partial = []
    for ci in range(n_chunks):
        c0 = ci * FFN_CHUNK
        slab0 = 2 * n_sub * (ci % 2)
        za = _dot(he, wup_ref[:, c0:c0 + FFN_CHUNK])
        zg = _dot(he, wup_ref[:, D_FF + c0:D_FF + c0 + FFN_CHUNK])
        for j in range(n_sub):
            lanes = slice(j * LANES, (j + 1) * LANES)
            slab_a, slab_g = slab0 + 2 * j, slab0 + 2 * j + 1
            col = c0 + j * LANES
            store_z(slab_a, za[:, lanes])
            store_z(slab_g, zg[:, lanes])
            for s, (row0, n) in enumerate(segments):
                a = conv_rows(slab_a, row0, n, col)
                g = conv_rows(slab_g, row0, n, D_FF + col)
                act_scr[s * n:(s + 1) * n, col:col + LANES] = (jax.nn.silu(a) * g).astype(BF16)
        if (ci + 1) % FFN_DOWN_GROUP == 0 or ci + 1 == n_chunks:
            k0 = (ci // FFN_DOWN_GROUP) * FFN_DOWN_GROUP * FFN_CHUNK
            k1 = (ci + 1) * FFN_CHUNK
            partial.append(_dot(act_scr[:, k0:k1], wdn_ref[k0:k1, :]))
    x2 = x1_ref[...] + mod_ref[5:6, :] * functools.reduce(lambda p, r: p + r, partial)
    if final_norm:
        ms = jnp.mean(x2 * x2, axis=-1, keepdims=True)
        x2 = x2 * lax.rsqrt(ms + EPS) * fg_ref[...]
    out_ref[...] = x2


def _ffn(layer, x1, h2, seq_len, mod_row0, mods, w_up, conv_w, conv_b, w_down, final_g, final_norm):
    nb, n_tok, _ = x1.shape
    tm = FFN_TILE
    has_halo = seq_len > tm
    assert seq_len % tm == 0 or tm % seq_len == 0
    halo = BF16_SUBLANES
    gap = F32_SUBLANES
    n_halo_blocks = n_tok // halo
    tile = lambda b, i: (b, i, 0)
    halo_specs = [
        pl.BlockSpec((None, halo, D_MODEL),
                     lambda b, i: (b, jnp.maximum(i * (tm // halo) - 1, 0), 0)),
        pl.BlockSpec((None, halo, D_MODEL),
                     lambda b, i: (b, jnp.minimum((i + 1) * (tm // halo), n_halo_blocks - 1), 0)),
    ] if has_halo else []
    halo_args = [h2, h2] if has_halo else []
    z_rows = tm + 2 * halo if has_halo else gap + (tm // seq_len) * (seq_len + gap)
    return pl.pallas_call(
        functools.partial(_ffn_kernel, seq_len=seq_len, final_norm=final_norm, has_halo=has_halo),
        out_shape=jax.ShapeDtypeStruct((nb, n_tok, D_MODEL), F32),
        grid=(nb, n_tok // tm),
        in_specs=[
            pl.BlockSpec((None, tm, D_MODEL), tile),
            pl.BlockSpec((None, tm, D_MODEL), tile),
            *halo_specs,
            pl.BlockSpec((None, None, N_MOD, D_MODEL), lambda b, i: (layer, mod_row0 + b, 0, 0)),
            pl.BlockSpec((D_MODEL, 2 * D_FF), lambda b, i: (0, 0), pipeline_mode=pl.Buffered(1)),
            pl.BlockSpec((None, 3, 2 * D_FF), lambda b, i: (layer, 0, 0)),
            pl.BlockSpec((None, 1, 2 * D_FF), lambda b, i: (layer, 0, 0)),
            pl.BlockSpec((D_FF, D_MODEL), lambda b, i: (0, 0), pipeline_mode=pl.Buffered(1)),
            pl.BlockSpec((1, D_MODEL), lambda b, i: (0, 0)),
        ],
        out_specs=pl.BlockSpec((None, tm, D_MODEL), tile),
        scratch_shapes=[
            pltpu.VMEM((4 * (FFN_CHUNK // LANES), z_rows, LANES), F32),
            pltpu.VMEM((tm, D_FF), BF16),
        ],
        compiler_params=pltpu.CompilerParams(
            dimension_semantics=("arbitrary", "arbitrary"), vmem_limit_bytes=VMEM_LIMIT),
        name=f"ffn_{'lat' if nb > 1 else 'ctx'}_{layer}",
    )(x1, h2, *halo_args, mods, w_up, conv_w, conv_b, w_down, final_g)


def _rope_tables(n_tokens):
    t = jnp.arange(n_tokens)
    row = (t // GRID_W).astype(F32)
    col = (t % GRID_W).astype(F32)
    n_freq = HEAD_DIM // 4
    inv = ROPE_THETA ** (-jnp.arange(n_freq, dtype=F32) / n_freq)
    ang = jnp.stack([row[:, None] * inv, col[:, None] * inv], axis=1)
    cos = jnp.broadcast_to(jnp.cos(ang)[:, :, None, :], (n_tokens, 2, 2, n_freq))
    sin = jnp.sin(ang)[:, :, None, :] * jnp.array([-1.0, 1.0], F32)[None, None, :, None]
    cos = cos.reshape(n_tokens, HEAD_DIM)
    sin = sin.reshape(n_tokens, HEAD_DIM)
    return jnp.tile(cos, (1, 2)), jnp.tile(sin, (1, 2))


def kernel(x_prompt, x_sample, cache_k, cache_v, c, c_ctx, w_mod, b_mod, norm1_g, w_in, q_norm_g, k_norm_g, w_pool, pool_scale, w_out, norm2_g, w_up, conv_w, conv_b, w_down, final_norm_g):
    batch, seq, d = x_prompt.shape
    dec_batch, dec_seq, _ = x_sample.shape
    past = cache_k.shape[2]

    w_in_l = w_in[0].astype(BF16)
    w_out_l = w_out[0].astype(BF16)
    w_pool_b = w_pool.astype(BF16)

    cond8 = jnp.zeros((8, d), F32).at[0].set(c_ctx).at[1:1 + dec_batch].set(c)
    mods = _modulation(cond8, w_mod, b_mod).reshape(DEPTH, 8, N_MOD, d)

    head_id = jnp.arange(ATTN_WIDTH) // HEAD_DIM
    blockdiag = (head_id[:, None] == head_id[None, :]).astype(BF16)
    cos_t, sin_t = _rope_tables(dec_seq)
    ck = jnp.transpose(cache_k, (0, 1, 3, 4, 2)).reshape(dec_batch, DEPTH, KV_WIDTH, past)
    cv = jnp.transpose(cache_v, (0, 1, 3, 4, 2)).reshape(dec_batch, DEPTH, KV_WIDTH, past)
    final_g = final_norm_g.reshape(1, d)

    xp = x_prompt.reshape(batch * seq, d)
    xs = x_sample
    ks_out, vs_out = [], []
    for l in range(DEPTH):
        n1g = norm1_g[l].reshape(1, d)
        n2g = norm2_g[l].reshape(1, d)
        qg = jnp.tile(q_norm_g[l], N_HEADS).reshape(1, ATTN_WIDTH)
        kg = jnp.tile(k_norm_g[l], N_KV_HEADS).reshape(1, KV_WIDTH)
        ps = pool_scale[l].reshape(1, POOL_WIDTH)
        cb = conv_b.reshape(DEPTH, 1, 2 * D_FF)
        last = l == DEPTH - 1

        k_l, v_l, x1p, h2p, w_up_l, w_down_l = _ctx_front(
            l, xp, seq, mods, n1g, w_in_l, blockdiag, qg, kg, w_pool_b, ps, w_out_l, n2g,
            (w_up, w_down))
        ks_out.append(k_l)
        vs_out.append(v_l)
        xp = _ffn(l, x1p[None], h2p[None], seq, 0, mods, w_up_l, conv_w, cb, w_down_l,
                  final_g, last)[0]

        q, k, v, mixed = _lat_proj(l, xs, mods, n1g, w_in_l, blockdiag, qg, kg, cos_t, sin_t,
                                   w_pool_b, ps)
        x1s, h2s, *next_weights = _lat_attn(l, q, k, v, ck, cv, mixed, xs, mods, w_out_l, n2g,
                                            () if last else (w_in, w_out), l + 1)
        xs = _ffn(l, x1s, h2s, dec_seq, 1, mods, w_up_l, conv_w, cb, w_down_l, final_g, last)
        if not last:
            w_in_l, w_out_l = next_weights

    y_prompt = xp.reshape(batch, seq, d)

    def cache_layout(per_layer):
        t = jnp.stack(per_layer, axis=1).reshape(batch, DEPTH, N_KV_HEADS, HEAD_DIM, seq)
        return jnp.transpose(t, (0, 1, 4, 2, 3))

    new_cache_k = cache_layout(ks_out)
    new_cache_v = cache_layout(vs_out)
    return (y_prompt, xs, new_cache_k, new_cache_v)
```

```python
import functools

import jax
import jax.numpy as jnp
from jax import lax
from jax.experimental import pallas as pl
from jax.experimental.pallas import tpu as pltpu

D_MODEL = 1024
DEPTH = 4
GRID_W = 64
HEAD_DIM = 64
N_HEADS = 8
N_KV_HEADS = 2
ATTN_WIDTH = N_HEADS * HEAD_DIM
KV_WIDTH = N_KV_HEADS * HEAD_DIM
POOL_WIDTH = D_MODEL - ATTN_WIDTH
POOL_WINDOWS = (2, 4, 8, 16)
POOL_GROUP_DIM = 128
IN_WIDTH = ATTN_WIDTH + 2 * KV_WIDTH + POOL_WIDTH
D_FF = 2816
ROPE_THETA = 10000.0
ROPE_AXIS_DIM = HEAD_DIM // 2
ROPE_PAIR_DIM = HEAD_DIM // 4
EPS = 1e-6
N_MOD = 6
LOG2_E = 1.4426950408889634

LANES = 128
F32_SUBLANES = 8
BF16_SUBLANES = 16
VMEM_LIMIT = 56 * 1024 * 1024

MOD_ROWS = F32_SUBLANES
MOD_COL_BLOCKS = 4
POOL_HALO = 8
FFN_CHUNK = 256
CTX_TILE = 1024
LAT_TILE = 1024
LAT_SUB_TILE = 512
LAT_SUB_SKEW = 1
LAT_Q_TILE = 512
LAT_Q_BLOCK = 128
FFN_TILE = 512
LAT_SCORE_LOOKAHEAD = 4
CTX_SCORE_LOOKAHEAD = 2
CTX_SEQ_SKEW = 1

BF16 = jnp.bfloat16
F32 = jnp.float32


def _dot(a, b):
    return jnp.dot(a, b, preferred_element_type=F32)


def _dot_nt(a, b):
    return lax.dot_general(a, b, (((1,), (1,)), ((), ())), preferred_element_type=F32)


def _rms_mod(x, g, scale1p, shift):
    ms = jnp.mean(x * x, axis=-1, keepdims=True)
    return (x * lax.rsqrt(ms + EPS) * g) * scale1p + shift


def _head_norm(t, blockdiag, g_tiled):
    ssq = _dot((t * t).astype(BF16), blockdiag)
    return t * lax.rsqrt(ssq * (1.0 / HEAD_DIM) + EPS) * g_tiled


def _rope(t, cos_t, sin_t):
    lane = lax.broadcasted_iota(jnp.int32, (t.shape[0], LANES), 1)
    first = (lane % ROPE_AXIS_DIM) < ROPE_PAIR_DIM
    outs = []
    for ci in range(t.shape[1] // LANES):
        tc = t[:, ci * LANES:(ci + 1) * LANES]
        partner = jnp.where(first, pltpu.roll(tc, LANES - ROPE_PAIR_DIM, 1),
                            pltpu.roll(tc, ROPE_PAIR_DIM, 1))
        outs.append(tc * cos_t + partner * sin_t)
    return outs[0] if len(outs) == 1 else jnp.concatenate(outs, axis=1)


def _dup_halves(t):
    lane = lax.broadcasted_iota(jnp.int32, t.shape, 1)
    lo = lane < HEAD_DIM
    sw = pltpu.roll(t, HEAD_DIM, 1)
    return jnp.where(lo, t, sw).astype(BF16), jnp.where(lo, sw, t).astype(BF16)


def _value_rows(vt):
    ones = jnp.ones((HEAD_DIM, vt.shape[1]), F32)
    out = []
    for g in range(N_KV_HEADS):
        vg = vt[g * HEAD_DIM:(g + 1) * HEAD_DIM]
        out.append(jnp.concatenate([vg, ones], axis=0).astype(BF16))
        out.append(jnp.concatenate([ones, vg], axis=0).astype(BF16))
    return out


def _interleave(chains, skew):
    live = list(enumerate(chains))
    t = 0
    while live:
        for i, chain in list(live):
            if t >= i * skew and next(chain, StopIteration) is StopIteration:
                live.remove((i, chain))
        t += 1


def _run(chain):
    while True:
        try:
            next(chain)
        except StopIteration as stop:
            return stop.value


def _attend_keys_major(q, kdup_ref, vt_ref, q_block, lookahead):
    tq = q.shape[0]
    lane = lax.broadcasted_iota(jnp.int32, (1, LANES), 1)
    masks = ((lane < HEAD_DIM).astype(BF16), (lane >= HEAD_DIM).astype(BF16))
    units = [(r, g, half) for r in range(0, tq, q_block) for g in range(N_KV_HEADS) for half in range(2)]

    def scores(r, g, half):
        q_a = q[r:r + q_block, (2 * g) * LANES:(2 * g + 1) * LANES]
        q_b = q[r:r + q_block, (2 * g + 1) * LANES:(2 * g + 2) * LANES]
        q_rows = jnp.concatenate([q_a * masks[half], q_b * masks[half]], axis=0)
        return _dot_nt(kdup_ref[g], q_rows)

    def values(r, g, half, st):
        pt = jnp.exp2(st - jnp.max(st, axis=0, keepdims=True)).astype(BF16)
        return _dot(vt_ref[2 * g + half], pt).T

    outs = {}
    pending = []
    for unit in units[:lookahead]:
        pending.append(scores(*unit))
        yield
    for i, unit in enumerate(units):
        if i + lookahead < len(units):
            pending.append(scores(*units[i + lookahead]))
            yield
        outs[unit] = values(*unit, pending.pop(0))
        yield

    lo = lax.broadcasted_iota(jnp.int32, (2 * q_block, LANES), 1) < HEAD_DIM
    rows = []
    for r in range(0, tq, q_block):
        blocks = []
        for g in range(N_KV_HEADS):
            o_even, o_odd = outs[(r, g, 0)], outs[(r, g, 1)]
            num = jnp.where(lo, o_even, o_odd)
            den = jnp.where(lo, pltpu.roll(o_even, HEAD_DIM, 1), pltpu.roll(o_odd, HEAD_DIM, 1))
            out = num / den
            blocks.extend([out[0:q_block], out[q_block:2 * q_block]])
        rows.append(jnp.concatenate(blocks, axis=1))
    return rows[0] if len(rows) == 1 else jnp.concatenate(rows, axis=0)


def _pool_mix(u, t_loc, seq_len, wpool_ref, lo_row, n_rows):
    m = u.shape[0]
    outs = []
    for gi, w in enumerate(POOL_WINDOWS):
        ug = u[:, gi * LANES:(gi + 1) * LANES]
        half = w // 2
        past, future, d = ug, ug, 1
        while d < half:
            past = past + jnp.where(t_loc >= d, pltpu.roll(past, d, 0), 0.0)
            future = future + jnp.where(t_loc + d < seq_len, pltpu.roll(future, m - d, 0), 0.0)
            d *= 2
        total = jnp.where(t_loc >= 1, pltpu.roll(past, 1, 0), 0.0) + future
        cnt = jnp.minimum(t_loc + half, seq_len) - jnp.maximum(t_loc - half, 0)
        pooled = total / cnt.astype(F32) - ug
        pooled = pooled[lo_row:lo_row + n_rows]
        outs.append(_dot(pooled.astype(BF16), wpool_ref[gi]))
    return jnp.concatenate(outs, axis=1)


def _mix_residual_norm2(x, attn, mixed, wout_ref, mod_ref, n2g_ref, x1_out, h2_out, rows=slice(None)):
    gt1, sh2, sc2 = mod_ref[2:3, :], mod_ref[3:4, :], mod_ref[4:5, :]
    mix = (_dot(attn.astype(BF16), wout_ref[0:ATTN_WIDTH, :])
           + _dot(mixed.astype(BF16), wout_ref[ATTN_WIDTH:D_MODEL, :]))
    x1 = x + gt1 * mix
    x1_out[rows, :] = x1
    h2_out[rows, :] = _rms_mod(x1, n2g_ref[...], 1.0 + sc2, sh2).astype(BF16)


def _cast_specs(weights, layer, n_steps, step_index):
    in_specs, out_specs, out_shapes = [], [], []
    for w in weights:
        _, rows, cols = w.shape
        blk = rows // n_steps
        assert blk * n_steps == rows and blk % BF16_SUBLANES == 0
        in_specs.append(pl.BlockSpec((None, blk, cols), lambda *g: (layer, step_index(*g), 0)))
        out_specs.append(pl.BlockSpec((blk, cols), lambda *g: (step_index(*g), 0)))
        out_shapes.append(jax.ShapeDtypeStruct((rows, cols), BF16))
    return in_specs, out_specs, out_shapes


def _cast_blocks(src_refs, dst_refs):
    for src, dst in zip(src_refs, dst_refs, strict=True):
        dst[...] = src[...].astype(BF16)


def _mod_kernel(cond_ref, w_ref, b_ref, out_ref):
    s = jax.nn.silu(cond_ref[...]).astype(BF16)
    out_ref[...] = _dot(s, w_ref[...].astype(BF16)) + b_ref[...]


def _modulation(cond8, w_mod, b_mod):
    n = N_MOD * D_MODEL
    tn = n // MOD_COL_BLOCKS
    return pl.pallas_call(
        _mod_kernel,
        out_shape=jax.ShapeDtypeStruct((DEPTH, MOD_ROWS, n), F32),
        grid=(DEPTH, MOD_COL_BLOCKS),
        in_specs=[
            pl.BlockSpec((MOD_ROWS, D_MODEL), lambda l, j: (0, 0)),
            pl.BlockSpec((None, D_MODEL, tn), lambda l, j: (l, 0, j)),
            pl.BlockSpec((None, 1, tn), lambda l, j: (l, 0, j)),
        ],
        out_specs=pl.BlockSpec((None, MOD_ROWS, tn), lambda l, j: (l, 0, j)),
        compiler_params=pltpu.CompilerParams(
            dimension_semantics=("arbitrary", "arbitrary"), vmem_limit_bytes=VMEM_LIMIT),
        name="modulation",
    )(cond8, w_mod, b_mod.reshape(DEPTH, 1, n))


def _ctx_front_kernel(*refs, seq_len, n_cast):
    (x_ref, mod_ref, n1g_ref, win_ref, bd_ref, qg_ref, kg_ref, wpool_ref, pscale_ref, wout_ref,
     n2g_ref) = refs[:11]
    cast_src = refs[11:11 + n_cast]
    kt_out, vt_out, x1_out, h2_out = refs[11 + n_cast:15 + n_cast]
    _cast_blocks(cast_src, refs[15 + n_cast:])
    tm = x_ref.shape[0]
    sh1, sc1 = mod_ref[0:1, :], mod_ref[1:2, :]
    bd = bd_ref[...]
    t_loc = lax.broadcasted_iota(jnp.int32, (seq_len, LANES), 0)

    def sequence(s):
        rows = slice(s * seq_len, (s + 1) * seq_len)
        x = x_ref[rows, :]
        h = _rms_mod(x, n1g_ref[...], 1.0 + sc1, sh1).astype(BF16)
        proj = _dot(h, win_ref[...])
        yield
        q = (_head_norm(proj[:, 0:ATTN_WIDTH], bd, qg_ref[...])
             * (HEAD_DIM ** -0.5 * LOG2_E)).astype(BF16)
        k = _head_norm(proj[:, ATTN_WIDTH:ATTN_WIDTH + KV_WIDTH], bd[0:KV_WIDTH, 0:KV_WIDTH],
                       kg_ref[...])
        yield
        v = proj[:, ATTN_WIDTH + KV_WIDTH:ATTN_WIDTH + 2 * KV_WIDTH]
        u = proj[:, ATTN_WIDTH + 2 * KV_WIDTH:IN_WIDTH]
        kt_out[s] = k.T
        vt = v.T
        vt_out[s] = vt
        attn = yield from _attend_keys_major(q, _dup_halves(k), _value_rows(vt), seq_len,
                                             CTX_SCORE_LOOKAHEAD)
        mixed = _pool_mix(u, t_loc, seq_len, wpool_ref, 0, seq_len) * pscale_ref[...]
        yield
        _mix_residual_norm2(x, attn, mixed, wout_ref, mod_ref, n2g_ref, x1_out, h2_out, rows)

    _interleave([sequence(s) for s in range(tm // seq_len)], CTX_SEQ_SKEW)


def _ctx_front(layer, xp, seq_len, mods, norm1_g, w_in, bd, qg, kg, w_pool, pool_scale, w_out, norm2_g,
               cast_weights):
    n_tok = xp.shape[0]
    tm = CTX_TILE
    const2 = lambda i: (0, 0)
    tile = lambda i: (i, 0)
    cast_in, cast_out, cast_shapes = _cast_specs(cast_weights, layer, n_tok // tm, lambda i: i)
    return pl.pallas_call(
        functools.partial(_ctx_front_kernel, seq_len=seq_len, n_cast=len(cast_weights)),
        out_shape=(
            jax.ShapeDtypeStruct((n_tok // seq_len, KV_WIDTH, seq_len), F32),
            jax.ShapeDtypeStruct((n_tok // seq_len, KV_WIDTH, seq_len), F32),
            jax.ShapeDtypeStruct((n_tok, D_MODEL), F32),
            jax.ShapeDtypeStruct((n_tok, D_MODEL), BF16),
            *cast_shapes,
        ),
        grid=(n_tok // tm,),
        in_specs=[
            pl.BlockSpec((tm, D_MODEL), tile),
            pl.BlockSpec((None, None, N_MOD, D_MODEL), lambda i: (layer, 0, 0, 0)),
            pl.BlockSpec((1, D_MODEL), const2),
            pl.BlockSpec((D_MODEL, IN_WIDTH), const2),
            pl.BlockSpec((ATTN_WIDTH, ATTN_WIDTH), const2),
            pl.BlockSpec((1, ATTN_WIDTH), const2),
            pl.BlockSpec((1, KV_WIDTH), const2),
            pl.BlockSpec((None, len(POOL_WINDOWS), POOL_GROUP_DIM, POOL_GROUP_DIM), lambda i: (layer, 0, 0, 0)),
            pl.BlockSpec((1, POOL_WIDTH), const2),
            pl.BlockSpec((D_MODEL, D_MODEL), const2),
            pl.BlockSpec((1, D_MODEL), const2),
            *cast_in,
        ],
        out_specs=(
            pl.BlockSpec((tm // seq_len, KV_WIDTH, seq_len), lambda i: (i, 0, 0)),
            pl.BlockSpec((tm // seq_len, KV_WIDTH, seq_len), lambda i: (i, 0, 0)),
            pl.BlockSpec((tm, D_MODEL), tile),
            pl.BlockSpec((tm, D_MODEL), tile),
            *cast_out,
        ),
        compiler_params=pltpu.CompilerParams(
            dimension_semantics=("arbitrary",), vmem_limit_bytes=VMEM_LIMIT),
        name=f"ctx_front_{layer}",
    )(xp, mods, norm1_g, w_in, bd, qg, kg, w_pool, pool_scale, w_out, norm2_g, *cast_weights)


def _lat_proj_kernel(xprev_ref, x_ref, xnext_ref, mod_ref, n1g_ref, win_ref, bd_ref, qg_ref, kg_ref,
                     cos_ref, sin_ref, wpool_ref, pscale_ref, q_out, k_out, v_out, mixed_out, *, seq_len):
    tm = x_ref.shape[0]
    halo = POOL_HALO
    sub = LAT_SUB_TILE
    sh1, sc1 = mod_ref[0:1, :], mod_ref[1:2, :]
    bd = bd_ref[...]
    tile_start = pl.program_id(1) * tm

    def sub_tile(r0):
        rows = slice(r0, r0 + sub)
        before = xprev_ref[...] if r0 == 0 else x_ref[r0 - halo:r0, :]
        after = xnext_ref[...] if r0 + sub == tm else x_ref[r0 + sub:r0 + sub + halo, :]
        xe = jnp.concatenate([before, x_ref[rows, :], after], axis=0)
        h = _rms_mod(xe, n1g_ref[...], 1.0 + sc1, sh1).astype(BF16)
        proj = _dot(h, win_ref[...])
        yield
        main = proj[halo:halo + sub]
        cos_t, sin_t = cos_ref[rows, :], sin_ref[rows, :]
        q = _head_norm(main[:, 0:ATTN_WIDTH], bd, qg_ref[...])
        q_out[rows, :] = (_rope(q, cos_t, sin_t) * (HEAD_DIM ** -0.5 * LOG2_E)).astype(BF16)
        yield
        k = _head_norm(main[:, ATTN_WIDTH:ATTN_WIDTH + KV_WIDTH], bd[0:KV_WIDTH, 0:KV_WIDTH],
                       kg_ref[...])
        k_out[rows, :] = _rope(k, cos_t, sin_t).astype(BF16)
        v_out[rows, :] = main[:, ATTN_WIDTH + KV_WIDTH:ATTN_WIDTH + 2 * KV_WIDTH].astype(BF16)
        yield
        u = proj[:, ATTN_WIDTH + 2 * KV_WIDTH:IN_WIDTH]
        t_loc = (lax.broadcasted_iota(jnp.int32, (sub + 2 * halo, LANES), 0)
                 + (tile_start + r0 - halo))
        mixed = _pool_mix(u, t_loc, seq_len, wpool_ref, halo, sub) * pscale_ref[...]
        mixed_out[rows, :] = mixed.astype(BF16)

    _interleave([sub_tile(r0) for r0 in range(0, tm, sub)], LAT_SUB_SKEW)


def _lat_proj(layer, xs, mods, norm1_g, w_in, bd, qg, kg, cos_t, sin_t, w_pool, pool_scale):
    nb, seq_len, _ = xs.shape
    tm = LAT_TILE
    halo = POOL_HALO
    n_halo_blocks = seq_len // halo
    const2 = lambda b, i: (0, 0)
    tile = lambda b, i: (b, i, 0)
    return pl.pallas_call(
        functools.partial(_lat_proj_kernel, seq_len=seq_len),
        out_shape=(
            jax.ShapeDtypeStruct((nb, seq_len, ATTN_WIDTH), BF16),
            jax.ShapeDtypeStruct((nb, seq_len, KV_WIDTH), BF16),
            jax.ShapeDtypeStruct((nb, seq_len, KV_WIDTH), BF16),
            jax.ShapeDtypeStruct((nb, seq_len, POOL_WIDTH), BF16),
        ),
        grid=(nb, seq_len // tm),
        in_specs=[
            pl.BlockSpec((None, halo, D_MODEL),
                         lambda b, i: (b, jnp.maximum(i * (tm // halo) - 1, 0), 0)),
            pl.BlockSpec((None, tm, D_MODEL), tile),
            pl.BlockSpec((None, halo, D_MODEL),
                         lambda b, i: (b, jnp.minimum((i + 1) * (tm // halo), n_halo_blocks - 1), 0)),
            pl.BlockSpec((None, None, N_MOD, D_MODEL), lambda b, i: (layer, 1 + b, 0, 0)),
            pl.BlockSpec((1, D_MODEL), const2),
            pl.BlockSpec((D_MODEL, IN_WIDTH), const2),
            pl.BlockSpec((ATTN_WIDTH, ATTN_WIDTH), const2),
            pl.BlockSpec((1, ATTN_WIDTH), const2),
            pl.BlockSpec((1, KV_WIDTH), const2),
            pl.BlockSpec((tm, LANES), lambda b, i: (i, 0)),
            pl.BlockSpec((tm, LANES), lambda b, i: (i, 0)),
            pl.BlockSpec((None, len(POOL_WINDOWS), POOL_GROUP_DIM, POOL_GROUP_DIM), lambda b, i: (layer, 0, 0, 0)),
            pl.BlockSpec((1, POOL_WIDTH), const2),
        ],
        out_specs=(
            pl.BlockSpec((None, tm, ATTN_WIDTH), tile),
            pl.BlockSpec((None, tm, KV_WIDTH), tile),
            pl.BlockSpec((None, tm, KV_WIDTH), tile),
            pl.BlockSpec((None, tm, POOL_WIDTH), tile),
        ),
        compiler_params=pltpu.CompilerParams(
            dimension_semantics=("arbitrary", "arbitrary"), vmem_limit_bytes=VMEM_LIMIT),
        name=f"lat_proj_{layer}",
    )(xs, xs, xs, mods, norm1_g, w_in, bd, qg, kg, cos_t, sin_t, w_pool, pool_scale)


def _lat_attn_kernel(*refs, n_cast):
    (q_ref, klat_ref, vlat_ref, ck_ref, cv_ref, mixed_ref, x_ref, mod_ref, wout_ref,
     n2g_ref) = refs[:10]
    cast_src = refs[10:10 + n_cast]
    x1_out, h2_out = refs[10 + n_cast:12 + n_cast]
    cast_dst = refs[12 + n_cast:12 + 2 * n_cast]
    kdup_scr, vt_scr = refs[12 + 2 * n_cast:]
    _cast_blocks(cast_src, cast_dst)

    @pl.when(pl.program_id(1) == 0)
    def _():
        k_all = jnp.concatenate([ck_ref[...].T, klat_ref[...].astype(F32)], axis=0)
        vt_all = jnp.concatenate([cv_ref[...], vlat_ref[...].astype(F32).T], axis=1)
        k0, k1 = _dup_halves(k_all)
        kdup_scr[0] = k0
        kdup_scr[1] = k1
        for i, vt in enumerate(_value_rows(vt_all)):
            vt_scr[i] = vt

    attn = _run(_attend_keys_major(q_ref[...], kdup_scr, vt_scr, LAT_Q_BLOCK, LAT_SCORE_LOOKAHEAD))
    _mix_residual_norm2(x_ref[...], attn, mixed_ref[...], wout_ref, mod_ref, n2g_ref, x1_out, h2_out)


def _lat_attn(layer, q, k, v, cache_k, cache_v, mixed, xs, mods, w_out, norm2_g, cast_weights, cast_layer):
    nb, seq_len, _ = xs.shape
    past = cache_k.shape[3]
    tq = LAT_Q_TILE
    n_keys = past + seq_len
    n_tiles = seq_len // tq
    tile = lambda b, j: (b, j, 0)
    whole = lambda b, j: (b, 0, 0)
    cast_in, cast_out, cast_shapes = _cast_specs(cast_weights, cast_layer, nb * n_tiles,
                                                 lambda b, j: b * n_tiles + j)
    return pl.pallas_call(
        functools.partial(_lat_attn_kernel, n_cast=len(cast_weights)),
        out_shape=(
            jax.ShapeDtypeStruct((nb, seq_len, D_MODEL), F32),
            jax.ShapeDtypeStruct((nb, seq_len, D_MODEL), BF16),
            *cast_shapes,
        ),
        grid=(nb, seq_len // tq),
        in_specs=[
            pl.BlockSpec((None, tq, ATTN_WIDTH), tile),
            pl.BlockSpec((None, seq_len, KV_WIDTH), whole),
            pl.BlockSpec((None, seq_len, KV_WIDTH), whole),
            pl.BlockSpec((None, None, KV_WIDTH, past), lambda b, j: (b, layer, 0, 0)),
            pl.BlockSpec((None, None, KV_WIDTH, past), lambda b, j: (b, layer, 0, 0)),
            pl.BlockSpec((None, tq, POOL_WIDTH), tile),
            pl.BlockSpec((None, tq, D_MODEL), tile),
            pl.BlockSpec((None, None, N_MOD, D_MODEL), lambda b, j: (layer, 1 + b, 0, 0)),
            pl.BlockSpec((D_MODEL, D_MODEL), lambda b, j: (0, 0)),
            pl.BlockSpec((1, D_MODEL), lambda b, j: (0, 0)),
            *cast_in,
        ],
        out_specs=(
            pl.BlockSpec((None, tq, D_MODEL), tile),
            pl.BlockSpec((None, tq, D_MODEL), tile),
            *cast_out,
        ),
        scratch_shapes=[
            pltpu.VMEM((N_KV_HEADS, n_keys, LANES), BF16),
            pltpu.VMEM((2 * N_KV_HEADS, LANES, n_keys), BF16),
        ],
        compiler_params=pltpu.CompilerParams(
            dimension_semantics=("arbitrary", "arbitrary"), vmem_limit_bytes=VMEM_LIMIT),
        name=f"lat_attn_{layer}",
    )(q, k, v, cache_k, cache_v, mixed, xs, mods, w_out, norm2_g, *cast_weights)


def _ffn_kernel(*refs, seq_len, final_norm, has_halo):
    if has_halo:
        (x1_ref, h2_ref, h2prev_ref, h2next_ref, mod_ref, wup_ref, cw_ref, cb_ref, wdn_ref, fg_ref,
         out_ref, z_scr, act_scr) = refs
    else:
        (x1_ref, h2_ref, mod_ref, wup_ref, cw_ref, cb_ref, wdn_ref, fg_ref,
         out_ref, z_scr, act_scr) = refs
    tm = x1_ref.shape[0]
    gap = F32_SUBLANES
    if has_halo:
        halo = BF16_SUBLANES
        tile_start = pl.program_id(1) * tm
        h2prev = jnp.where(tile_start % seq_len != 0, h2prev_ref[...], jnp.zeros_like(h2prev_ref))
        h2next = jnp.where((tile_start + tm) % seq_len != 0, h2next_ref[...], jnp.zeros_like(h2next_ref))
        he = jnp.concatenate([h2prev, h2_ref[...], h2next], axis=0)
        segments = [(halo, tm)]
    else:
        he = h2_ref[...]
        segments = [(gap + s * (seq_len + gap), seq_len) for s in range(tm // seq_len)]
        for slab in range(z_scr.shape[0]):
            for s in range(tm // seq_len + 1):
                r = s * (seq_len + gap)
                z_scr[slab, r:r + gap, :] = jnp.zeros((gap, LANES), F32)

    def store_z(slab, z):
        if has_halo:
            z_scr[slab, 0:z.shape[0], :] = z
        else:
            for s, (row0, n) in enumerate(segments):
                z_scr[slab, row0:row0 + n, :] = z[s * n:(s + 1) * n]

    def conv_rows(slab, row, n, col0):
        cols = slice(col0, col0 + LANES)
        return (z_scr[slab, pl.ds(row - 1, n, stride=1), :] * cw_ref[0:1, cols]
                + z_scr[slab, row:row + n, :] * cw_ref[1:2, cols]
                + z_scr[slab, pl.ds(row + 1, n, stride=1), :] * cw_ref[2:3, cols]
                + cb_ref[0:1, cols])

    n_sub = FFN_CHUNK // LANES
    for ci in range(D_FF // FFN_CHUNK):
        c0 = ci * FFN_CHUNK
        slab0 = 2 * n_sub * (ci % 2)
        za = _dot(he, wup_ref[:, c0:c0 + FFN_CHUNK])
        zg = _dot(he, wup_ref[:, D_FF + c0:D_FF + c0 + FFN_CHUNK])
        for j in range(n_sub):
            lanes = slice(j * LANES, (j + 1) * LANES)
            slab_a, slab_g = slab0 + 2 * j, slab0 + 2 * j + 1
            col = c0 + j * LANES
            store_z(slab_a, za[:, lanes])
            store_z(slab_g, zg[:, lanes])
            for s, (row0, n) in enumerate(segments):
                a = conv_rows(slab_a, row0, n, col)
                g = conv_rows(slab_g, row0, n, D_FF + col)
                act_scr[s * n:(s + 1) * n, col:col + LANES] = (jax.nn.silu(a) * g).astype(BF16)
    x2 = x1_ref[...] + mod_ref[5:6, :] * _dot(act_scr[...], wdn_ref[...])
    if final_norm:
        ms = jnp.mean(x2 * x2, axis=-1, keepdims=True)
        x2 = x2 * lax.rsqrt(ms + EPS) * fg_ref[...]
    out_ref[...] = x2


def _ffn(layer, x1, h2, seq_len, mod_row0, mods, w_up, conv_w, conv_b, w_down, final_g, final_norm):
    nb, n_tok, _ = x1.shape
    tm = FFN_TILE
    has_halo = seq_len > tm
    assert seq_len % tm == 0 or tm % seq_len == 0
    halo = BF16_SUBLANES
    gap = F32_SUBLANES
    n_halo_blocks = n_tok // halo
    tile = lambda b, i: (b, i, 0)
    halo_specs = [
        pl.BlockSpec((None, halo, D_MODEL),
                     lambda b, i: (b, jnp.maximum(i * (tm // halo) - 1, 0), 0)),
        pl.BlockSpec((None, halo, D_MODEL),
                     lambda b, i: (b, jnp.minimum((i + 1) * (tm // halo), n_halo_blocks - 1), 0)),
    ] if has_halo else []
    halo_args = [h2, h2] if has_halo else []
    z_rows = tm + 2 * halo if has_halo else gap + (tm // seq_len) * (seq_len + gap)
    return pl.pallas_call(
        functools.partial(_ffn_kernel, seq_len=seq_len, final_norm=final_norm, has_halo=has_halo),
        out_shape=jax.ShapeDtypeStruct((nb, n_tok, D_MODEL), F32),
        grid=(nb, n_tok // tm),
        in_specs=[
            pl.BlockSpec((None, tm, D_MODEL), tile),
            pl.BlockSpec((None, tm, D_MODEL), tile),
            *halo_specs,
            pl.BlockSpec((None, None, N_MOD, D_MODEL), lambda b, i: (layer, mod_row0 + b, 0, 0)),
            pl.BlockSpec((D_MODEL, 2 * D_FF), lambda b, i: (0, 0), pipeline_mode=pl.Buffered(1)),
            pl.BlockSpec((None, 3, 2 * D_FF), lambda b, i: (layer, 0, 0)),
            pl.BlockSpec((None, 1, 2 * D_FF), lambda b, i: (layer, 0, 0)),
            pl.BlockSpec((D_FF, D_MODEL), lambda b, i: (0, 0), pipeline_mode=pl.Buffered(1)),
            pl.BlockSpec((1, D_MODEL), lambda b, i: (0, 0)),
        ],
        out_specs=pl.BlockSpec((None, tm, D_MODEL), tile),
        scratch_shapes=[
            pltpu.VMEM((4 * (FFN_CHUNK // LANES), z_rows, LANES), F32),
            pltpu.VMEM((tm, D_FF), BF16),
        ],
        compiler_params=pltpu.CompilerParams(
            dimension_semantics=("arbitrary", "arbitrary"), vmem_limit_bytes=VMEM_LIMIT),
        name=f"ffn_{'lat' if nb > 1 else 'ctx'}_{layer}",
    )(x1, h2, *halo_args, mods, w_up, conv_w, conv_b, w_down, final_g)


def _rope_tables(n_tokens):
    t = jnp.arange(n_tokens)
    row = (t // GRID_W).astype(F32)
    col = (t % GRID_W).astype(F32)
    n_freq = HEAD_DIM // 4
    inv = ROPE_THETA ** (-jnp.arange(n_freq, dtype=F32) / n_freq)
    ang = jnp.stack([row[:, None] * inv, col[:, None] * inv], axis=1)
    cos = jnp.broadcast_to(jnp.cos(ang)[:, :, None, :], (n_tokens, 2, 2, n_freq))
    sin = jnp.sin(ang)[:, :, None, :] * jnp.array([-1.0, 1.0], F32)[None, None, :, None]
    cos = cos.reshape(n_tokens, HEAD_DIM)
    sin = sin.reshape(n_tokens, HEAD_DIM)
    return jnp.tile(cos, (1, 2)), jnp.tile(sin, (1, 2))


def kernel(x_prompt, x_sample, cache_k, cache_v, c, c_ctx, w_mod, b_mod, norm1_g, w_in, q_norm_g, k_norm_g, w_pool, pool_scale, w_out, norm2_g, w_up, conv_w, conv_b, w_down, final_norm_g):
    batch, seq, d = x_prompt.shape
    dec_batch, dec_seq, _ = x_sample.shape
    past = cache_k.shape[2]

    w_in_l = w_in[0].astype(BF16)
    w_out_l = w_out[0].astype(BF16)
    w_pool_b = w_pool.astype(BF16)

    assert 1 + dec_batch <= MOD_ROWS
    cond8 = jnp.zeros((MOD_ROWS, d), F32).at[0].set(c_ctx).at[1:1 + dec_batch].set(c)
    mods = _modulation(cond8, w_mod, b_mod).reshape(DEPTH, MOD_ROWS, N_MOD, d)

    head_id = jnp.arange(ATTN_WIDTH) // HEAD_DIM
    blockdiag = (head_id[:, None] == head_id[None, :]).astype(BF16)
    cos_t, sin_t = _rope_tables(dec_seq)
    ck = jnp.transpose(cache_k, (0, 1, 3, 4, 2)).reshape(dec_batch, DEPTH, KV_WIDTH, past)
    cv = jnp.transpose(cache_v, (0, 1, 3, 4, 2)).reshape(dec_batch, DEPTH, KV_WIDTH, past)
    final_g = final_norm_g.reshape(1, d)

    xp = x_prompt.reshape(batch * seq, d)
    xs = x_sample
    ks_out, vs_out = [], []
    for l in range(DEPTH):
        n1g = norm1_g[l].reshape(1, d)
        n2g = norm2_g[l].reshape(1, d)
        qg = jnp.tile(q_norm_g[l], N_HEADS).reshape(1, ATTN_WIDTH)
        kg = jnp.tile(k_norm_g[l], N_KV_HEADS).reshape(1, KV_WIDTH)
        ps = pool_scale[l].reshape(1, POOL_WIDTH)
        cb = conv_b.reshape(DEPTH, 1, 2 * D_FF)
        last = l == DEPTH - 1

        k_l, v_l, x1p, h2p, w_up_l, w_down_l = _ctx_front(
            l, xp, seq, mods, n1g, w_in_l, blockdiag, qg, kg, w_pool_b, ps, w_out_l, n2g,
            (w_up, w_down))
        ks_out.append(k_l)
        vs_out.append(v_l)
        xp = _ffn(l, x1p[None], h2p[None], seq, 0, mods, w_up_l, conv_w, cb, w_down_l,
                  final_g, last)[0]

        q, k, v, mixed = _lat_proj(l, xs, mods, n1g, w_in_l, blockdiag, qg, kg, cos_t, sin_t,
                                   w_pool_b, ps)
        x1s, h2s, *next_weights = _lat_attn(l, q, k, v, ck, cv, mixed, xs, mods, w_out_l, n2g,
                                            () if last else (w_in, w_out), l + 1)
        xs = _ffn(l, x1s, h2s, dec_seq, 1, mods, w_up_l, conv_w, cb, w_down_l, final_g, last)
        if not last:
            w_in_l, w_out_l = next_weights

    y_prompt = xp.reshape(batch, seq, d)

    def cache_layout(per_layer):
        t = jnp.stack(per_layer, axis=1).reshape(batch, DEPTH, N_KV_HEADS, HEAD_DIM, seq)
        return jnp.transpose(t, (0, 1, 4, 2, 3))

    new_cache_k = cache_layout(ks_out)
    new_cache_v = cache_layout(vs_out)
    return (y_prompt, xs, new_cache_k, new_cache_v)
```

```python
import functools

import jax
import jax.numpy as jnp
from jax import lax
from jax.experimental import pallas as pl
from jax.experimental.pallas import tpu as pltpu

D_MODEL = 1024
DEPTH = 4
GRID_W = 64
HEAD_DIM = 64
N_HEADS = 8
N_KV_HEADS = 2
ATTN_WIDTH = N_HEADS * HEAD_DIM
KV_WIDTH = N_KV_HEADS * HEAD_DIM
POOL_WIDTH = D_MODEL - ATTN_WIDTH
POOL_WINDOWS = (2, 4, 8, 16)
POOL_GROUP_DIM = 128
IN_WIDTH = ATTN_WIDTH + 2 * KV_WIDTH + POOL_WIDTH
D_FF = 2816
ROPE_THETA = 10000.0
ROPE_AXIS_DIM = HEAD_DIM // 2
ROPE_PAIR_DIM = HEAD_DIM // 4
EPS = 1e-6
N_MOD = 6
LOG2_E = 1.4426950408889634

LANES = 128
F32_SUBLANES = 8
BF16_SUBLANES = 16
VMEM_LIMIT = 56 * 1024 * 1024

MOD_ROWS = F32_SUBLANES
MOD_COL_BLOCKS = 4
POOL_HALO = 8
FFN_CHUNK = 256
CTX_TILE = 1024
LAT_TILE = 2048
LAT_SUB_TILE = 512
LAT_SUB_SKEW = 1
LAT_Q_TILE = 512
LAT_Q_BLOCK = 128
FFN_TILE = 512
LAT_SCORE_LOOKAHEAD = 4
CTX_SCORE_LOOKAHEAD = 2
CTX_SEQ_SKEW = 1

BF16 = jnp.bfloat16
F32 = jnp.float32


def _dot(a, b):
    return jnp.dot(a, b, preferred_element_type=F32)


def _dot_nt(a, b):
    return lax.dot_general(a, b, (((1,), (1,)), ((), ())), preferred_element_type=F32)


def _rms_mod(x, g, scale1p, shift):
    ms = jnp.mean(x * x, axis=-1, keepdims=True)
    return (x * lax.rsqrt(ms + EPS) * g) * scale1p + shift


def _head_norm(t, blockdiag, g_tiled):
    ssq = _dot((t * t).astype(BF16), blockdiag)
    return t * lax.rsqrt(ssq * (1.0 / HEAD_DIM) + EPS) * g_tiled


def _rope(t, cos_t, sin_t):
    lane = lax.broadcasted_iota(jnp.int32, (t.shape[0], LANES), 1)
    first = (lane % ROPE_AXIS_DIM) < ROPE_PAIR_DIM
    outs = []
    for ci in range(t.shape[1] // LANES):
        tc = t[:, ci * LANES:(ci + 1) * LANES]
        partner = jnp.where(first, pltpu.roll(tc, LANES - ROPE_PAIR_DIM, 1),
                            pltpu.roll(tc, ROPE_PAIR_DIM, 1))
        outs.append(tc * cos_t + partner * sin_t)
    return outs[0] if len(outs) == 1 else jnp.concatenate(outs, axis=1)


def _dup_halves(t):
    lane = lax.broadcasted_iota(jnp.int32, t.shape, 1)
    lo = lane < HEAD_DIM
    sw = pltpu.roll(t, HEAD_DIM, 1)
    return jnp.where(lo, t, sw).astype(BF16), jnp.where(lo, sw, t).astype(BF16)


def _value_rows(vt):
    ones = jnp.ones((HEAD_DIM, vt.shape[1]), F32)
    out = []
    for g in range(N_KV_HEADS):
        vg = vt[g * HEAD_DIM:(g + 1) * HEAD_DIM]
        out.append(jnp.concatenate([vg, ones], axis=0).astype(BF16))
        out.append(jnp.concatenate([ones, vg], axis=0).astype(BF16))
    return out


def _interleave(chains, skew):
    live = list(enumerate(chains))
    t = 0
    while live:
        for i, chain in list(live):
            if t >= i * skew and next(chain, StopIteration) is StopIteration:
                live.remove((i, chain))
        t += 1


def _run(chain):
    while True:
        try:
            next(chain)
        except StopIteration as stop:
            return stop.value


def _attend_keys_major(q, kdup_ref, vt_ref, q_block, lookahead):
    tq = q.shape[0]
    lane = lax.broadcasted_iota(jnp.int32, (1, LANES), 1)
    masks = ((lane < HEAD_DIM).astype(BF16), (lane >= HEAD_DIM).astype(BF16))
    units = [(r, g, half) for r in range(0, tq, q_block) for g in range(N_KV_HEADS) for half in range(2)]

    def scores(r, g, half):
        q_a = q[r:r + q_block, (2 * g) * LANES:(2 * g + 1) * LANES]
        q_b = q[r:r + q_block, (2 * g + 1) * LANES:(2 * g + 2) * LANES]
        q_rows = jnp.concatenate([q_a * masks[half], q_b * masks[half]], axis=0)
        return _dot_nt(kdup_ref[g], q_rows)

    def values(r, g, half, st):
        pt = jnp.exp2(st - jnp.max(st, axis=0, keepdims=True)).astype(BF16)
        return _dot(vt_ref[2 * g + half], pt).T

    outs = {}
    pending = []
    for unit in units[:lookahead]:
        pending.append(scores(*unit))
        yield
    for i, unit in enumerate(units):
        if i + lookahead < len(units):
            pending.append(scores(*units[i + lookahead]))
            yield
        outs[unit] = values(*unit, pending.pop(0))
        yield

    lo = lax.broadcasted_iota(jnp.int32, (2 * q_block, LANES), 1) < HEAD_DIM
    rows = []
    for r in range(0, tq, q_block):
        blocks = []
        for g in range(N_KV_HEADS):
            o_even, o_odd = outs[(r, g, 0)], outs[(r, g, 1)]
            num = jnp.where(lo, o_even, o_odd)
            den = jnp.where(lo, pltpu.roll(o_even, HEAD_DIM, 1), pltpu.roll(o_odd, HEAD_DIM, 1))
            out = num / den
            blocks.extend([out[0:q_block], out[q_block:2 * q_block]])
        rows.append(jnp.concatenate(blocks, axis=1))
    return rows[0] if len(rows) == 1 else jnp.concatenate(rows, axis=0)


def _pool_mix(u, t_loc, seq_len, wpool_ref, lo_row, n_rows):
    m = u.shape[0]
    outs = []
    for gi, w in enumerate(POOL_WINDOWS):
        ug = u[:, gi * LANES:(gi + 1) * LANES]
        half = w // 2
        past, future, d = ug, ug, 1
        while d < half:
            past = past + jnp.where(t_loc >= d, pltpu.roll(past, d, 0), 0.0)
            future = future + jnp.where(t_loc + d < seq_len, pltpu.roll(future, m - d, 0), 0.0)
            d *= 2
        total = jnp.where(t_loc >= 1, pltpu.roll(past, 1, 0), 0.0) + future
        cnt = jnp.minimum(t_loc + half, seq_len) - jnp.maximum(t_loc - half, 0)
        pooled = total / cnt.astype(F32) - ug
        pooled = pooled[lo_row:lo_row + n_rows]
        outs.append(_dot(pooled.astype(BF16), wpool_ref[gi]))
    return jnp.concatenate(outs, axis=1)


def _mix_residual_norm2(x, attn, mixed, wout_ref, mod_ref, n2g_ref, x1_out, h2_out, rows=slice(None)):
    gt1, sh2, sc2 = mod_ref[2:3, :], mod_ref[3:4, :], mod_ref[4:5, :]
    mix = (_dot(attn.astype(BF16), wout_ref[0:ATTN_WIDTH, :])
           + _dot(mixed.astype(BF16), wout_ref[ATTN_WIDTH:D_MODEL, :]))
    x1 = x + gt1 * mix
    x1_out[rows, :] = x1
    h2_out[rows, :] = _rms_mod(x1, n2g_ref[...], 1.0 + sc2, sh2).astype(BF16)


def _cast_specs(weights, layer, n_steps, step_index):
    in_specs, out_specs, out_shapes = [], [], []
    for w in weights:
        _, rows, cols = w.shape
        blk = rows // n_steps
        assert blk * n_steps == rows and blk % BF16_SUBLANES == 0
        in_specs.append(pl.BlockSpec((None, blk, cols), lambda *g: (layer, step_index(*g), 0)))
        out_specs.append(pl.BlockSpec((blk, cols), lambda *g: (step_index(*g), 0)))
        out_shapes.append(jax.ShapeDtypeStruct((rows, cols), BF16))
    return in_specs, out_specs, out_shapes


def _cast_blocks(src_refs, dst_refs):
    for src, dst in zip(src_refs, dst_refs, strict=True):
        dst[...] = src[...].astype(BF16)


def _mod_kernel(cond_ref, w_ref, b_ref, out_ref):
    s = jax.nn.silu(cond_ref[...]).astype(BF16)
    out_ref[...] = _dot(s, w_ref[...].astype(BF16)) + b_ref[...]


def _modulation(cond8, w_mod, b_mod):
    n = N_MOD * D_MODEL
    tn = n // MOD_COL_BLOCKS
    return pl.pallas_call(
        _mod_kernel,
        out_shape=jax.ShapeDtypeStruct((DEPTH, MOD_ROWS, n), F32),
        grid=(DEPTH, MOD_COL_BLOCKS),
        in_specs=[
            pl.BlockSpec((MOD_ROWS, D_MODEL), lambda l, j: (0, 0)),
            pl.BlockSpec((None, D_MODEL, tn), lambda l, j: (l, 0, j)),
            pl.BlockSpec((None, 1, tn), lambda l, j: (l, 0, j)),
        ],
        out_specs=pl.BlockSpec((None, MOD_ROWS, tn), lambda l, j: (l, 0, j)),
        compiler_params=pltpu.CompilerParams(
            dimension_semantics=("arbitrary", "arbitrary"), vmem_limit_bytes=VMEM_LIMIT),
        name="modulation",
    )(cond8, w_mod, b_mod.reshape(DEPTH, 1, n))


def _ctx_front_kernel(*refs, seq_len, n_cast):
    (x_ref, mod_ref, n1g_ref, win_ref, bd_ref, qg_ref, kg_ref, wpool_ref, pscale_ref, wout_ref,
     n2g_ref) = refs[:11]
    cast_src = refs[11:11 + n_cast]
    kt_out, vt_out, x1_out, h2_out = refs[11 + n_cast:15 + n_cast]
    _cast_blocks(cast_src, refs[15 + n_cast:])
    tm = x_ref.shape[0]
    sh1, sc1 = mod_ref[0:1, :], mod_ref[1:2, :]
    bd = bd_ref[...]
    t_loc = lax.broadcasted_iota(jnp.int32, (seq_len, LANES), 0)

    def sequence(s):
        rows = slice(s * seq_len, (s + 1) * seq_len)
        x = x_ref[rows, :]
        h = _rms_mod(x, n1g_ref[...], 1.0 + sc1, sh1).astype(BF16)
        proj = _dot(h, win_ref[...])
        yield
        q = (_head_norm(proj[:, 0:ATTN_WIDTH], bd, qg_ref[...])
             * (HEAD_DIM ** -0.5 * LOG2_E)).astype(BF16)
        k = _head_norm(proj[:, ATTN_WIDTH:ATTN_WIDTH + KV_WIDTH], bd[0:KV_WIDTH, 0:KV_WIDTH],
                       kg_ref[...])
        yield
        v = proj[:, ATTN_WIDTH + KV_WIDTH:ATTN_WIDTH + 2 * KV_WIDTH]
        u = proj[:, ATTN_WIDTH + 2 * KV_WIDTH:IN_WIDTH]
        kt_out[s] = k.T
        vt = v.T
        vt_out[s] = vt
        attn = yield from _attend_keys_major(q, _dup_halves(k), _value_rows(vt), seq_len,
                                             CTX_SCORE_LOOKAHEAD)
        mixed = _pool_mix(u, t_loc, seq_len, wpool_ref, 0, seq_len) * pscale_ref[...]
        yield
        _mix_residual_norm2(x, attn, mixed, wout_ref, mod_ref, n2g_ref, x1_out, h2_out, rows)

    _interleave([sequence(s) for s in range(tm // seq_len)], CTX_SEQ_SKEW)


def _ctx_front(layer, xp, seq_len, mods, norm1_g, w_in, bd, qg, kg, w_pool, pool_scale, w_out, norm2_g,
               cast_weights):
    n_tok = xp.shape[0]
    tm = CTX_TILE
    const2 = lambda i: (0, 0)
    tile = lambda i: (i, 0)
    cast_in, cast_out, cast_shapes = _cast_specs(cast_weights, layer, n_tok // tm, lambda i: i)
    return pl.pallas_call(
        functools.partial(_ctx_front_kernel, seq_len=seq_len, n_cast=len(cast_weights)),
        out_shape=(
            jax.ShapeDtypeStruct((n_tok // seq_len, KV_WIDTH, seq_len), F32),
            jax.ShapeDtypeStruct((n_tok // seq_len, KV_WIDTH, seq_len), F32),
            jax.ShapeDtypeStruct((n_tok, D_MODEL), F32),
            jax.ShapeDtypeStruct((n_tok, D_MODEL), BF16),
            *cast_shapes,
        ),
        grid=(n_tok // tm,),
        in_specs=[
            pl.BlockSpec((tm, D_MODEL), tile),
            pl.BlockSpec((None, None, N_MOD, D_MODEL), lambda i: (layer, 0, 0, 0)),
            pl.BlockSpec((1, D_MODEL), const2),
            pl.BlockSpec((D_MODEL, IN_WIDTH), const2),
            pl.BlockSpec((ATTN_WIDTH, ATTN_WIDTH), const2),
            pl.BlockSpec((1, ATTN_WIDTH), const2),
            pl.BlockSpec((1, KV_WIDTH), const2),
            pl.BlockSpec((None, len(POOL_WINDOWS), POOL_GROUP_DIM, POOL_GROUP_DIM), lambda i: (layer, 0, 0, 0)),
            pl.BlockSpec((1, POOL_WIDTH), const2),
            pl.BlockSpec((D_MODEL, D_MODEL), const2),
            pl.BlockSpec((1, D_MODEL), const2),
            *cast_in,
        ],
        out_specs=(
            pl.BlockSpec((tm // seq_len, KV_WIDTH, seq_len), lambda i: (i, 0, 0)),
            pl.BlockSpec((tm // seq_len, KV_WIDTH, seq_len), lambda i: (i, 0, 0)),
            pl.BlockSpec((tm, D_MODEL), tile),
            pl.BlockSpec((tm, D_MODEL), tile),
            *cast_out,
        ),
        compiler_params=pltpu.CompilerParams(
            dimension_semantics=("arbitrary",), vmem_limit_bytes=VMEM_LIMIT),
        name=f"ctx_front_{layer}",
    )(xp, mods, norm1_g, w_in, bd, qg, kg, w_pool, pool_scale, w_out, norm2_g, *cast_weights)


def _lat_proj_kernel(xprev_ref, x_ref, xnext_ref, mod_ref, n1g_ref, win_ref, bd_ref, qg_ref, kg_ref,
                     cos_ref, sin_ref, wpool_ref, pscale_ref, q_out, k_out, v_out, mixed_out, *, seq_len):
    tm = x_ref.shape[0]
    halo = POOL_HALO
    sub = LAT_SUB_TILE
    sh1, sc1 = mod_ref[0:1, :], mod_ref[1:2, :]
    bd = bd_ref[...]
    tile_start = pl.program_id(1) * tm

    def sub_tile(r0):
        rows = slice(r0, r0 + sub)
        before = xprev_ref[...] if r0 == 0 else x_ref[r0 - halo:r0, :]
        after = xnext_ref[...] if r0 + sub == tm else x_ref[r0 + sub:r0 + sub + halo, :]
        xe = jnp.concatenate([before, x_ref[rows, :], after], axis=0)
        h = _rms_mod(xe, n1g_ref[...], 1.0 + sc1, sh1).astype(BF16)
        proj = _dot(h, win_ref[...])
        yield
        main = proj[halo:halo + sub]
        cos_t, sin_t = cos_ref[rows, :], sin_ref[rows, :]
        q = _head_norm(main[:, 0:ATTN_WIDTH], bd, qg_ref[...])
        q_out[rows, :] = (_rope(q, cos_t, sin_t) * (HEAD_DIM ** -0.5 * LOG2_E)).astype(BF16)
        yield
        k = _head_norm(main[:, ATTN_WIDTH:ATTN_WIDTH + KV_WIDTH], bd[0:KV_WIDTH, 0:KV_WIDTH],
                       kg_ref[...])
        k_out[rows, :] = _rope(k, cos_t, sin_t).astype(BF16)
        v_out[rows, :] = main[:, ATTN_WIDTH + KV_WIDTH:ATTN_WIDTH + 2 * KV_WIDTH].astype(BF16)
        yield
        u = proj[:, ATTN_WIDTH + 2 * KV_WIDTH:IN_WIDTH]
        t_loc = (lax.broadcasted_iota(jnp.int32, (sub + 2 * halo, LANES), 0)
                 + (tile_start + r0 - halo))
        mixed = _pool_mix(u, t_loc, seq_len, wpool_ref, halo, sub) * pscale_ref[...]
        mixed_out[rows, :] = mixed.astype(BF16)

    _interleave([sub_tile(r0) for r0 in range(0, tm, sub)], LAT_SUB_SKEW)


def _lat_proj(layer, xs, mods, norm1_g, w_in, bd, qg, kg, cos_t, sin_t, w_pool, pool_scale):
    nb, seq_len, _ = xs.shape
    tm = LAT_TILE
    halo = POOL_HALO
    n_halo_blocks = seq_len // halo
    const2 = lambda b, i: (0, 0)
    tile = lambda b, i: (b, i, 0)
    return pl.pallas_call(
        functools.partial(_lat_proj_kernel, seq_len=seq_len),
        out_shape=(
            jax.ShapeDtypeStruct((nb, seq_len, ATTN_WIDTH), BF16),
            jax.ShapeDtypeStruct((nb, seq_len, KV_WIDTH), BF16),
            jax.ShapeDtypeStruct((nb, seq_len, KV_WIDTH), BF16),
            jax.ShapeDtypeStruct((nb, seq_len, POOL_WIDTH), BF16),
        ),
        grid=(nb, seq_len // tm),
        in_specs=[
            pl.BlockSpec((None, halo, D_MODEL),
                         lambda b, i: (b, jnp.maximum(i * (tm // halo) - 1, 0), 0)),
            pl.BlockSpec((None, tm, D_MODEL), tile),
            pl.BlockSpec((None, halo, D_MODEL),
                         lambda b, i: (b, jnp.minimum((i + 1) * (tm // halo), n_halo_blocks - 1), 0)),
            pl.BlockSpec((None, None, N_MOD, D_MODEL), lambda b, i: (layer, 1 + b, 0, 0)),
            pl.BlockSpec((1, D_MODEL), const2),
            pl.BlockSpec((D_MODEL, IN_WIDTH), const2),
            pl.BlockSpec((ATTN_WIDTH, ATTN_WIDTH), const2),
            pl.BlockSpec((1, ATTN_WIDTH), const2),
            pl.BlockSpec((1, KV_WIDTH), const2),
            pl.BlockSpec((tm, LANES), lambda b, i: (i, 0)),
            pl.BlockSpec((tm, LANES), lambda b, i: (i, 0)),
            pl.BlockSpec((None, len(POOL_WINDOWS), POOL_GROUP_DIM, POOL_GROUP_DIM), lambda b, i: (layer, 0, 0, 0)),
            pl.BlockSpec((1, POOL_WIDTH), const2),
        ],
        out_specs=(
            pl.BlockSpec((None, tm, ATTN_WIDTH), tile),
            pl.BlockSpec((None, tm, KV_WIDTH), tile),
            pl.BlockSpec((None, tm, KV_WIDTH), tile),
            pl.BlockSpec((None, tm, POOL_WIDTH), tile),
        ),
        compiler_params=pltpu.CompilerParams(
            dimension_semantics=("arbitrary", "arbitrary"), vmem_limit_bytes=VMEM_LIMIT),
        name=f"lat_proj_{layer}",
    )(xs, xs, xs, mods, norm1_g, w_in, bd, qg, kg, cos_t, sin_t, w_pool, pool_scale)


def _lat_attn_kernel(*refs, n_cast):
    (q_ref, klat_ref, vlat_ref, ck_ref, cv_ref, mixed_ref, x_ref, mod_ref, wout_ref,
     n2g_ref) = refs[:10]
    cast_src = refs[10:10 + n_cast]
    x1_out, h2_out = refs[10 + n_cast:12 + n_cast]
    cast_dst = refs[12 + n_cast:12 + 2 * n_cast]
    kdup_scr, vt_scr = refs[12 + 2 * n_cast:]
    _cast_blocks(cast_src, cast_dst)

    @pl.when(pl.program_id(1) == 0)
    def _():
        k_all = jnp.concatenate([ck_ref[...].T, klat_ref[...].astype(F32)], axis=0)
        vt_all = jnp.concatenate([cv_ref[...], vlat_ref[...].astype(F32).T], axis=1)
        k0, k1 = _dup_halves(k_all)
        kdup_scr[0] = k0
        kdup_scr[1] = k1
        for i, vt in enumerate(_value_rows(vt_all)):
            vt_scr[i] = vt

    attn = _run(_attend_keys_major(q_ref[...], kdup_scr, vt_scr, LAT_Q_BLOCK, LAT_SCORE_LOOKAHEAD))
    _mix_residual_norm2(x_ref[...], attn, mixed_ref[...], wout_ref, mod_ref, n2g_ref, x1_out, h2_out)


def _lat_attn(layer, q, k, v, cache_k, cache_v, mixed, xs, mods, w_out, norm2_g, cast_weights, cast_layer):
    nb, seq_len, _ = xs.shape
    past = cache_k.shape[3]
    tq = LAT_Q_TILE
    n_keys = past + seq_len
    n_tiles = seq_len // tq
    tile = lambda b, j: (b, j, 0)
    whole = lambda b, j: (b, 0, 0)
    cast_in, cast_out, cast_shapes = _cast_specs(cast_weights, cast_layer, nb * n_tiles,
                                                 lambda b, j: b * n_tiles + j)
    return pl.pallas_call(
        functools.partial(_lat_attn_kernel, n_cast=len(cast_weights)),
        out_shape=(
            jax.ShapeDtypeStruct((nb, seq_len, D_MODEL), F32),
            jax.ShapeDtypeStruct((nb, seq_len, D_MODEL), BF16),
            *cast_shapes,
        ),
        grid=(nb, seq_len // tq),
        in_specs=[
            pl.BlockSpec((None, tq, ATTN_WIDTH), tile),
            pl.BlockSpec((None, seq_len, KV_WIDTH), whole),
            pl.BlockSpec((None, seq_len, KV_WIDTH), whole),
            pl.BlockSpec((None, None, KV_WIDTH, past), lambda b, j: (b, layer, 0, 0)),
            pl.BlockSpec((None, None, KV_WIDTH, past), lambda b, j: (b, layer, 0, 0)),
            pl.BlockSpec((None, tq, POOL_WIDTH), tile),
            pl.BlockSpec((None, tq, D_MODEL), tile),
            pl.BlockSpec((None, None, N_MOD, D_MODEL), lambda b, j: (layer, 1 + b, 0, 0)),
            pl.BlockSpec((D_MODEL, D_MODEL), lambda b, j: (0, 0)),
            pl.BlockSpec((1, D_MODEL), lambda b, j: (0, 0)),
            *cast_in,
        ],
        out_specs=(
            pl.BlockSpec((None, tq, D_MODEL), tile),
            pl.BlockSpec((None, tq, D_MODEL), tile),
            *cast_out,
        ),
        scratch_shapes=[
            pltpu.VMEM((N_KV_HEADS, n_keys, LANES), BF16),
            pltpu.VMEM((2 * N_KV_HEADS, LANES, n_keys), BF16),
        ],
        compiler_params=pltpu.CompilerParams(
            dimension_semantics=("arbitrary", "arbitrary"), vmem_limit_bytes=VMEM_LIMIT),
        name=f"lat_attn_{layer}",
    )(q, k, v, cache_k, cache_v, mixed, xs, mods, w_out, norm2_g, *cast_weights)


def _ffn_kernel(*refs, seq_len, final_norm, has_halo):
    if has_halo:
        (x1_ref, h2_ref, h2prev_ref, h2next_ref, mod_ref, wup_ref, cw_ref, cb_ref, wdn_ref, fg_ref,
         out_ref, z_scr, act_scr) = refs
    else:
        (x1_ref, h2_ref, mod_ref, wup_ref, cw_ref, cb_ref, wdn_ref, fg_ref,
         out_ref, z_scr, act_scr) = refs
    tm = x1_ref.shape[0]
    gap = F32_SUBLANES
    if has_halo:
        halo = BF16_SUBLANES
        tile_start = pl.program_id(1) * tm
        h2prev = jnp.where(tile_start % seq_len != 0, h2prev_ref[...], jnp.zeros_like(h2prev_ref))
        h2next = jnp.where((tile_start + tm) % seq_len != 0, h2next_ref[...], jnp.zeros_like(h2next_ref))
        he = jnp.concatenate([h2prev, h2_ref[...], h2next], axis=0)
        segments = [(halo, tm)]
    else:
        he = h2_ref[...]
        segments = [(gap + s * (seq_len + gap), seq_len) for s in range(tm // seq_len)]
        for slab in range(z_scr.shape[0]):
            for s in range(tm // seq_len + 1):
                r = s * (seq_len + gap)
                z_scr[slab, r:r + gap, :] = jnp.zeros((gap, LANES), F32)

    def store_z(slab, z):
        if has_halo:
            z_scr[slab, 0:z.shape[0], :] = z
        else:
            for s, (row0, n) in enumerate(segments):
                z_scr[slab, row0:row0 + n, :] = z[s * n:(s + 1) * n]

    def conv_rows(slab, row, n, col0):
        cols = slice(col0, col0 + LANES)
        return (z_scr[slab, pl.ds(row - 1, n, stride=1), :] * cw_ref[0:1, cols]
                + z_scr[slab, row:row + n, :] * cw_ref[1:2, cols]
                + z_scr[slab, pl.ds(row + 1, n, stride=1), :] * cw_ref[2:3, cols]
                + cb_ref[0:1, cols])

    n_sub = FFN_CHUNK // LANES
    for ci in range(D_FF // FFN_CHUNK):
        c0 = ci * FFN_CHUNK
        slab0 = 2 * n_sub * (ci % 2)
        za = _dot(he, wup_ref[:, c0:c0 + FFN_CHUNK])
        zg = _dot(he, wup_ref[:, D_FF + c0:D_FF + c0 + FFN_CHUNK])
        for j in range(n_sub):
            lanes = slice(j * LANES, (j + 1) * LANES)
            slab_a, slab_g = slab0 + 2 * j, slab0 + 2 * j + 1
            col = c0 + j * LANES
            store_z(slab_a, za[:, lanes])
            store_z(slab_g, zg[:, lanes])
            for s, (row0, n) in enumerate(segments):
                a = conv_rows(slab_a, row0, n, col)
                g = conv_rows(slab_g, row0, n, D_FF + col)
                act_scr[s * n:(s + 1) * n, col:col + LANES] = (jax.nn.silu(a) * g).astype(BF16)
    x2 = x1_ref[...] + mod_ref[5:6, :] * _dot(act_scr[...], wdn_ref[...])
    if final_norm:
        ms = jnp.mean(x2 * x2, axis=-1, keepdims=True)
        x2 = x2 * lax.rsqrt(ms + EPS) * fg_ref[...]
    out_ref[...] = x2


def _ffn(layer, x1, h2, seq_len, mod_row0, mods, w_up, conv_w, conv_b, w_down, final_g, final_norm):
    nb, n_tok, _ = x1.shape
    tm = FFN_TILE
    has_halo = seq_len > tm
    assert seq_len % tm == 0 or tm % seq_len == 0
    halo = BF16_SUBLANES
    gap = F32_SUBLANES
    n_halo_blocks = n_tok // halo
    tile = lambda b, i: (b, i, 0)
    halo_specs = [
        pl.BlockSpec((None, halo, D_MODEL),
                     lambda b, i: (b, jnp.maximum(i * (tm // halo) - 1, 0), 0)),
        pl.BlockSpec((None, halo, D_MODEL),
                     lambda b, i: (b, jnp.minimum((i + 1) * (tm // halo), n_halo_blocks - 1), 0)),
    ] if has_halo else []
    halo_args = [h2, h2] if has_halo else []
    z_rows = tm + 2 * halo if has_halo else gap + (tm // seq_len) * (seq_len + gap)
    return pl.pallas_call(
        functools.partial(_ffn_kernel, seq_len=seq_len, final_norm=final_norm, has_halo=has_halo),
        out_shape=jax.ShapeDtypeStruct((nb, n_tok, D_MODEL), F32),
        grid=(nb, n_tok // tm),
        in_specs=[
            pl.BlockSpec((None, tm, D_MODEL), tile),
            pl.BlockSpec((None, tm, D_MODEL), tile),
            *halo_specs,
            pl.BlockSpec((None, None, N_MOD, D_MODEL), lambda b, i: (layer, mod_row0 + b, 0, 0)),
            pl.BlockSpec((D_MODEL, 2 * D_FF), lambda b, i: (0, 0), pipeline_mode=pl.Buffered(1)),
            pl.BlockSpec((None, 3, 2 * D_FF), lambda b, i: (layer, 0, 0)),
            pl.BlockSpec((None, 1, 2 * D_FF), lambda b, i: (layer, 0, 0)),
            pl.BlockSpec((D_FF, D_MODEL), lambda b, i: (0, 0), pipeline_mode=pl.Buffered(1)),
            pl.BlockSpec((1, D_MODEL), lambda b, i: (0, 0)),
        ],
        out_specs=pl.BlockSpec((None, tm, D_MODEL), tile),
        scratch_shapes=[
            pltpu.VMEM((4 * (FFN_CHUNK // LANES), z_rows, LANES), F32),
            pltpu.VMEM((tm, D_FF), BF16),
        ],
        compiler_params=pltpu.CompilerParams(
            dimension_semantics=("arbitrary", "arbitrary"), vmem_limit_bytes=VMEM_LIMIT),
        name=f"ffn_{'lat' if nb > 1 else 'ctx'}_{layer}",
    )(x1, h2, *halo_args, mods, w_up, conv_w, conv_b, w_down, final_g)


def _rope_tables(n_tokens):
    t = jnp.arange(n_tokens)
    row = (t // GRID_W).astype(F32)
    col = (t % GRID_W).astype(F32)
    n_freq = HEAD_DIM // 4
    inv = ROPE_THETA ** (-jnp.arange(n_freq, dtype=F32) / n_freq)
    ang = jnp.stack([row[:, None] * inv, col[:, None] * inv], axis=1)
    cos = jnp.broadcast_to(jnp.cos(ang)[:, :, None, :], (n_tokens, 2, 2, n_freq))
    sin = jnp.sin(ang)[:, :, None, :] * jnp.array([-1.0, 1.0], F32)[None, None, :, None]
    cos = cos.reshape(n_tokens, HEAD_DIM)
    sin = sin.reshape(n_tokens, HEAD_DIM)
    return jnp.tile(cos, (1, 2)), jnp.tile(sin, (1, 2))


def kernel(x_prompt, x_sample, cache_k, cache_v, c, c_ctx, w_mod, b_mod, norm1_g, w_in, q_norm_g, k_norm_g, w_pool, pool_scale, w_out, norm2_g, w_up, conv_w, conv_b, w_down, final_norm_g):
    batch, seq, d = x_prompt.shape
    dec_batch, dec_seq, _ = x_sample.shape
    past = cache_k.shape[2]

    w_in_l = w_in[0].astype(BF16)
    w_out_l = w_out[0].astype(BF16)
    w_pool_b = w_pool.astype(BF16)

    assert 1 + dec_batch <= MOD_ROWS
    cond8 = jnp.zeros((MOD_ROWS, d), F32).at[0].set(c_ctx).at[1:1 + dec_batch].set(c)
    mods = _modulation(cond8, w_mod, b_mod).reshape(DEPTH, MOD_ROWS, N_MOD, d)

    head_id = jnp.arange(ATTN_WIDTH) // HEAD_DIM
    blockdiag = (head_id[:, None] == head_id[None, :]).astype(BF16)
    cos_t, sin_t = _rope_tables(dec_seq)
    ck = jnp.transpose(cache_k, (0, 1, 3, 4, 2)).reshape(dec_batch, DEPTH, KV_WIDTH, past)
    cv = jnp.transpose(cache_v, (0, 1, 3, 4, 2)).reshape(dec_batch, DEPTH, KV_WIDTH, past)
    final_g = final_norm_g.reshape(1, d)

    xp = x_prompt.reshape(batch * seq, d)
    xs = x_sample
    ks_out, vs_out = [], []
    for l in range(DEPTH):
        n1g = norm1_g[l].reshape(1, d)
        n2g = norm2_g[l].reshape(1, d)
        qg = jnp.tile(q_norm_g[l], N_HEADS).reshape(1, ATTN_WIDTH)
        kg = jnp.tile(k_norm_g[l], N_KV_HEADS).reshape(1, KV_WIDTH)
        ps = pool_scale[l].reshape(1, POOL_WIDTH)
        cb = conv_b.reshape(DEPTH, 1, 2 * D_FF)
        last = l == DEPTH - 1

        k_l, v_l, x1p, h2p, w_up_l, w_down_l = _ctx_front(
            l, xp, seq, mods, n1g, w_in_l, blockdiag, qg, kg, w_pool_b, ps, w_out_l, n2g,
            (w_up, w_down))
        ks_out.append(k_l)
        vs_out.append(v_l)
        xp = _ffn(l, x1p[None], h2p[None], seq, 0, mods, w_up_l, conv_w, cb, w_down_l,
                  final_g, last)[0]

        q, k, v, mixed = _lat_proj(l, xs, mods, n1g, w_in_l, blockdiag, qg, kg, cos_t, sin_t,
                                   w_pool_b, ps)
        x1s, h2s, *next_weights = _lat_attn(l, q, k, v, ck, cv, mixed, xs, mods, w_out_l, n2g,
                                            () if last else (w_in, w_out), l + 1)
        xs = _ffn(l, x1s, h2s, dec_seq, 1, mods, w_up_l, conv_w, cb, w_down_l, final_g, last)
        if not last:
            w_in_l, w_out_l = next_weights

    y_prompt = xp.reshape(batch, seq, d)

    def cache_layout(per_layer):
        t = jnp.stack(per_layer, axis=1).reshape(batch, DEPTH, N_KV_HEADS, HEAD_DIM, seq)
        return jnp.transpose(t, (0, 1, 4, 2, 3))

    new_cache_k = cache_layout(ks_out)
    new_cache_v = cache_layout(vs_out)
    return (y_prompt, xs, new_cache_k, new_cache_v)
```

```python
import functools

import jax
import jax.numpy as jnp
from jax import lax
from jax.experimental import pallas as pl
from jax.experimental.pallas import tpu as pltpu

D_MODEL = 1024
DEPTH = 4
GRID_W = 64
HEAD_DIM = 64
N_HEADS = 8
N_KV_HEADS = 2
ATTN_WIDTH = N_HEADS * HEAD_DIM
KV_WIDTH = N_KV_HEADS * HEAD_DIM
POOL_WIDTH = D_MODEL - ATTN_WIDTH
POOL_WINDOWS = (2, 4, 8, 16)
POOL_GROUP_DIM = 128
IN_WIDTH = ATTN_WIDTH + 2 * KV_WIDTH + POOL_WIDTH
D_FF = 2816
ROPE_THETA = 10000.0
ROPE_AXIS_DIM = HEAD_DIM // 2
ROPE_PAIR_DIM = HEAD_DIM // 4
EPS = 1e-6
N_MOD = 6
LOG2_E = 1.4426950408889634

LANES = 128
F32_SUBLANES = 8
BF16_SUBLANES = 16
VMEM_LIMIT = 56 * 1024 * 1024

MOD_ROWS = F32_SUBLANES
MOD_COL_BLOCKS = 4
POOL_HALO = 8
FFN_CHUNK = 256
CTX_TILE = 1024
LAT_TILE = 1024
LAT_SUB_TILE = 512
LAT_SUB_SKEW = 1
LAT_Q_TILE = 512
LAT_Q_BLOCK = 128
FFN_TILE = 512
LAT_SCORE_LOOKAHEAD = 4
CTX_SCORE_LOOKAHEAD = 2
CTX_SEQ_SKEW = 1

BF16 = jnp.bfloat16
F32 = jnp.float32


def _dot(a, b):
    return jnp.dot(a, b, preferred_element_type=F32)


def _dot_nt(a, b):
    return lax.dot_general(a, b, (((1,), (1,)), ((), ())), preferred_element_type=F32)


def _rms_mod(x, g, scale1p, shift):
    ms = jnp.mean(x * x, axis=-1, keepdims=True)
    return (x * lax.rsqrt(ms + EPS) * g) * scale1p + shift


def _head_norm(t, g_tiled):
    lo = lax.broadcasted_iota(jnp.int32, (t.shape[0], LANES), 1) < HEAD_DIM
    outs = []
    for ci in range(t.shape[1] // LANES):
        tc = t[:, ci * LANES:(ci + 1) * LANES]
        sq = tc * tc
        s_lo = jnp.sum(jnp.where(lo, sq, 0.0), axis=-1, keepdims=True)
        s_hi = jnp.sum(jnp.where(lo, 0.0, sq), axis=-1, keepdims=True)
        ssq = jnp.where(lo, s_lo, s_hi)
        outs.append(tc * lax.rsqrt(ssq * (1.0 / HEAD_DIM) + EPS))
    y = outs[0] if len(outs) == 1 else jnp.concatenate(outs, axis=1)
    return y * g_tiled


def _rope(t, cos_t, sin_t):
    lane = lax.broadcasted_iota(jnp.int32, (t.shape[0], LANES), 1)
    first = (lane % ROPE_AXIS_DIM) < ROPE_PAIR_DIM
    outs = []
    for ci in range(t.shape[1] // LANES):
        tc = t[:, ci * LANES:(ci + 1) * LANES]
        partner = jnp.where(first, pltpu.roll(tc, LANES - ROPE_PAIR_DIM, 1),
                            pltpu.roll(tc, ROPE_PAIR_DIM, 1))
        outs.append(tc * cos_t + partner * sin_t)
    return outs[0] if len(outs) == 1 else jnp.concatenate(outs, axis=1)


def _dup_halves(t):
    lane = lax.broadcasted_iota(jnp.int32, t.shape, 1)
    lo = lane < HEAD_DIM
    sw = pltpu.roll(t, HEAD_DIM, 1)
    return jnp.where(lo, t, sw).astype(BF16), jnp.where(lo, sw, t).astype(BF16)


def _value_rows(vt):
    ones = jnp.ones((HEAD_DIM, vt.shape[1]), F32)
    out = []
    for g in range(N_KV_HEADS):
        vg = vt[g * HEAD_DIM:(g + 1) * HEAD_DIM]
        out.append(jnp.concatenate([vg, ones], axis=0).astype(BF16))
        out.append(jnp.concatenate([ones, vg], axis=0).astype(BF16))
    return out


def _interleave(chains, skew):
    live = list(enumerate(chains))
    t = 0
    while live:
        for i, chain in list(live):
            if t >= i * skew and next(chain, StopIteration) is StopIteration:
                live.remove((i, chain))
        t += 1


def _run(chain):
    while True:
        try:
            next(chain)
        except StopIteration as stop:
            return stop.value


def _attend_keys_major(q, kdup_ref, vt_ref, q_block, lookahead):
    tq = q.shape[0]
    lane = lax.broadcasted_iota(jnp.int32, (1, LANES), 1)
    masks = ((lane < HEAD_DIM).astype(BF16), (lane >= HEAD_DIM).astype(BF16))
    units = [(r, g, half) for r in range(0, tq, q_block) for g in range(N_KV_HEADS) for half in range(2)]

    def scores(r, g, half):
        q_a = q[r:r + q_block, (2 * g) * LANES:(2 * g + 1) * LANES]
        q_b = q[r:r + q_block, (2 * g + 1) * LANES:(2 * g + 2) * LANES]
        q_rows = jnp.concatenate([q_a * masks[half], q_b * masks[half]], axis=0)
        return _dot_nt(kdup_ref[g], q_rows)

    def values(r, g, half, st):
        pt = jnp.exp2(st - jnp.max(st, axis=0, keepdims=True)).astype(BF16)
        return _dot(vt_ref[2 * g + half], pt).T

    outs = {}
    pending = []
    for unit in units[:lookahead]:
        pending.append(scores(*unit))
        yield
    for i, unit in enumerate(units):
        if i + lookahead < len(units):
            pending.append(scores(*units[i + lookahead]))
            yield
        outs[unit] = values(*unit, pending.pop(0))
        yield

    lo = lax.broadcasted_iota(jnp.int32, (2 * q_block, LANES), 1) < HEAD_DIM
    rows = []
    for r in range(0, tq, q_block):
        blocks = []
        for g in range(N_KV_HEADS):
            o_even, o_odd = outs[(r, g, 0)], outs[(r, g, 1)]
            num = jnp.where(lo, o_even, o_odd)
            den = jnp.where(lo, pltpu.roll(o_even, HEAD_DIM, 1), pltpu.roll(o_odd, HEAD_DIM, 1))
            out = num / den
            blocks.extend([out[0:q_block], out[q_block:2 * q_block]])
        rows.append(jnp.concatenate(blocks, axis=1))
    return rows[0] if len(rows) == 1 else jnp.concatenate(rows, axis=0)


def _pool_mix(u, t_loc, seq_len, wpool_ref, lo_row, n_rows):
    m = u.shape[0]
    outs = []
    for gi, w in enumerate(POOL_WINDOWS):
        ug = u[:, gi * LANES:(gi + 1) * LANES]
        half = w // 2
        past, future, d = ug, ug, 1
        while d < half:
            past = past + jnp.where(t_loc >= d, pltpu.roll(past, d, 0), 0.0)
            future = future + jnp.where(t_loc + d < seq_len, pltpu.roll(future, m - d, 0), 0.0)
            d *= 2
        total = jnp.where(t_loc >= 1, pltpu.roll(past, 1, 0), 0.0) + future
        cnt = jnp.minimum(t_loc + half, seq_len) - jnp.maximum(t_loc - half, 0)
        pooled = total / cnt.astype(F32) - ug
        pooled = pooled[lo_row:lo_row + n_rows]
        outs.append(_dot(pooled.astype(BF16), wpool_ref[gi]))
    return jnp.concatenate(outs, axis=1)


def _mix_residual_norm2(x, attn, mixed, wout_ref, mod_ref, n2g_ref, x1_out, h2_out, rows=slice(None)):
    gt1, sh2, sc2 = mod_ref[2:3, :], mod_ref[3:4, :], mod_ref[4:5, :]
    mix = (_dot(attn.astype(BF16), wout_ref[0:ATTN_WIDTH, :])
           + _dot(mixed.astype(BF16), wout_ref[ATTN_WIDTH:D_MODEL, :]))
    x1 = x + gt1 * mix
    x1_out[rows, :] = x1
    h2_out[rows, :] = _rms_mod(x1, n2g_ref[...], 1.0 + sc2, sh2).astype(BF16)


def _cast_specs(weights, layer, n_steps, step_index):
    in_specs, out_specs, out_shapes = [], [], []
    for w in weights:
        _, rows, cols = w.shape
        blk = rows // n_steps
        assert blk * n_steps == rows and blk % BF16_SUBLANES == 0
        in_specs.append(pl.BlockSpec((None, blk, cols), lambda *g: (layer, step_index(*g), 0)))
        out_specs.append(pl.BlockSpec((blk, cols), lambda *g: (step_index(*g), 0)))
        out_shapes.append(jax.ShapeDtypeStruct((rows, cols), BF16))
    return in_specs, out_specs, out_shapes


def _cast_blocks(src_refs, dst_refs):
    for src, dst in zip(src_refs, dst_refs, strict=True):
        dst[...] = src[...].astype(BF16)


def _mod_kernel(cond_ref, w_ref, b_ref, out_ref):
    s = jax.nn.silu(cond_ref[...]).astype(BF16)
    out_ref[...] = _dot(s, w_ref[...].astype(BF16)) + b_ref[...]


def _modulation(cond8, w_mod, b_mod):
    n = N_MOD * D_MODEL
    tn = n // MOD_COL_BLOCKS
    return pl.pallas_call(
        _mod_kernel,
        out_shape=jax.ShapeDtypeStruct((DEPTH, MOD_ROWS, n), F32),
        grid=(DEPTH, MOD_COL_BLOCKS),
        in_specs=[
            pl.BlockSpec((MOD_ROWS, D_MODEL), lambda l, j: (0, 0)),
            pl.BlockSpec((None, D_MODEL, tn), lambda l, j: (l, 0, j)),
            pl.BlockSpec((None, 1, tn), lambda l, j: (l, 0, j)),
        ],
        out_specs=pl.BlockSpec((None, MOD_ROWS, tn), lambda l, j: (l, 0, j)),
        compiler_params=pltpu.CompilerParams(
            dimension_semantics=("arbitrary", "arbitrary"), vmem_limit_bytes=VMEM_LIMIT),
        name="modulation",
    )(cond8, w_mod, b_mod.reshape(DEPTH, 1, n))


def _ctx_front_kernel(*refs, seq_len, n_cast):
    (x_ref, mod_ref, n1g_ref, win_ref, qg_ref, kg_ref, wpool_ref, pscale_ref, wout_ref,
     n2g_ref) = refs[:10]
    cast_src = refs[10:10 + n_cast]
    kt_out, vt_out, x1_out, h2_out = refs[10 + n_cast:14 + n_cast]
    _cast_blocks(cast_src, refs[14 + n_cast:])
    tm = x_ref.shape[0]
    sh1, sc1 = mod_ref[0:1, :], mod_ref[1:2, :]
    t_loc = lax.broadcasted_iota(jnp.int32, (seq_len, LANES), 0)

    def sequence(s):
        rows = slice(s * seq_len, (s + 1) * seq_len)
        x = x_ref[rows, :]
        h = _rms_mod(x, n1g_ref[...], 1.0 + sc1, sh1).astype(BF16)
        proj = _dot(h, win_ref[...])
        yield
        q = (_head_norm(proj[:, 0:ATTN_WIDTH], qg_ref[...])
             * (HEAD_DIM ** -0.5 * LOG2_E)).astype(BF16)
        k = _head_norm(proj[:, ATTN_WIDTH:ATTN_WIDTH + KV_WIDTH], kg_ref[...])
        v = proj[:, ATTN_WIDTH + KV_WIDTH:ATTN_WIDTH + 2 * KV_WIDTH]
        u = proj[:, ATTN_WIDTH + 2 * KV_WIDTH:IN_WIDTH]
        kt_out[s] = k.T
        vt = v.T
        vt_out[s] = vt
        attn = yield from _attend_keys_major(q, _dup_halves(k), _value_rows(vt), seq_len,
                                             CTX_SCORE_LOOKAHEAD)
        mixed = _pool_mix(u, t_loc, seq_len, wpool_ref, 0, seq_len) * pscale_ref[...]
        yield
        _mix_residual_norm2(x, attn, mixed, wout_ref, mod_ref, n2g_ref, x1_out, h2_out, rows)

    _interleave([sequence(s) for s in range(tm // seq_len)], CTX_SEQ_SKEW)


def _ctx_front(layer, xp, seq_len, mods, norm1_g, w_in, qg, kg, w_pool, pool_scale, w_out, norm2_g,
               cast_weights):
    n_tok = xp.shape[0]
    tm = CTX_TILE
    const2 = lambda i: (0, 0)
    tile = lambda i: (i, 0)
    cast_in, cast_out, cast_shapes = _cast_specs(cast_weights, layer, n_tok // tm, lambda i: i)
    return pl.pallas_call(
        functools.partial(_ctx_front_kernel, seq_len=seq_len, n_cast=len(cast_weights)),
        out_shape=(
            jax.ShapeDtypeStruct((n_tok // seq_len, KV_WIDTH, seq_len), F32),
            jax.ShapeDtypeStruct((n_tok // seq_len, KV_WIDTH, seq_len), F32),
            jax.ShapeDtypeStruct((n_tok, D_MODEL), F32),
            jax.ShapeDtypeStruct((n_tok, D_MODEL), BF16),
            *cast_shapes,
        ),
        grid=(n_tok // tm,),
        in_specs=[
            pl.BlockSpec((tm, D_MODEL), tile),
            pl.BlockSpec((None, None, N_MOD, D_MODEL), lambda i: (layer, 0, 0, 0)),
            pl.BlockSpec((1, D_MODEL), const2),
            pl.BlockSpec((D_MODEL, IN_WIDTH), const2),
            pl.BlockSpec((1, ATTN_WIDTH), const2),
            pl.BlockSpec((1, KV_WIDTH), const2),
            pl.BlockSpec((None, len(POOL_WINDOWS), POOL_GROUP_DIM, POOL_GROUP_DIM), lambda i: (layer, 0, 0, 0)),
            pl.BlockSpec((1, POOL_WIDTH), const2),
            pl.BlockSpec((D_MODEL, D_MODEL), const2),
            pl.BlockSpec((1, D_MODEL), const2),
            *cast_in,
        ],
        out_specs=(
            pl.BlockSpec((tm // seq_len, KV_WIDTH, seq_len), lambda i: (i, 0, 0)),
            pl.BlockSpec((tm // seq_len, KV_WIDTH, seq_len), lambda i: (i, 0, 0)),
            pl.BlockSpec((tm, D_MODEL), tile),
            pl.BlockSpec((tm, D_MODEL), tile),
            *cast_out,
        ),
        compiler_params=pltpu.CompilerParams(
            dimension_semantics=("arbitrary",), vmem_limit_bytes=VMEM_LIMIT),
        name=f"ctx_front_{layer}",
    )(xp, mods, norm1_g, w_in, qg, kg, w_pool, pool_scale, w_out, norm2_g, *cast_weights)


def _lat_proj_kernel(xprev_ref, x_ref, xnext_ref, mod_ref, n1g_ref, win_ref, qg_ref, kg_ref,
                     cos_ref, sin_ref, wpool_ref, pscale_ref, q_out, k_out, v_out, mixed_out, *, seq_len):
    tm = x_ref.shape[0]
    halo = POOL_HALO
    sub = LAT_SUB_TILE
    sh1, sc1 = mod_ref[0:1, :], mod_ref[1:2, :]
    tile_start = pl.program_id(1) * tm

    def sub_tile(r0):
        rows = slice(r0, r0 + sub)
        before = xprev_ref[...] if r0 == 0 else x_ref[r0 - halo:r0, :]
        after = xnext_ref[...] if r0 + sub == tm else x_ref[r0 + sub:r0 + sub + halo, :]
        xe = jnp.concatenate([before, x_ref[rows, :], after], axis=0)
        h = _rms_mod(xe, n1g_ref[...], 1.0 + sc1, sh1).astype(BF16)
        proj = _dot(h, win_ref[...])
        yield
        main = proj[halo:halo + sub]
        cos_t, sin_t = cos_ref[rows, :], sin_ref[rows, :]
        q = _head_norm(main[:, 0:ATTN_WIDTH], qg_ref[...])
        q_out[rows, :] = (_rope(q, cos_t, sin_t) * (HEAD_DIM ** -0.5 * LOG2_E)).astype(BF16)
        k = _head_norm(main[:, ATTN_WIDTH:ATTN_WIDTH + KV_WIDTH], kg_ref[...])
        k_out[rows, :] = _rope(k, cos_t, sin_t).astype(BF16)
        v_out[rows, :] = main[:, ATTN_WIDTH + KV_WIDTH:ATTN_WIDTH + 2 * KV_WIDTH].astype(BF16)
        u = proj[:, ATTN_WIDTH + 2 * KV_WIDTH:IN_WIDTH]
        t_loc = (lax.broadcasted_iota(jnp.int32, (sub + 2 * halo, LANES), 0)
                 + (tile_start + r0 - halo))
        mixed = _pool_mix(u, t_loc, seq_len, wpool_ref, halo, sub) * pscale_ref[...]
        mixed_out[rows, :] = mixed.astype(BF16)

    _interleave([sub_tile(r0) for r0 in range(0, tm, sub)], LAT_SUB_SKEW)


def _lat_proj(layer, xs, mods, norm1_g, w_in, qg, kg, cos_t, sin_t, w_pool, pool_scale):
    nb, seq_len, _ = xs.shape
    tm = LAT_TILE
    halo = POOL_HALO
    n_halo_blocks = seq_len // halo
    const2 = lambda b, i: (0, 0)
    tile = lambda b, i: (b, i, 0)
    return pl.pallas_call(
        functools.partial(_lat_proj_kernel, seq_len=seq_len),
        out_shape=(
            jax.ShapeDtypeStruct((nb, seq_len, ATTN_WIDTH), BF16),
            jax.ShapeDtypeStruct((nb, seq_len, KV_WIDTH), BF16),
            jax.ShapeDtypeStruct((nb, seq_len, KV_WIDTH), BF16),
            jax.ShapeDtypeStruct((nb, seq_len, POOL_WIDTH), BF16),
        ),
        grid=(nb, seq_len // tm),
        in_specs=[
            pl.BlockSpec((None, halo, D_MODEL),
                         lambda b, i: (b, jnp.maximum(i * (tm // halo) - 1, 0), 0)),
            pl.BlockSpec((None, tm, D_MODEL), tile),
            pl.BlockSpec((None, halo, D_MODEL),
                         lambda b, i: (b, jnp.minimum((i + 1) * (tm // halo), n_halo_blocks - 1), 0)),
            pl.BlockSpec((None, None, N_MOD, D_MODEL), lambda b, i: (layer, 1 + b, 0, 0)),
            pl.BlockSpec((1, D_MODEL), const2),
            pl.BlockSpec((D_MODEL, IN_WIDTH), const2),
            pl.BlockSpec((1, ATTN_WIDTH), const2),
            pl.BlockSpec((1, KV_WIDTH), const2),
            pl.BlockSpec((tm, LANES), lambda b, i: (i, 0)),
            pl.BlockSpec((tm, LANES), lambda b, i: (i, 0)),
            pl.BlockSpec((None, len(POOL_WINDOWS), POOL_GROUP_DIM, POOL_GROUP_DIM), lambda b, i: (layer, 0, 0, 0)),
            pl.BlockSpec((1, POOL_WIDTH), const2),
        ],
        out_specs=(
            pl.BlockSpec((None, tm, ATTN_WIDTH), tile),
            pl.BlockSpec((None, tm, KV_WIDTH), tile),
            pl.BlockSpec((None, tm, KV_WIDTH), tile),
            pl.BlockSpec((None, tm, POOL_WIDTH), tile),
        ),
        compiler_params=pltpu.CompilerParams(
            dimension_semantics=("arbitrary", "arbitrary"), vmem_limit_bytes=VMEM_LIMIT),
        name=f"lat_proj_{layer}",
    )(xs, xs, xs, mods, norm1_g, w_in, qg, kg, cos_t, sin_t, w_pool, pool_scale)


def _lat_attn_kernel(*refs, n_cast):
    (q_ref, klat_ref, vlat_ref, ck_ref, cv_ref, mixed_ref, x_ref, mod_ref, wout_ref,
     n2g_ref) = refs[:10]
    cast_src = refs[10:10 + n_cast]
    x1_out, h2_out = refs[10 + n_cast:12 + n_cast]
    cast_dst = refs[12 + n_cast:12 + 2 * n_cast]
    kdup_scr, vt_scr = refs[12 + 2 * n_cast:]
    _cast_blocks(cast_src, cast_dst)

    @pl.when(pl.program_id(1) == 0)
    def _():
        k_all = jnp.concatenate([ck_ref[...].T, klat_ref[...].astype(F32)], axis=0)
        vt_all = jnp.concatenate([cv_ref[...], vlat_ref[...].astype(F32).T], axis=1)
        k0, k1 = _dup_halves(k_all)
        kdup_scr[0] = k0
        kdup_scr[1] = k1
        for i, vt in enumerate(_value_rows(vt_all)):
            vt_scr[i] = vt

    attn = _run(_attend_keys_major(q_ref[...], kdup_scr, vt_scr, LAT_Q_BLOCK, LAT_SCORE_LOOKAHEAD))
    _mix_residual_norm2(x_ref[...], attn, mixed_ref[...], wout_ref, mod_ref, n2g_ref, x1_out, h2_out)


def _lat_attn(layer, q, k, v, cache_k, cache_v, mixed, xs, mods, w_out, norm2_g, cast_weights, cast_layer):
    nb, seq_len, _ = xs.shape
    past = cache_k.shape[3]
    tq = LAT_Q_TILE
    n_keys = past + seq_len
    n_tiles = seq_len // tq
    tile = lambda b, j: (b, j, 0)
    whole = lambda b, j: (b, 0, 0)
    cast_in, cast_out, cast_shapes = _cast_specs(cast_weights, cast_layer, nb * n_tiles,
                                                 lambda b, j: b * n_tiles + j)
    return pl.pallas_call(
        functools.partial(_lat_attn_kernel, n_cast=len(cast_weights)),
        out_shape=(
            jax.ShapeDtypeStruct((nb, seq_len, D_MODEL), F32),
            jax.ShapeDtypeStruct((nb, seq_len, D_MODEL), BF16),
            *cast_shapes,
        ),
        grid=(nb, seq_len // tq),
        in_specs=[
            pl.BlockSpec((None, tq, ATTN_WIDTH), tile),
            pl.BlockSpec((None, seq_len, KV_WIDTH), whole),
            pl.BlockSpec((None, seq_len, KV_WIDTH), whole),
            pl.BlockSpec((None, None, KV_WIDTH, past), lambda b, j: (b, layer, 0, 0)),
            pl.BlockSpec((None, None, KV_WIDTH, past), lambda b, j: (b, layer, 0, 0)),
            pl.BlockSpec((None, tq, POOL_WIDTH), tile),
            pl.BlockSpec((None, tq, D_MODEL), tile),
            pl.BlockSpec((None, None, N_MOD, D_MODEL), lambda b, j: (layer, 1 + b, 0, 0)),
            pl.BlockSpec((D_MODEL, D_MODEL), lambda b, j: (0, 0)),
            pl.BlockSpec((1, D_MODEL), lambda b, j: (0, 0)),
            *cast_in,
        ],
        out_specs=(
            pl.BlockSpec((None, tq, D_MODEL), tile),
            pl.BlockSpec((None, tq, D_MODEL), tile),
            *cast_out,
        ),
        scratch_shapes=[
            pltpu.VMEM((N_KV_HEADS, n_keys, LANES), BF16),
            pltpu.VMEM((2 * N_KV_HEADS, LANES, n_keys), BF16),
        ],
        compiler_params=pltpu.CompilerParams(
            dimension_semantics=("arbitrary", "arbitrary"), vmem_limit_bytes=VMEM_LIMIT),
        name=f"lat_attn_{layer}",
    )(q, k, v, cache_k, cache_v, mixed, xs, mods, w_out, norm2_g, *cast_weights)


def _ffn_kernel(*refs, seq_len, final_norm, has_halo):
    if has_halo:
        (x1_ref, h2_ref, h2prev_ref, h2next_ref, mod_ref, wup_ref, cw_ref, cb_ref, wdn_ref, fg_ref,
         out_ref, z_scr, act_scr) = refs
    else:
        (x1_ref, h2_ref, mod_ref, wup_ref, cw_ref, cb_ref, wdn_ref, fg_ref,
         out_ref, z_scr, act_scr) = refs
    tm = x1_ref.shape[0]
    gap = F32_SUBLANES
    if has_halo:
        halo = BF16_SUBLANES
        tile_start = pl.program_id(1) * tm
        h2prev = jnp.where(tile_start % seq_len != 0, h2prev_ref[...], jnp.zeros_like(h2prev_ref))
        h2next = jnp.where((tile_start + tm) % seq_len != 0, h2next_ref[...], jnp.zeros_like(h2next_ref))
        he = jnp.concatenate([h2prev, h2_ref[...], h2next], axis=0)
        segments = [(halo, tm)]
    else:
        he = h2_ref[...]
        segments = [(gap + s * (seq_len + gap), seq_len) for s in range(tm // seq_len)]
        for slab in range(z_scr.shape[0]):
            for s in range(tm // seq_len + 1):
                r = s * (seq_len + gap)
                z_scr[slab, r:r + gap, :] = jnp.zeros((gap, LANES), F32)

    def store_z(slab, z):
        if has_halo:
            z_scr[slab, 0:z.shape[0], :] = z
        else:
            for s, (row0, n) in enumerate(segments):
                z_scr[slab, row0:row0 + n, :] = z[s * n:(s + 1) * n]

    def conv_rows(slab, row, n, col0):
        cols = slice(col0, col0 + LANES)
        return (z_scr[slab, pl.ds(row - 1, n, stride=1), :] * cw_ref[0:1, cols]
                + z_scr[slab, row:row + n, :] * cw_ref[1:2, cols]
                + z_scr[slab, pl.ds(row + 1, n, stride=1), :] * cw_ref[2:3, cols]
                + cb_ref[0:1, cols])

    n_sub = FFN_CHUNK // LANES
    for ci in range(D_FF // FFN_CHUNK):
        c0 = ci * FFN_CHUNK
        slab0 = 2 * n_sub * (ci % 2)
        za = _dot(he, wup_ref[:, c0:c0 + FFN_CHUNK])
        zg = _dot(he, wup_ref[:, D_FF + c0:D_FF + c0 + FFN_CHUNK])
        for j in range(n_sub):
            lanes = slice(j * LANES, (j + 1) * LANES)
            slab_a, slab_g = slab0 + 2 * j, slab0 + 2 * j + 1
            col = c0 + j * LANES
            store_z(slab_a, za[:, lanes])
            store_z(slab_g, zg[:, lanes])
            for s, (row0, n) in enumerate(segments):
                a = conv_rows(slab_a, row0, n, col)
                g = conv_rows(slab_g, row0, n, D_FF + col)
                act_scr[s * n:(s + 1) * n, col:col + LANES] = (jax.nn.silu(a) * g).astype(BF16)
    x2 = x1_ref[...] + mod_ref[5:6, :] * _dot(act_scr[...], wdn_ref[...])
    if final_norm:
        ms = jnp.mean(x2 * x2, axis=-1, keepdims=True)
        x2 = x2 * lax.rsqrt(ms + EPS) * fg_ref[...]
    out_ref[...] = x2


def _ffn(layer, x1, h2, seq_len, mod_row0, mods, w_up, conv_w, conv_b, w_down, final_g, final_norm):
    nb, n_tok, _ = x1.shape
    tm = FFN_TILE
    has_halo = seq_len > tm
    assert seq_len % tm == 0 or tm % seq_len == 0
    halo = BF16_SUBLANES
    gap = F32_SUBLANES
    n_halo_blocks = n_tok // halo
    tile = lambda b, i: (b, i, 0)
    halo_specs = [
        pl.BlockSpec((None, halo, D_MODEL),
                     lambda b, i: (b, jnp.maximum(i * (tm // halo) - 1, 0), 0)),
        pl.BlockSpec((None, halo, D_MODEL),
                     lambda b, i: (b, jnp.minimum((i + 1) * (tm // halo), n_halo_blocks - 1), 0)),
    ] if has_halo else []
    halo_args = [h2, h2] if has_halo else []
    z_rows = tm + 2 * halo if has_halo else gap + (tm // seq_len) * (seq_len + gap)
    return pl.pallas_call(
        functools.partial(_ffn_kernel, seq_len=seq_len, final_norm=final_norm, has_halo=has_halo),
        out_shape=jax.ShapeDtypeStruct((nb, n_tok, D_MODEL), F32),
        grid=(nb, n_tok // tm),
        in_specs=[
            pl.BlockSpec((None, tm, D_MODEL), tile),
            pl.BlockSpec((None, tm, D_MODEL), tile),
            *halo_specs,
            pl.BlockSpec((None, None, N_MOD, D_MODEL), lambda b, i: (layer, mod_row0 + b, 0, 0)),
            pl.BlockSpec((D_MODEL, 2 * D_FF), lambda b, i: (0, 0), pipeline_mode=pl.Buffered(1)),
            pl.BlockSpec((None, 3, 2 * D_FF), lambda b, i: (layer, 0, 0)),
            pl.BlockSpec((None, 1, 2 * D_FF), lambda b, i: (layer, 0, 0)),
            pl.BlockSpec((D_FF, D_MODEL), lambda b, i: (0, 0), pipeline_mode=pl.Buffered(1)),
            pl.BlockSpec((1, D_MODEL), lambda b, i: (0, 0)),
        ],
        out_specs=pl.BlockSpec((None, tm, D_MODEL), tile),
        scratch_shapes=[
            pltpu.VMEM((4 * (FFN_CHUNK // LANES), z_rows, LANES), F32),
            pltpu.VMEM((tm, D_FF), BF16),
        ],
        compiler_params=pltpu.CompilerParams(
            dimension_semantics=("arbitrary", "arbitrary"), vmem_limit_bytes=VMEM_LIMIT),
        name=f"ffn_{'lat' if nb > 1 else 'ctx'}_{layer}",
    )(x1, h2, *halo_args, mods, w_up, conv_w, conv_b, w_down, final_g)


def _rope_tables(n_tokens):
    t = jnp.arange(n_tokens)
    row = (t // GRID_W).astype(F32)
    col = (t % GRID_W).astype(F32)
    n_freq = HEAD_DIM // 4
    inv = ROPE_THETA ** (-jnp.arange(n_freq, dtype=F32) / n_freq)
    ang = jnp.stack([row[:, None] * inv, col[:, None] * inv], axis=1)
    cos = jnp.broadcast_to(jnp.cos(ang)[:, :, None, :], (n_tokens, 2, 2, n_freq))
    sin = jnp.sin(ang)[:, :, None, :] * jnp.array([-1.0, 1.0], F32)[None, None, :, None]
    cos = cos.reshape(n_tokens, HEAD_DIM)
    sin = sin.reshape(n_tokens, HEAD_DIM)
    return jnp.tile(cos, (1, 2)), jnp.tile(sin, (1, 2))


def kernel(x_prompt, x_sample, cache_k, cache_v, c, c_ctx, w_mod, b_mod, norm1_g, w_in, q_norm_g, k_norm_g, w_pool, pool_scale, w_out, norm2_g, w_up, conv_w, conv_b, w_down, final_norm_g):
    batch, seq, d = x_prompt.shape
    dec_batch, dec_seq, _ = x_sample.shape
    past = cache_k.shape[2]

    w_in_l = w_in[0].astype(BF16)
    w_out_l = w_out[0].astype(BF16)
    w_pool_b = w_pool.astype(BF16)

    assert 1 + dec_batch <= MOD_ROWS
    cond8 = jnp.zeros((MOD_ROWS, d), F32).at[0].set(c_ctx).at[1:1 + dec_batch].set(c)
    mods = _modulation(cond8, w_mod, b_mod).reshape(DEPTH, MOD_ROWS, N_MOD, d)

    cos_t, sin_t = _rope_tables(dec_seq)
    ck = jnp.transpose(cache_k, (0, 1, 3, 4, 2)).reshape(dec_batch, DEPTH, KV_WIDTH, past)
    cv = jnp.transpose(cache_v, (0, 1, 3, 4, 2)).reshape(dec_batch, DEPTH, KV_WIDTH, past)
    final_g = final_norm_g.reshape(1, d)

    xp = x_prompt.reshape(batch * seq, d)
    xs = x_sample
    ks_out, vs_out = [], []
    for l in range(DEPTH):
        n1g = norm1_g[l].reshape(1, d)
        n2g = norm2_g[l].reshape(1, d)
        qg = jnp.tile(q_norm_g[l], N_HEADS).reshape(1, ATTN_WIDTH)
        kg = jnp.tile(k_norm_g[l], N_KV_HEADS).reshape(1, KV_WIDTH)
        ps = pool_scale[l].reshape(1, POOL_WIDTH)
        cb = conv_b.reshape(DEPTH, 1, 2 * D_FF)
        last = l == DEPTH - 1

        k_l, v_l, x1p, h2p, w_up_l, w_down_l = _ctx_front(
            l, xp, seq, mods, n1g, w_in_l, qg, kg, w_pool_b, ps, w_out_l, n2g,
            (w_up, w_down))
        ks_out.append(k_l)
        vs_out.append(v_l)
        xp = _ffn(l, x1p[None], h2p[None], seq, 0, mods, w_up_l, conv_w, cb, w_down_l,
                  final_g, last)[0]

        q, k, v, mixed = _lat_proj(l, xs, mods, n1g, w_in_l, qg, kg, cos_t, sin_t,
                                   w_pool_b, ps)
        x1s, h2s, *next_weights = _lat_attn(l, q, k, v, ck, cv, mixed, xs, mods, w_out_l, n2g,
                                            () if last else (w_in, w_out), l + 1)
        xs = _ffn(l, x1s, h2s, dec_seq, 1, mods, w_up_l, conv_w, cb, w_down_l, final_g, last)
        if not last:
            w_in_l, w_out_l = next_weights

    y_prompt = xp.reshape(batch, seq, d)

    def cache_layout(per_layer):
        t = jnp.stack(per_layer, axis=1).reshape(batch, DEPTH, N_KV_HEADS, HEAD_DIM, seq)
        return jnp.transpose(t, (0, 1, 4, 2, 3))

    new_cache_k = cache_layout(ks_out)
    new_cache_v = cache_layout(vs_out)
    return (y_prompt, xs, new_cache_k, new_cache_v)
```
